```python
import jax, jax.numpy as jnp
from jax import lax
import numpy as np

D_MODEL = 4096
BATCH = 1
SEQ = 8192
DEPTH = 1
DEC_BATCH = 8
DEC_SEQ = 64
PAST_LEN = 2048

CHUNK = 64
GMLP_CHUNK = 128
GMLP_GROUP_DIM = 128
GMLP_WIDTH = D_MODEL // 2
GMLP_GROUPS = GMLP_WIDTH // GMLP_GROUP_DIM
V_HEAD_DIM = 128
QK_NOPE_DIM = 128
QK_ROPE_DIM = 64
MLA_HEADS = (D_MODEL - GMLP_WIDTH) // V_HEAD_DIM
MLA_WIDTH = MLA_HEADS * V_HEAD_DIM
MIX_WIDTH = GMLP_WIDTH + MLA_WIDTH
KV_LORA_RANK = 512
Q_LORA_RANK = D_MODEL // 4
D_FF = 256 * ((8 * D_MODEL // 3 + 255) // 256)
N_MOD = 9
QBLOCK = 128
ROPE_THETA = 10000.0
LN_EPS = 1e-5
RMS_EPS = 1e-6
DEEPNORM_ALPHA = (2 * DEPTH) ** 0.25
DEEPNORM_BETA = (8 * DEPTH) ** -0.25
IN_WIDTH = 2 * GMLP_WIDTH + Q_LORA_RANK + KV_LORA_RANK + QK_ROPE_DIM

kernel_name = 'hymba_gmlp_mla_macaron_deepnorm_adaln_stream_step'


def _layernorm(x, g, b):
    xf = x.astype(jnp.float32)
    mu = jnp.mean(xf, axis=-1, keepdims=True)
    var = jnp.mean(jnp.square(xf - mu), axis=-1, keepdims=True)
    y = (xf - mu) * lax.rsqrt(var + LN_EPS) * g.astype(jnp.float32) + b.astype(jnp.float32)
    return y.astype(x.dtype)


def _rmsnorm(x, g):
    xf = x.astype(jnp.float32)
    y = xf * lax.rsqrt(jnp.mean(jnp.square(xf), axis=-1, keepdims=True) + RMS_EPS) * g.astype(jnp.float32)
    return y.astype(x.dtype)


def _rope_tables(pos):
    inv = 1.0 / (ROPE_THETA ** (jnp.arange(0, QK_ROPE_DIM, 2, dtype=jnp.float32) / QK_ROPE_DIM))
    ang = pos.astype(jnp.float32)[:, None] * inv[None, :]
    return jnp.cos(ang), jnp.sin(ang)


def _apply_rope(x, cos, sin):
    x1, x2 = jnp.split(x.astype(jnp.float32), 2, axis=-1)
    return jnp.concatenate([x1 * cos - x2 * sin, x2 * cos + x1 * sin], axis=-1).astype(x.dtype)


def _swiglu(h, w_gate_up, w_down):
    g, u = jnp.split(h @ w_gate_up, 2, axis=-1)
    return (jax.nn.silu(g) * u) @ w_down


def _spatial_gate(u, v_n, w_s, b_s):
    b, s, _ = v_n.shape
    L = min(s, GMLP_CHUNK)
    n = s // L
    vg = v_n.reshape(b, n, L, GMLP_GROUPS, GMLP_GROUP_DIM)
    w = jnp.tril(w_s[:, :L, :L])
    mixed = jnp.einsum('gts,bnsgc->bntgc', w, vg) + b_s[:, :L].T[None, None, :, :, None]
    return u * mixed.reshape(b, s, GMLP_WIDTH)


def _mla_attention(q_nope, q_rope, k_nope, k_rope, v, q_pos, k_pos):
    scale = (QK_NOPE_DIM + QK_ROPE_DIM) ** -0.5
    k_chunk = k_pos // CHUNK

    def block(args):
        qn, qr, qp = args
        s = jnp.einsum('bqhd,bkhd->bhqk', qn, k_nope, preferred_element_type=jnp.float32)
        s = s + jnp.einsum('bqhr,bkr->bhqk', qr, k_rope, preferred_element_type=jnp.float32)
        mask = (qp // CHUNK)[:, None] >= k_chunk[None, :]
        s = jnp.where(mask[None, None], s * scale, -jnp.inf)
        p = jax.nn.softmax(s, axis=-1).astype(v.dtype)
        return jnp.einsum('bhqk,bkhd->bqhd', p, v)

    b, sq = q_nope.shape[:2]
    if sq <= QBLOCK:
        return block((q_nope, q_rope, q_pos))
    nb = sq // QBLOCK

    def to_blocks(t):
        return jnp.moveaxis(t.reshape((b, nb, QBLOCK) + t.shape[2:]), 1, 0)

    out = lax.map(block, (to_blocks(q_nope), to_blocks(q_rope), q_pos.reshape(nb, QBLOCK)))
    return jnp.moveaxis(out, 0, 1).reshape(b, sq, MLA_HEADS, V_HEAD_DIM)


def _mixer(h, pos, cache_latent, cache_krope, w_in, gmlp_ln_g, gmlp_ln_b, gmlp_w_s, gmlp_b_s,
           mla_q_norm_g, mla_w_uq, mla_kv_norm_g, mla_w_ukv, w_out):
    b, s, _ = h.shape
    o1 = GMLP_WIDTH
    o2 = o1 + GMLP_WIDTH
    o3 = o2 + Q_LORA_RANK
    o4 = o3 + KV_LORA_RANK
    u, v, cq, ckv, kr = jnp.split(h @ w_in, [o1, o2, o3, o4], axis=-1)
    v_n = _layernorm(v, gmlp_ln_g, gmlp_ln_b)
    a_out = _spatial_gate(u, v_n, gmlp_w_s, gmlp_b_s)
    cos, sin = _rope_tables(pos)
    q = (_rmsnorm(cq, mla_q_norm_g) @ mla_w_uq).reshape(b, s, MLA_HEADS, QK_NOPE_DIM + QK_ROPE_DIM)
    q_nope = q[..., :QK_NOPE_DIM]
    q_rope = _apply_rope(q[..., QK_NOPE_DIM:], cos[:, None, :], sin[:, None, :])
    lat = _rmsnorm(ckv, mla_kv_norm_g)
    k_rope = _apply_rope(kr, cos, sin)
    if cache_latent is None:
        lat_all, kr_all, k_pos = lat, k_rope, pos
    else:
        lat_all = jnp.concatenate([cache_latent.astype(lat.dtype), lat], axis=1)
        kr_all = jnp.concatenate([cache_krope.astype(k_rope.dtype), k_rope], axis=1)
        k_pos = jnp.arange(lat_all.shape[1], dtype=jnp.int32)
    kv = (lat_all @ mla_w_ukv).reshape(b, lat_all.shape[1], MLA_HEADS, QK_NOPE_DIM + V_HEAD_DIM)
    k_nope, v_h = kv[..., :QK_NOPE_DIM], kv[..., QK_NOPE_DIM:]
    o = _mla_attention(q_nope, q_rope, k_nope, kr_all, v_h, pos, k_pos)
    out = jnp.concatenate([a_out, o.reshape(b, s, MLA_WIDTH)], axis=-1) @ w_out
    return out, lat, k_rope, v_n


def _layer(x, c, pos, cache_latent, cache_krope, w_ada, b_ada, ffn1_w_gate_up, ffn1_w_down,
           ln1_g, ln1_b, w_in, gmlp_ln_g, gmlp_ln_b, gmlp_w_s, gmlp_b_s, mla_q_norm_g, mla_w_uq,
           mla_kv_norm_g, mla_w_ukv, w_out, ln2_g, ln2_b, ffn2_w_gate_up, ffn2_w_down, ln3_g, ln3_b):
    mod = (jax.nn.silu(c) @ w_ada + b_ada)[:, None, :]
    sh1, sc1, g1, sh2, sc2, g2, sh3, sc3, g3 = jnp.split(mod, N_MOD, axis=-1)
    h = x * (1 + sc1) + sh1
    x = _layernorm(DEEPNORM_ALPHA * x + 0.5 * g1 * _swiglu(h, ffn1_w_gate_up, ffn1_w_down), ln1_g, ln1_b)
    h = x * (1 + sc2) + sh2
    m, lat, k_rope, v_n = _mixer(h, pos, cache_latent, cache_krope, w_in, gmlp_ln_g, gmlp_ln_b,
                                 gmlp_w_s, gmlp_b_s, mla_q_norm_g, mla_w_uq, mla_kv_norm_g,
                                 mla_w_ukv, w_out)
    x = _layernorm(DEEPNORM_ALPHA * x + g2 * m, ln2_g, ln2_b)
    h = x * (1 + sc3) + sh3
    x = _layernorm(DEEPNORM_ALPHA * x + 0.5 * g3 * _swiglu(h, ffn2_w_gate_up, ffn2_w_down), ln3_g, ln3_b)
    return x, lat, k_rope, v_n


def setup_inputs(seed: int = 0) -> dict:
    key = jax.random.key(seed)
    ks = jax.random.split(key, 32)

    def nrm(k, shape, scale=1.0):
        return jax.random.normal(k, shape, dtype=jnp.float32) * scale

    d = D_MODEL
    return {
        'x_prompt': nrm(ks[0], (BATCH, SEQ, d)),
        'x_sample': nrm(ks[1], (DEC_BATCH, DEC_SEQ, d)),
        'cache_mla_latent': nrm(ks[2], (DEC_BATCH, PAST_LEN, KV_LORA_RANK)),
        'cache_mla_krope': nrm(ks[3], (DEC_BATCH, PAST_LEN, QK_ROPE_DIM)),
        'c_prompt': nrm(ks[4], (BATCH, d)),
        'c_sample': nrm(ks[5], (DEC_BATCH, d)),
        'w_ada': nrm(ks[6], (d, N_MOD * d), 0.5 * d ** -0.5),
        'b_ada': nrm(ks[7], (N_MOD * d,), 0.02),
        'ffn1_w_gate_up': nrm(ks[8], (d, 2 * D_FF), d ** -0.5),
        'ffn1_w_down': nrm(ks[9], (D_FF, d), DEEPNORM_BETA * D_FF ** -0.5),
        'ln1_g': 1.0 + nrm(ks[10], (d,), 0.02),
        'ln1_b': nrm(ks[11], (d,), 0.02),
        'w_in': nrm(ks[12], (d, IN_WIDTH), d ** -0.5),
        'gmlp_ln_g': 1.0 + nrm(ks[13], (GMLP_WIDTH,), 0.02),
        'gmlp_ln_b': nrm(ks[14], (GMLP_WIDTH,), 0.02),
        'gmlp_w_s': nrm(ks[15], (GMLP_GROUPS, GMLP_CHUNK, GMLP_CHUNK), GMLP_CHUNK ** -0.5),
        'gmlp_b_s': 1.0 + nrm(ks[16], (GMLP_GROUPS, GMLP_CHUNK), 0.02),
        'mla_q_norm_g': 1.0 + nrm(ks[17], (Q_LORA_RANK,), 0.02),
        'mla_w_uq': nrm(ks[18], (Q_LORA_RANK, MLA_HEADS * (QK_NOPE_DIM + QK_ROPE_DIM)), Q_LORA_RANK ** -0.5),
        'mla_kv_norm_g': 1.0 + nrm(ks[19], (KV_LORA_RANK,), 0.02),
        'mla_w_ukv': nrm(ks[20], (KV_LORA_RANK, MLA_HEADS * (QK_NOPE_DIM + V_HEAD_DIM)), KV_LORA_RANK ** -0.5),
        'w_out': nrm(ks[21], (MIX_WIDTH, d), DEEPNORM_BETA * MIX_WIDTH ** -0.5),
        'ln2_g': 1.0 + nrm(ks[22], (d,), 0.02),
        'ln2_b': nrm(ks[23], (d,), 0.02),
        'ffn2_w_gate_up': nrm(ks[24], (d, 2 * D_FF), d ** -0.5),
        'ffn2_w_down': nrm(ks[25], (D_FF, d), DEEPNORM_BETA * D_FF ** -0.5),
        'ln3_g': 1.0 + nrm(ks[26], (d,), 0.02),
        'ln3_b': nrm(ks[27], (d,), 0.02),
    }


def reference(x_prompt, x_sample, cache_mla_latent, cache_mla_krope, c_prompt, c_sample,
              w_ada, b_ada, ffn1_w_gate_up, ffn1_w_down, ln1_g, ln1_b, w_in, gmlp_ln_g, gmlp_ln_b,
              gmlp_w_s, gmlp_b_s, mla_q_norm_g, mla_w_uq, mla_kv_norm_g, mla_w_ukv, w_out,
              ln2_g, ln2_b, ffn2_w_gate_up, ffn2_w_down, ln3_g, ln3_b):
    past = cache_mla_latent.shape[1]
    pos_p = jnp.arange(x_prompt.shape[1], dtype=jnp.int32)
    pos_s = past + jnp.arange(x_sample.shape[1], dtype=jnp.int32)
    y_p, c_lat_p, c_kr_p = x_prompt, None, None
    y_s, c_lat_s, c_kr_s = x_sample, cache_mla_latent, cache_mla_krope
    for _ in range(DEPTH):
        y_p, lat_p, kr_p, _v_p = _layer(
            y_p, c_prompt, pos_p, c_lat_p, c_kr_p, w_ada, b_ada, ffn1_w_gate_up, ffn1_w_down,
            ln1_g, ln1_b, w_in, gmlp_ln_g, gmlp_ln_b, gmlp_w_s, gmlp_b_s, mla_q_norm_g, mla_w_uq,
            mla_kv_norm_g, mla_w_ukv, w_out, ln2_g, ln2_b, ffn2_w_gate_up, ffn2_w_down, ln3_g, ln3_b)
        y_s, lat_s, kr_s, v_s = _layer(
            y_s, c_sample, pos_s, c_lat_s, c_kr_s, w_ada, b_ada, ffn1_w_gate_up, ffn1_w_down,
            ln1_g, ln1_b, w_in, gmlp_ln_g, gmlp_ln_b, gmlp_w_s, gmlp_b_s, mla_q_norm_g, mla_w_uq,
            mla_kv_norm_g, mla_w_ukv, w_out, ln2_g, ln2_b, ffn2_w_gate_up, ffn2_w_down, ln3_g, ln3_b)
    return (y_p, y_s, lat_p, kr_p, lat_s, kr_s, v_s)
```

```python
import functools

import jax
import jax.numpy as jnp
from jax import lax
from jax.experimental import pallas as pl
from jax.experimental.pallas import tpu as pltpu

F32 = jnp.float32
BF16 = jnp.bfloat16

D_MODEL = 4096
SEQ = 8192
DEC_BATCH = 8
DEC_SEQ = 64
PAST_LEN = 2048
CHUNK = 64
CHUNK_SHIFT = 6
GMLP_CHUNK = 128
GROUP_DIM = 128
GMLP_WIDTH = D_MODEL // 2
GMLP_GROUPS = GMLP_WIDTH // GROUP_DIM
HEAD_V = 128
QK_NOPE = 128
QK_ROPE = 64
HEADS = (D_MODEL - GMLP_WIDTH) // HEAD_V
KV_RANK = 512
Q_RANK = D_MODEL // 4
D_FF = 256 * ((8 * D_MODEL // 3 + 255) // 256)
N_MOD = 9
ROPE_THETA = 10000.0
LN_EPS = 1e-5
RMS_EPS = 1e-6
ALPHA = 2.0 ** 0.25
SM_SCALE = (QK_NOPE + QK_ROPE) ** -0.5
NEG = -1e30

LANES = 128
HEAD_PAD = 2 * LANES
S_ROWS = DEC_BATCH * DEC_SEQ
M_ROWS = SEQ + S_ROWS
SEG = DEC_SEQ
N_SEG = M_ROWS // SEG
TM = M_ROWS // 8
TM_DOWN = M_ROWS // 16
TR = 4 * SEG
IN_COLS = 2 * GMLP_WIDTH + Q_RANK + KV_RANK + 2 * QK_ROPE
KV_PAD = 17 * LANES
MIB = 2 ** 20


def _params(n_axes, vmem_mib):
    return pltpu.CompilerParams(dimension_semantics=("arbitrary",) * n_axes,
                                vmem_limit_bytes=vmem_mib * MIB)


def _mod_kernel(c_ref, w_ref, b_ref, o_ref):
    c = c_ref[...]
    s = (c * jax.nn.sigmoid(c)).astype(BF16)
    o_ref[...] = jnp.dot(s, w_ref[...].astype(BF16), preferred_element_type=F32) + b_ref[...]


def _modulation(c16, w_ada, b_ada):
    tn = 512
    n = w_ada.shape[1]
    return pl.pallas_call(
        _mod_kernel,
        grid=(n // tn,),
        in_specs=[pl.BlockSpec((16, D_MODEL), lambda j: (0, 0)),
                  pl.BlockSpec((D_MODEL, tn), lambda j: (0, j)),
                  pl.BlockSpec((1, tn), lambda j: (0, j))],
        out_specs=pl.BlockSpec((16, tn), lambda j: (0, j)),
        out_shape=jax.ShapeDtypeStruct((16, n), F32),
        compiler_params=_params(1, 40),
        name="adaln_mod",
    )(c16, w_ada, b_ada.reshape(1, n))


def _seg_spec(comp):
    return pl.BlockSpec((1, 1, TR // SEG, D_MODEL), lambda i, comp=comp: (comp, i, 0, 0))


def _modulate_kernel(x_ref, sc_ref, sh_ref, h_ref):
    for s in range(TR // SEG):
        rows = slice(s * SEG, (s + 1) * SEG)
        h = x_ref[rows, :] * (1.0 + sc_ref[0, 0, s:s + 1, :]) + sh_ref[0, 0, s:s + 1, :]
        h_ref[rows, :] = h.astype(h_ref.dtype)


def _modulate(x, segmod, sc, sh):
    row = pl.BlockSpec((TR, D_MODEL), lambda i: (i, 0))
    return pl.pallas_call(
        _modulate_kernel,
        grid=(M_ROWS // TR,),
        in_specs=[row, _seg_spec(sc), _seg_spec(sh)],
        out_specs=row,
        out_shape=jax.ShapeDtypeStruct((M_ROWS, D_MODEL), BF16),
        compiler_params=_params(1, 32),
        name="modulate",
    )(x, segmod, segmod)


def _ln_kernel(x_ref, y_ref, gate_ref, g_ref, b_ref, *rest, gate_scale, with_h):
    if with_h:
        sc_ref, sh_ref, xo_ref, h_ref = rest
    else:
        (xo_ref,) = rest
    for s in range(TR // SEG):
        rows = slice(s * SEG, (s + 1) * SEG)
        z = ALPHA * x_ref[rows, :] + (gate_scale * gate_ref[0, 0, s:s + 1, :]) * y_ref[rows, :]
        mu = jnp.mean(z, axis=-1, keepdims=True)
        zc = z - mu
        var = jnp.mean(zc * zc, axis=-1, keepdims=True)
        xn = zc * lax.rsqrt(var + LN_EPS) * g_ref[...] + b_ref[...]
        xo_ref[rows, :] = xn
        if with_h:
            h = xn * (1.0 + sc_ref[0, 0, s:s + 1, :]) + sh_ref[0, 0, s:s + 1, :]
            h_ref[rows, :] = h.astype(h_ref.dtype)


def _res_ln(x, y, segmod, gate, gate_scale, ln_g, ln_b, sc=None, sh=None, *, row0=0, rows=M_ROWS):
    with_h = sc is not None
    off = row0 // TR
    row_in = pl.BlockSpec((TR, D_MODEL), lambda i: (i + off, 0))
    row_out = pl.BlockSpec((TR, D_MODEL), lambda i: (i, 0))
    vec = pl.BlockSpec((1, D_MODEL), lambda i: (0, 0))

    def seg(comp):
        return pl.BlockSpec((1, 1, TR // SEG, D_MODEL), lambda i, comp=comp: (comp, i + off, 0, 0))

    in_specs = [row_in, row_in, seg(gate), vec, vec]
    args = [x, y, segmod, ln_g.reshape(1, D_MODEL), ln_b.reshape(1, D_MODEL)]
    out_specs = [row_out]
    out_shape = [jax.ShapeDtypeStruct((rows, D_MODEL), F32)]
    if with_h:
        in_specs += [seg(sc), seg(sh)]
        args += [segmod, segmod]
        out_specs.append(row_out)
        out_shape.append(jax.ShapeDtypeStruct((rows, D_MODEL), BF16))
    return pl.pallas_call(
        functools.partial(_ln_kernel, gate_scale=gate_scale, with_h=with_h),
        grid=(rows // TR,),
        in_specs=in_specs,
        out_specs=out_specs,
        out_shape=out_shape,
        compiler_params=_params(1, 48),
        name="res_ln_mod" if with_h else "res_ln",
    )(*args)


def _gateup_kernel(h_ref, wg_ref, wu_ref, o_ref):
    h = h_ref[...]
    g = jnp.dot(h, wg_ref[...], preferred_element_type=F32)
    u = jnp.dot(h, wu_ref[...], preferred_element_type=F32)
    o_ref[...] = (g * jax.nn.sigmoid(g) * u).astype(o_ref.dtype)


def _swiglu_up(h, wgu):
    tn = 256
    nj = D_FF // tn
    return pl.pallas_call(
        _gateup_kernel,
        grid=(M_ROWS // TM, nj),
        in_specs=[pl.BlockSpec((TM, D_MODEL), lambda i, j: (i, 0)),
                  pl.BlockSpec((D_MODEL, tn), lambda i, j: (0, j)),
                  pl.BlockSpec((D_MODEL, tn), lambda i, j: (0, j + nj))],
        out_specs=pl.BlockSpec((TM, tn), lambda i, j: (i, j)),
        out_shape=jax.ShapeDtypeStruct((M_ROWS, D_FF), BF16),
        compiler_params=_params(2, 48),
        name="swiglu_up",
    )(h, wgu, wgu)


def _mm_kernel(a_ref, b_ref, o_ref):
    o_ref[...] = jnp.dot(a_ref[...], b_ref[...], preferred_element_type=F32).astype(o_ref.dtype)


def _matmul(a, b, tm, tn, name, vmem_mib=48):
    m, k = a.shape
    n = b.shape[1]
    return pl.pallas_call(
        _mm_kernel,
        grid=(m // tm, n // tn),
        in_specs=[pl.BlockSpec((tm, k), lambda i, j: (i, 0)),
                  pl.BlockSpec((k, tn), lambda i, j: (0, j))],
        out_specs=pl.BlockSpec((tm, tn), lambda i, j: (i, j)),
        out_shape=jax.ShapeDtypeStruct((m, n), F32),
        compiler_params=_params(2, vmem_mib),
        name=name,
    )(a, b)


def _mm2_kernel(a1_ref, a2_ref, b_ref, o_ref):
    k1 = a1_ref.shape[1]
    acc = jnp.dot(a1_ref[...], b_ref[:k1, :], preferred_element_type=F32)
    acc += jnp.dot(a2_ref[...], b_ref[k1:, :], preferred_element_type=F32)
    o_ref[...] = acc


def _matmul_cat(a1, a2, b, tm, tn, name):
    m, k1 = a1.shape
    k2 = a2.shape[1]
    n = b.shape[1]
    return pl.pallas_call(
        _mm2_kernel,
        grid=(m // tm, n // tn),
        in_specs=[pl.BlockSpec((tm, k1), lambda i, j: (i, 0)),
                  pl.BlockSpec((tm, k2), lambda i, j: (i, 0)),
                  pl.BlockSpec((k1 + k2, tn), lambda i, j: (0, j))],
        out_specs=pl.BlockSpec((tm, tn), lambda i, j: (i, j)),
        out_shape=jax.ShapeDtypeStruct((m, n), F32),
        compiler_params=_params(2, 48),
        name=name,
    )(a1, a2, b)


def _rope128(blk, cc, ss):
    return blk * cc + pltpu.roll(blk, QK_ROPE, 1) * ss


def _q_kernel(cq_ref, g_ref, w_ref, cc_ref, ss_ref, o_ref, a_scr):
    @pl.when(pl.program_id(1) == 0)
    def _():
        x = cq_ref[...]
        r = lax.rsqrt(jnp.mean(x * x, axis=-1, keepdims=True) + RMS_EPS)
        a_scr[...] = (x * r * g_ref[...]).astype(BF16)

    acc = jnp.dot(a_scr[...], w_ref[...], preferred_element_type=F32)
    o_ref[0, :, :LANES] = (acc[:, :LANES] * SM_SCALE).astype(o_ref.dtype)
    rot = _rope128(acc[:, LANES:], cc_ref[...], ss_ref[...])
    o_ref[0, :, LANES:] = (rot * SM_SCALE).astype(o_ref.dtype)


def _q_proj(proj, q_norm_g, wuq, cc, ss):
    cq_block = (2 * GMLP_WIDTH) // Q_RANK
    return pl.pallas_call(
        _q_kernel,
        grid=(M_ROWS // TM, HEADS),
        in_specs=[pl.BlockSpec((TM, Q_RANK), lambda i, h: (i, cq_block)),
                  pl.BlockSpec((1, Q_RANK), lambda i, h: (0, 0)),
                  pl.BlockSpec((Q_RANK, HEAD_PAD), lambda i, h: (0, h)),
                  pl.BlockSpec((TM, LANES), lambda i, h: (i, 0)),
                  pl.BlockSpec((TM, LANES), lambda i, h: (i, 0))],
        out_specs=pl.BlockSpec((1, TM, HEAD_PAD), lambda i, h: (h, i, 0)),
        out_shape=jax.ShapeDtypeStruct((HEADS, M_ROWS, HEAD_PAD), BF16),
        scratch_shapes=[pltpu.VMEM((TM, Q_RANK), BF16)],
        compiler_params=_params(2, 32),
        name="q_proj",
    )(proj, q_norm_g.reshape(1, Q_RANK), wuq, cc, ss)


def _latkr_kernel(ckv_ref, kr_ref, g_ref, cc_ref, ss_ref, lat_ref, kr_out_ref, kr128_ref):
    x = ckv_ref[...]
    r = lax.rsqrt(jnp.mean(x * x, axis=-1, keepdims=True) + RMS_EPS)
    lat_ref[...] = x * r * g_ref[...]
    rot = _rope128(kr_ref[...], cc_ref[...], ss_ref[...])
    kr_out_ref[...] = rot[:, :QK_ROPE]
    kr128_ref[...] = rot


def _lat_krope(proj, kv_norm_g, cc, ss, row0, rows):
    tr = 512
    off = row0 // tr
    ckv_block = (2 * GMLP_WIDTH + Q_RANK) // KV_RANK
    kr_block = (2 * GMLP_WIDTH + Q_RANK + KV_RANK) // LANES
    return pl.pallas_call(
        _latkr_kernel,
        grid=(rows // tr,),
        in_specs=[pl.BlockSpec((tr, KV_RANK), lambda i: (i + off, ckv_block)),
                  pl.BlockSpec((tr, LANES), lambda i: (i + off, kr_block)),
                  pl.BlockSpec((1, KV_RANK), lambda i: (0, 0)),
                  pl.BlockSpec((tr, LANES), lambda i: (i + off, 0)),
                  pl.BlockSpec((tr, LANES), lambda i: (i + off, 0))],
        out_specs=[pl.BlockSpec((tr, KV_RANK), lambda i: (i, 0)),
                   pl.BlockSpec((tr, QK_ROPE), lambda i: (i, 0)),
                   pl.BlockSpec((tr, LANES), lambda i: (i, 0))],
        out_shape=[jax.ShapeDtypeStruct((rows, KV_RANK), F32),
                   jax.ShapeDtypeStruct((rows, QK_ROPE), F32),
                   jax.ShapeDtypeStruct((rows, LANES), F32)],
        compiler_params=_params(1, 32),
        name="lat_krope",
    )(proj, proj, kv_norm_g.reshape(1, KV_RANK), cc, ss)


def _kv_kernel(lat_ref, kr_ref, w_ref, k_ref, v_ref):
    a = lat_ref[...].astype(BF16)
    kr = kr_ref[...].astype(BF16)
    for h in range(HEADS):
        acc = jnp.dot(a, w_ref[:, h * HEAD_PAD:(h + 1) * HEAD_PAD], preferred_element_type=F32)
        k_ref[h, :, :LANES] = acc[:, :QK_NOPE].astype(BF16)
        k_ref[h, :, LANES:] = kr
        v_ref[h] = acc[:, QK_NOPE:].astype(BF16)


def _kv_proj(lat, kr128, wukv):
    rows = lat.shape[0]
    tr = 512
    return pl.pallas_call(
        _kv_kernel,
        grid=(rows // tr,),
        in_specs=[pl.BlockSpec((tr, KV_RANK), lambda i: (i, 0)),
                  pl.BlockSpec((tr, LANES), lambda i: (i, 0)),
                  pl.BlockSpec((KV_RANK, HEADS * HEAD_PAD), lambda i: (0, 0))],
        out_specs=[pl.BlockSpec((HEADS, tr, HEAD_PAD), lambda i: (0, i, 0)),
                   pl.BlockSpec((HEADS, tr, HEAD_V), lambda i: (0, i, 0))],
        out_shape=[jax.ShapeDtypeStruct((HEADS, rows, HEAD_PAD), BF16),
                   jax.ShapeDtypeStruct((HEADS, rows, HEAD_V), BF16)],
        compiler_params=_params(1, 40),
        name="kv_proj",
    )(lat, kr128, wukv)


def _chunk_mask(q0, k0, tq, tk, k_valid):
    qc = (q0 + lax.broadcasted_iota(jnp.int32, (tq, tk), 0)) >> CHUNK_SHIFT
    kp = k0 + lax.broadcasted_iota(jnp.int32, (tq, tk), 1)
    return (qc >= (kp >> CHUNK_SHIFT)) & (kp < k_valid)


def _attn_prompt_kernel(q_ref, k_ref, v_ref, o_ref, *, tq):
    qi = pl.program_id(1)
    q = q_ref[0]

    def step(j, carry, masked):
        m, l, acc = carry
        k0 = pl.multiple_of(j * tq, tq)
        k = k_ref[0, pl.ds(k0, tq), :]
        v = v_ref[0, pl.ds(k0, tq), :]
        s = lax.dot_general(q, k, (((1,), (1,)), ((), ())), preferred_element_type=F32)
        if masked:
            s = jnp.where(_chunk_mask(qi * tq, k0, tq, tq, SEQ), s, NEG)
        m_new = jnp.maximum(m, jnp.max(s, axis=-1, keepdims=True))
        p = jnp.exp(s - m_new)
        a = jnp.exp(m - m_new)
        l = a * l + jnp.sum(p, axis=-1, keepdims=True)
        acc = a * acc + jnp.dot(p.astype(BF16), v, preferred_element_type=F32)
        return m_new, l, acc

    init = (jnp.full((tq, 1), NEG, F32), jnp.zeros((tq, 1), F32), jnp.zeros((tq, HEAD_V), F32))
    carry = lax.fori_loop(0, qi, functools.partial(step, masked=False), init)
    _, l, acc = step(qi, carry, True)
    o_ref[...] = (acc / l).astype(o_ref.dtype)


def _attn_prompt(q, k, v):
    tq = 512
    return pl.pallas_call(
        functools.partial(_attn_prompt_kernel, tq=tq),
        grid=(HEADS, SEQ // tq),
        in_specs=[pl.BlockSpec((1, tq, HEAD_PAD), lambda h, i: (h, i, 0)),
                  pl.BlockSpec((1, SEQ, HEAD_PAD), lambda h, i: (h, 0, 0)),
                  pl.BlockSpec((1, SEQ, HEAD_V), lambda h, i: (h, 0, 0))],
        out_specs=pl.BlockSpec((tq, HEAD_V), lambda h, i: (i, h)),
        out_shape=jax.ShapeDtypeStruct((SEQ, HEADS * HEAD_V), BF16),
        compiler_params=_params(2, 40),
        name="attn_prompt",
    )(q, k, v)


def _attn_sample_kernel(q_ref, k_ref, v_ref, o_ref):
    s = lax.dot_general(q_ref[0], k_ref[0], (((1,), (1,)), ((), ())), preferred_element_type=F32)
    s = jnp.where(_chunk_mask(PAST_LEN, 0, DEC_SEQ, KV_PAD, PAST_LEN + DEC_SEQ), s, NEG)
    m = jnp.max(s, axis=-1, keepdims=True)
    p = jnp.exp(s - m)
    l = jnp.sum(p, axis=-1, keepdims=True)
    acc = jnp.dot(p.astype(BF16), v_ref[0], preferred_element_type=F32)
    o_ref[...] = (acc / l).astype(o_ref.dtype)


def _attn_sample(q, k, v):
    q_off = SEQ // DEC_SEQ
    return pl.pallas_call(
        _attn_sample_kernel,
        grid=(DEC_BATCH, HEADS),
        in_specs=[pl.BlockSpec((1, DEC_SEQ, HEAD_PAD), lambda b, h: (h, q_off + b, 0)),
                  pl.BlockSpec((1, KV_PAD, HEAD_PAD), lambda b, h: (h, b, 0)),
                  pl.BlockSpec((1, KV_PAD, HEAD_V), lambda b, h: (h, b, 0))],
        out_specs=pl.BlockSpec((DEC_SEQ, HEAD_V), lambda b, h: (b, h)),
        out_shape=jax.ShapeDtypeStruct((S_ROWS, HEADS * HEAD_V), BF16),
        compiler_params=_params(2, 32),
        name="attn_sample",
    )(q, k, v)


def _gmlp_kernel(u_ref, v_ref, g_ref, b_ref, w_ref, bs_ref, a_ref, *rest, chunk, emit_vn):
    v = v_ref[...]
    mu = jnp.mean(v, axis=-1, keepdims=True)
    vc = v - mu
    var = jnp.mean(vc * vc, axis=-1, keepdims=True)
    vn = vc * lax.rsqrt(var + LN_EPS) * g_ref[...] + b_ref[...]
    if emit_vn:
        rest[0][...] = vn
    vnb = vn.astype(BF16)
    n_chunks = v.shape[0] // chunk
    causal = (lax.broadcasted_iota(jnp.int32, (chunk, GMLP_CHUNK), 0)
              >= lax.broadcasted_iota(jnp.int32, (chunk, GMLP_CHUNK), 1))
    for g in range(GMLP_GROUPS):
        cols = slice(g * GROUP_DIM, (g + 1) * GROUP_DIM)
        w = jnp.where(causal, w_ref[g, :chunk, :], 0.0).astype(BF16)
        rhs = jnp.concatenate([vnb[c * chunk:(c + 1) * chunk, cols] for c in range(n_chunks)], axis=1)
        if chunk < GMLP_CHUNK:
            rhs = jnp.concatenate([rhs, jnp.zeros((GMLP_CHUNK - chunk, rhs.shape[1]), BF16)], axis=0)
        mixed = jnp.dot(w, rhs, preferred_element_type=F32) + bs_ref[g, :chunk, :]
        for c in range(n_chunks):
            rows = slice(c * chunk, (c + 1) * chunk)
            gate = mixed[:, c * GROUP_DIM:(c + 1) * GROUP_DIM]
            a_ref[rows, cols] = (u_ref[rows, cols] * gate).astype(a_ref.dtype)


def _gmlp(proj, ln_g, ln_b, w_s, b_s3, row0, rows, tr, chunk, emit_vn):
    off = row0 // tr
    vec = pl.BlockSpec((1, GMLP_WIDTH), lambda i: (0, 0))
    out_specs = [pl.BlockSpec((tr, GMLP_WIDTH), lambda i: (i, 0))]
    out_shape = [jax.ShapeDtypeStruct((rows, GMLP_WIDTH), BF16)]
    if emit_vn:
        out_specs.append(pl.BlockSpec((tr, GMLP_WIDTH), lambda i: (i, 0)))
        out_shape.append(jax.ShapeDtypeStruct((rows, GMLP_WIDTH), F32))
    return pl.pallas_call(
        functools.partial(_gmlp_kernel, chunk=chunk, emit_vn=emit_vn),
        grid=(rows // tr,),
        in_specs=[pl.BlockSpec((tr, GMLP_WIDTH), lambda i: (i + off, 0)),
                  pl.BlockSpec((tr, GMLP_WIDTH), lambda i: (i + off, 1)),
                  vec, vec,
                  pl.BlockSpec((GMLP_GROUPS, GMLP_CHUNK, GMLP_CHUNK), lambda i: (0, 0, 0)),
                  pl.BlockSpec((GMLP_GROUPS, GMLP_CHUNK, 1), lambda i: (0, 0, 0))],
        out_specs=out_specs,
        out_shape=out_shape,
        compiler_params=_params(1, 40),
        name="gmlp_gate_vn" if emit_vn else "gmlp_gate",
    )(proj, proj, ln_g.reshape(1, GMLP_WIDTH), ln_b.reshape(1, GMLP_WIDTH), w_s, b_s3)


def _swap_halves(w):
    half = w.shape[-1] // 2
    return jnp.concatenate([w[..., half:], w[..., :half]], axis=-1)


def _rope_tables():
    pos = jnp.concatenate([jnp.arange(SEQ, dtype=jnp.int32),
                           jnp.tile(PAST_LEN + jnp.arange(DEC_SEQ, dtype=jnp.int32), DEC_BATCH)])
    inv = 1.0 / (ROPE_THETA ** (jnp.arange(0, QK_ROPE, 2, dtype=F32) / QK_ROPE))
    ang = pos.astype(F32)[:, None] * inv[None, :]
    cos, sin = jnp.cos(ang), jnp.sin(ang)
    zero = jnp.zeros((M_ROWS, LANES - QK_ROPE), F32)
    return (jnp.concatenate([cos, cos, zero], axis=1), jnp.concatenate([-sin, sin, zero], axis=1))


def kernel(x_prompt, x_sample, cache_mla_latent, cache_mla_krope, c_prompt, c_sample, w_ada, b_ada, ffn1_w_gate_up, ffn1_w_down, ln1_g, ln1_b, w_in, gmlp_ln_g, gmlp_ln_b, gmlp_w_s, gmlp_b_s, mla_q_norm_g, mla_w_uq, mla_kv_norm_g, mla_w_ukv, w_out, ln2_g, ln2_b, ffn2_w_gate_up, ffn2_w_down, ln3_g, ln3_b):
    wgu1 = ffn1_w_gate_up.astype(BF16)
    wd1 = ffn1_w_down.astype(BF16)
    wgu2 = ffn2_w_gate_up.astype(BF16)
    wd2 = ffn2_w_down.astype(BF16)
    wout = w_out.astype(BF16)
    wukv = mla_w_ukv.astype(BF16)
    win = jnp.concatenate([w_in, _swap_halves(w_in[:, -QK_ROPE:])], axis=1).astype(BF16)
    uq = mla_w_uq.reshape(Q_RANK, HEADS, QK_NOPE + QK_ROPE)
    wuq = jnp.concatenate([uq, _swap_halves(uq[..., QK_NOPE:])], axis=-1).reshape(Q_RANK, HEADS * HEAD_PAD).astype(BF16)
    b_s3 = gmlp_b_s[:, :, None]
    cc, ss = _rope_tables()

    c16 = jnp.concatenate([c_prompt, c_sample, jnp.zeros((16 - 1 - DEC_BATCH, D_MODEL), F32)], axis=0)
    mod = _modulation(c16, w_ada, b_ada).reshape(16, N_MOD, D_MODEL)
    segmod = jnp.concatenate([jnp.broadcast_to(mod[0:1], (SEQ // SEG, N_MOD, D_MODEL)), mod[1:1 + DEC_BATCH]], axis=0)
    segmod = segmod.transpose(1, 0, 2).reshape(N_MOD, M_ROWS // TR, TR // SEG, D_MODEL)
    SH1, SC1, G1, SH2, SC2, G2, SH3, SC3, G3 = range(N_MOD)

    x0 = jnp.concatenate([x_prompt.reshape(SEQ, D_MODEL), x_sample.reshape(S_ROWS, D_MODEL)], axis=0)

    h1 = _modulate(x0, segmod, SC1, SH1)
    y1 = _matmul(_swiglu_up(h1, wgu1), wd1, TM_DOWN, 256, "ffn_down")
    x1, h2 = _res_ln(x0, y1, segmod, G1, 0.5, ln1_g, ln1_b, SC2, SH2)

    proj = _matmul(h2, win, TM, 640, "in_proj")
    a_p = _gmlp(proj, gmlp_ln_g, gmlp_ln_b, gmlp_w_s, b_s3, 0, SEQ, 512, GMLP_CHUNK, False)[0]
    a_s, vn_s = _gmlp(proj, gmlp_ln_g, gmlp_ln_b, gmlp_w_s, b_s3, SEQ, S_ROWS, DEC_SEQ, DEC_SEQ, True)
    q = _q_proj(proj, mla_q_norm_g, wuq, cc, ss)
    lat_p, kr_p, kr128_p = _lat_krope(proj, mla_kv_norm_g, cc, ss, 0, SEQ)
    lat_s, kr_s, kr128_s = _lat_krope(proj, mla_kv_norm_g, cc, ss, SEQ, S_ROWS)

    k_p, v_p = _kv_proj(lat_p, kr128_p, wukv)
    o_p = _attn_prompt(q, k_p, v_p)

    pad_rows = KV_PAD - PAST_LEN - DEC_SEQ
    lat_all = jnp.concatenate([cache_mla_latent, lat_s.reshape(DEC_BATCH, DEC_SEQ, KV_RANK),
                               jnp.zeros((DEC_BATCH, pad_rows, KV_RANK), F32)], axis=1)
    cache_kr128 = jnp.pad(cache_mla_krope, ((0, 0), (0, 0), (0, LANES - QK_ROPE)))
    kr_all = jnp.concatenate([cache_kr128, kr128_s.reshape(DEC_BATCH, DEC_SEQ, LANES),
                              jnp.zeros((DEC_BATCH, pad_rows, LANES), F32)], axis=1)
    k_s, v_s = _kv_proj(lat_all.reshape(DEC_BATCH * KV_PAD, KV_RANK), kr_all.reshape(DEC_BATCH * KV_PAD, LANES), wukv)
    o_s = _attn_sample(q, k_s, v_s)

    a_all = jnp.concatenate([a_p, a_s], axis=0)
    o_all = jnp.concatenate([o_p, o_s], axis=0)
    mix = _matmul_cat(a_all, o_all, wout, TM, 512, "out_proj")
    x2, h3 = _res_ln(x1, mix, segmod, G2, 1.0, ln2_g, ln2_b, SC3, SH3)

    y3 = _matmul(_swiglu_up(h3, wgu2), wd2, TM_DOWN, 256, "ffn_down")
    (y_p,) = _res_ln(x2, y3, segmod, G3, 0.5, ln3_g, ln3_b, row0=0, rows=SEQ)
    (y_s,) = _res_ln(x2, y3, segmod, G3, 0.5, ln3_g, ln3_b, row0=SEQ, rows=S_ROWS)

    return (y_p.reshape(1, SEQ, D_MODEL),
            y_s.reshape(DEC_BATCH, DEC_SEQ, D_MODEL),
            lat_p.reshape(1, SEQ, KV_RANK),
            kr_p.reshape(1, SEQ, QK_ROPE),
            lat_s.reshape(DEC_BATCH, DEC_SEQ, KV_RANK),
            kr_s.reshape(DEC_BATCH, DEC_SEQ, QK_ROPE),
            vn_s.reshape(DEC_BATCH, DEC_SEQ, GMLP_WIDTH))
```

```python
import functools

import jax
import jax.numpy as jnp
from jax import lax
from jax.experimental import pallas as pl
from jax.experimental.pallas import tpu as pltpu

F32 = jnp.float32
BF16 = jnp.bfloat16

D_MODEL = 4096
SEQ = 8192
DEC_BATCH = 8
DEC_SEQ = 64
PAST_LEN = 2048
CHUNK = 64
CHUNK_SHIFT = 6
GMLP_CHUNK = 128
GROUP_DIM = 128
GMLP_WIDTH = D_MODEL // 2
GMLP_GROUPS = GMLP_WIDTH // GROUP_DIM
HEAD_V = 128
QK_NOPE = 128
QK_ROPE = 64
HEADS = (D_MODEL - GMLP_WIDTH) // HEAD_V
KV_RANK = 512
Q_RANK = D_MODEL // 4
D_FF = 256 * ((8 * D_MODEL // 3 + 255) // 256)
N_MOD = 9
ROPE_THETA = 10000.0
LN_EPS = 1e-5
RMS_EPS = 1e-6
ALPHA = 2.0 ** 0.25
Q_SCALE = (QK_NOPE + QK_ROPE) ** -0.5 * 1.4426950408889634
NEG = -1e30

LANES = 128
HEAD_PAD = 2 * LANES
S_ROWS = DEC_BATCH * DEC_SEQ
M_ROWS = SEQ + S_ROWS
SEG = DEC_SEQ
N_SEG = M_ROWS // SEG
TM = M_ROWS // 8
TR = 4 * SEG
IN_COLS = 2 * GMLP_WIDTH + Q_RANK + KV_RANK + 2 * QK_ROPE
KV_PAD = 17 * LANES
MIB = 2 ** 20


def _params(n_axes, vmem_mib):
    return pltpu.CompilerParams(dimension_semantics=("arbitrary",) * n_axes,
                                vmem_limit_bytes=vmem_mib * MIB)


def _mod_kernel(c_ref, w_ref, b_ref, o_ref):
    c = c_ref[...]
    s = (c * jax.nn.sigmoid(c)).astype(BF16)
    o_ref[...] = jnp.dot(s, w_ref[...].astype(BF16), preferred_element_type=F32) + b_ref[...]


def _modulation(c16, w_ada, b_ada):
    tn = 512
    n = w_ada.shape[1]
    return pl.pallas_call(
        _mod_kernel,
        grid=(n // tn,),
        in_specs=[pl.BlockSpec((16, D_MODEL), lambda j: (0, 0)),
                  pl.BlockSpec((D_MODEL, tn), lambda j: (0, j)),
                  pl.BlockSpec((1, tn), lambda j: (0, j))],
        out_specs=pl.BlockSpec((16, tn), lambda j: (0, j)),
        out_shape=jax.ShapeDtypeStruct((16, n), F32),
        compiler_params=_params(1, 40),
        name="adaln_mod",
    )(c16, w_ada, b_ada.reshape(1, n))


def _seg_spec(comp):
    return pl.BlockSpec((1, 1, TR // SEG, D_MODEL), lambda i, comp=comp: (comp, i, 0, 0))


def _modulate_kernel(x_ref, sc_ref, sh_ref, h_ref):
    for s in range(TR // SEG):
        rows = slice(s * SEG, (s + 1) * SEG)
        h = x_ref[rows, :] * (1.0 + sc_ref[0, 0, s:s + 1, :]) + sh_ref[0, 0, s:s + 1, :]
        h_ref[rows, :] = h.astype(h_ref.dtype)


def _modulate(x, segmod, sc, sh):
    row = pl.BlockSpec((TR, D_MODEL), lambda i: (i, 0))
    return pl.pallas_call(
        _modulate_kernel,
        grid=(M_ROWS // TR,),
        in_specs=[row, _seg_spec(sc), _seg_spec(sh)],
        out_specs=row,
        out_shape=jax.ShapeDtypeStruct((M_ROWS, D_MODEL), BF16),
        compiler_params=_params(1, 32),
        name="modulate",
    )(x, segmod, segmod)


def _ln_kernel(x_ref, y_ref, gate_ref, g_ref, b_ref, *rest, gate_scale, with_h):
    if with_h:
        sc_ref, sh_ref, xo_ref, h_ref = rest
    else:
        (xo_ref,) = rest
    for s in range(TR // SEG):
        rows = slice(s * SEG, (s + 1) * SEG)
        z = ALPHA * x_ref[rows, :] + (gate_scale * gate_ref[0, 0, s:s + 1, :]) * y_ref[rows, :]
        mu = jnp.mean(z, axis=-1, keepdims=True)
        zc = z - mu
        var = jnp.mean(zc * zc, axis=-1, keepdims=True)
        xn = zc * lax.rsqrt(var + LN_EPS) * g_ref[...] + b_ref[...]
        xo_ref[rows, :] = xn
        if with_h:
            h = xn * (1.0 + sc_ref[0, 0, s:s + 1, :]) + sh_ref[0, 0, s:s + 1, :]
            h_ref[rows, :] = h.astype(h_ref.dtype)


def _res_ln(x, y, segmod, gate, gate_scale, ln_g, ln_b, sc=None, sh=None, *, row0=0, rows=M_ROWS):
    with_h = sc is not None
    off = row0 // TR
    row_in = pl.BlockSpec((TR, D_MODEL), lambda i: (i + off, 0))
    row_out = pl.BlockSpec((TR, D_MODEL), lambda i: (i, 0))
    vec = pl.BlockSpec((1, D_MODEL), lambda i: (0, 0))

    def seg(comp):
        return pl.BlockSpec((1, 1, TR // SEG, D_MODEL), lambda i, comp=comp: (comp, i + off, 0, 0))

    in_specs = [row_in, row_in, seg(gate), vec, vec]
    args = [x, y, segmod, ln_g.reshape(1, D_MODEL), ln_b.reshape(1, D_MODEL)]
    out_specs = [row_out]
    out_shape = [jax.ShapeDtypeStruct((rows, D_MODEL), F32)]
    if with_h:
        in_specs += [seg(sc), seg(sh)]
        args += [segmod, segmod]
        out_specs.append(row_out)
        out_shape.append(jax.ShapeDtypeStruct((rows, D_MODEL), BF16))
    return pl.pallas_call(
        functools.partial(_ln_kernel, gate_scale=gate_scale, with_h=with_h),
        grid=(rows // TR,),
        in_specs=in_specs,
        out_specs=out_specs,
        out_shape=out_shape,
        compiler_params=_params(1, 48),
        name="res_ln_mod" if with_h else "res_ln",
    )(*args)


def _dot(a, b):
    return lax.dot_general(a, b, (((1,), (0,)), ((), ())), preferred_element_type=F32)


def _gateup_kernel(h_ref, wg_ref, wu_ref, o_ref):
    h = h_ref[...]
    g = _dot(h, wg_ref[...])
    u = _dot(h, wu_ref[...])
    o_ref[...] = (g * jax.nn.sigmoid(g) * u).astype(o_ref.dtype)


def _swiglu_up(h, wgu):
    tn = 256
    nj = D_FF // tn
    return pl.pallas_call(
        _gateup_kernel,
        grid=(M_ROWS // TM, nj),
        in_specs=[pl.BlockSpec((TM, D_MODEL), lambda i, j: (i, 0)),
                  pl.BlockSpec((D_MODEL, tn), lambda i, j: (0, j)),
                  pl.BlockSpec((D_MODEL, tn), lambda i, j: (0, j + nj))],
        out_specs=pl.BlockSpec((TM, tn), lambda i, j: (i, j)),
        out_shape=jax.ShapeDtypeStruct((M_ROWS, D_FF), BF16),
        compiler_params=_params(2, 48),
        name="swiglu_up",
    )(h, wgu, wgu)


def _mm_kernel(a_ref, b_ref, o_ref):
    o_ref[...] = _dot(a_ref[...], b_ref[...]).astype(o_ref.dtype)


def _matmul(a, b, tm, tn, name, vmem_mib=48, a_mode=None):
    m, k = a.shape
    n = b.shape[1]
    return pl.pallas_call(
        _mm_kernel,
        grid=(m // tm, n // tn),
        in_specs=[pl.BlockSpec((tm, k), lambda i, j: (i, 0), pipeline_mode=a_mode),
                  pl.BlockSpec((k, tn), lambda i, j: (0, j))],
        out_specs=pl.BlockSpec((tm, tn), lambda i, j: (i, j)),
        out_shape=jax.ShapeDtypeStruct((m, n), F32),
        compiler_params=_params(2, vmem_mib),
        name=name,
    )(a, b)


def _ffn_down(a, w_down):
    return _matmul(a, w_down, TM, 256, "ffn_down", vmem_mib=56, a_mode=pl.Buffered(1))


def _mm2_kernel(a1_ref, a2_ref, b_ref, o_ref):
    k1 = a1_ref.shape[1]
    acc = jnp.dot(a1_ref[...], b_ref[:k1, :], preferred_element_type=F32)
    acc += jnp.dot(a2_ref[...], b_ref[k1:, :], preferred_element_type=F32)
    o_ref[...] = acc


def _matmul_cat(a1, a2, b, tm, tn, name):
    m, k1 = a1.shape
    k2 = a2.shape[1]
    n = b.shape[1]
    return pl.pallas_call(
        _mm2_kernel,
        grid=(m // tm, n // tn),
        in_specs=[pl.BlockSpec((tm, k1), lambda i, j: (i, 0)),
                  pl.BlockSpec((tm, k2), lambda i, j: (i, 0)),
                  pl.BlockSpec((k1 + k2, tn), lambda i, j: (0, j))],
        out_specs=pl.BlockSpec((tm, tn), lambda i, j: (i, j)),
        out_shape=jax.ShapeDtypeStruct((m, n), F32),
        compiler_params=_params(2, 48),
        name=name,
    )(a1, a2, b)


def _rope128(blk, cc, ss):
    return blk * cc + pltpu.roll(blk, QK_ROPE, 1) * ss


def _q_kernel(cq_ref, g_ref, w_ref, cc_ref, ss_ref, o_ref, a_scr):
    @pl.when(pl.program_id(1) == 0)
    def _():
        x = cq_ref[...]
        r = lax.rsqrt(jnp.mean(x * x, axis=-1, keepdims=True) + RMS_EPS)
        a_scr[...] = (x * r * g_ref[...]).astype(BF16)

    acc = jnp.dot(a_scr[...], w_ref[...], preferred_element_type=F32)
    o_ref[0, :, :LANES] = (acc[:, :LANES] * Q_SCALE).astype(o_ref.dtype)
    rot = _rope128(acc[:, LANES:], cc_ref[...], ss_ref[...])
    o_ref[0, :, LANES:] = (rot * Q_SCALE).astype(o_ref.dtype)


def _q_proj(proj, q_norm_g, wuq, cc, ss):
    cq_block = (2 * GMLP_WIDTH) // Q_RANK
    return pl.pallas_call(
        _q_kernel,
        grid=(M_ROWS // TM, HEADS),
        in_specs=[pl.BlockSpec((TM, Q_RANK), lambda i, h: (i, cq_block)),
                  pl.BlockSpec((1, Q_RANK), lambda i, h: (0, 0)),
                  pl.BlockSpec((Q_RANK, HEAD_PAD), lambda i, h: (0, h)),
                  pl.BlockSpec((TM, LANES), lambda i, h: (i, 0)),
                  pl.BlockSpec((TM, LANES), lambda i, h: (i, 0))],
        out_specs=pl.BlockSpec((1, TM, HEAD_PAD), lambda i, h: (h, i, 0)),
        out_shape=jax.ShapeDtypeStruct((HEADS, M_ROWS, HEAD_PAD), BF16),
        scratch_shapes=[pltpu.VMEM((TM, Q_RANK), BF16)],
        compiler_params=_params(2, 32),
        name="q_proj",
    )(proj, q_norm_g.reshape(1, Q_RANK), wuq, cc, ss)


def _latkr_kernel(ckv_ref, kr_ref, g_ref, cc_ref, ss_ref, lat_ref, kr_out_ref, kr128_ref):
    x = ckv_ref[...]
    r = lax.rsqrt(jnp.mean(x * x, axis=-1, keepdims=True) + RMS_EPS)
    lat_ref[...] = x * r * g_ref[...]
    rot = _rope128(kr_ref[...], cc_ref[...], ss_ref[...])
    kr_out_ref[...] = rot[:, :QK_ROPE]
    kr128_ref[...] = rot


def _lat_krope(proj, kv_norm_g, cc, ss, row0, rows):
    tr = 512
    off = row0 // tr
    ckv_block = (2 * GMLP_WIDTH + Q_RANK) // KV_RANK
    kr_block = (2 * GMLP_WIDTH + Q_RANK + KV_RANK) // LANES
    return pl.pallas_call(
        _latkr_kernel,
        grid=(rows // tr,),
        in_specs=[pl.BlockSpec((tr, KV_RANK), lambda i: (i + off, ckv_block)),
                  pl.BlockSpec((tr, LANES), lambda i: (i + off, kr_block)),
                  pl.BlockSpec((1, KV_RANK), lambda i: (0, 0)),
                  pl.BlockSpec((tr, LANES), lambda i: (i + off, 0)),
                  pl.BlockSpec((tr, LANES), lambda i: (i + off, 0))],
        out_specs=[pl.BlockSpec((tr, KV_RANK), lambda i: (i, 0)),
                   pl.BlockSpec((tr, QK_ROPE), lambda i: (i, 0)),
                   pl.BlockSpec((tr, LANES), lambda i: (i, 0))],
        out_shape=[jax.ShapeDtypeStruct((rows, KV_RANK), F32),
                   jax.ShapeDtypeStruct((rows, QK_ROPE), F32),
                   jax.ShapeDtypeStruct((rows, LANES), F32)],
        compiler_params=_params(1, 32),
        name="lat_krope",
    )(proj, proj, kv_norm_g.reshape(1, KV_RANK), cc, ss)


def _kv_kernel(lat_ref, kr_ref, w_ref, k_ref, v_ref):
    a = lat_ref[...].astype(BF16)
    kr = kr_ref[...].astype(BF16)
    for h in range(HEADS):
        acc = jnp.dot(a, w_ref[:, h * HEAD_PAD:(h + 1) * HEAD_PAD], preferred_element_type=F32)
        k_ref[h, :, :LANES] = acc[:, :QK_NOPE].astype(BF16)
        k_ref[h, :, LANES:] = kr
        v_ref[h] = acc[:, QK_NOPE:].astype(BF16)


def _kv_proj(lat, kr128, wukv):
    rows = lat.shape[0]
    tr = 512
    return pl.pallas_call(
        _kv_kernel,
        grid=(rows // tr,),
        in_specs=[pl.BlockSpec((tr, KV_RANK), lambda i: (i, 0)),
                  pl.BlockSpec((tr, LANES), lambda i: (i, 0)),
                  pl.BlockSpec((KV_RANK, HEADS * HEAD_PAD), lambda i: (0, 0))],
        out_specs=[pl.BlockSpec((HEADS, tr, HEAD_PAD), lambda i: (0, i, 0)),
                   pl.BlockSpec((HEADS, tr, HEAD_V), lambda i: (0, i, 0))],
        out_shape=[jax.ShapeDtypeStruct((HEADS, rows, HEAD_PAD), BF16),
                   jax.ShapeDtypeStruct((HEADS, rows, HEAD_V), BF16)],
        compiler_params=_params(1, 40),
        name="kv_proj",
    )(lat, kr128, wukv)


def _chunk_mask(q0, k0, tq, tk, k_valid):
    qc = (q0 + lax.broadcasted_iota(jnp.int32, (tq, tk), 0)) >> CHUNK_SHIFT
    kp = k0 + lax.broadcasted_iota(jnp.int32, (tq, tk), 1)
    return (qc >= (kp >> CHUNK_SHIFT)) & (kp < k_valid)


def _softmax_step(q, k, v, carry, mask):
    m, l, acc = carry
    s = lax.dot_general(q, k, (((1,), (1,)), ((), ())), preferred_element_type=F32)
    if mask is not None:
        s = jnp.where(mask, s, NEG)
    m_new = jnp.maximum(m, jnp.max(s, axis=-1, keepdims=True))
    p = jnp.exp2(s - m_new)
    a = jnp.exp2(m - m_new)
    l = a * l + jnp.sum(p, axis=-1, keepdims=True)
    acc = a * acc + jnp.dot(p.astype(BF16), v, preferred_element_type=F32)
    return m_new, l, acc


def _attn_prompt_kernel(q_ref, k_ref, v_ref, o_ref, *, tq, rb):
    qi = pl.program_id(1)
    n_rb = tq // rb

    def full_step(j, carries):
        k0 = pl.multiple_of(j * tq, tq)
        k = k_ref[0, pl.ds(k0, tq), :]
        v = v_ref[0, pl.ds(k0, tq), :]
        return tuple(_softmax_step(q_ref[0, r * rb:(r + 1) * rb, :], k, v, carries[r], None)
                     for r in range(n_rb))

    init = tuple((jnp.full((rb, 1), NEG, F32), jnp.zeros((rb, 1), F32), jnp.zeros((rb, HEAD_V), F32))
                 for _ in range(n_rb))
    carries = lax.fori_loop(0, qi, full_step, init)

    q0 = pl.multiple_of(qi * tq, tq)
    for r in range(n_rb):
        nk = (r + 1) * rb
        mask = _chunk_mask(r * rb, 0, rb, nk, nk)
        k = k_ref[0, pl.ds(q0, nk), :]
        v = v_ref[0, pl.ds(q0, nk), :]
        _, l, acc = _softmax_step(q_ref[0, r * rb:(r + 1) * rb, :], k, v, carries[r], mask)
        o_ref[r * rb:(r + 1) * rb, :] = (acc / l).astype(o_ref.dtype)


def _attn_prompt(q, k, v):
    tq = 1024
    return pl.pallas_call(
        functools.partial(_attn_prompt_kernel, tq=tq, rb=256),
        grid=(HEADS, SEQ // tq),
        in_specs=[pl.BlockSpec((1, tq, HEAD_PAD), lambda h, i: (h, i, 0)),
                  pl.BlockSpec((1, SEQ, HEAD_PAD), lambda h, i: (h, 0, 0)),
                  pl.BlockSpec((1, SEQ, HEAD_V), lambda h, i: (h, 0, 0))],
        out_specs=pl.BlockSpec((tq, HEAD_V), lambda h, i: (i, h)),
        out_shape=jax.ShapeDtypeStruct((SEQ, HEADS * HEAD_V), BF16),
        compiler_params=_params(2, 40),
        name="attn_prompt",
    )(q, k, v)


def _attn_sample_kernel(q_ref, k_ref, v_ref, o_ref):
    s = lax.dot_general(q_ref[0], k_ref[0], (((1,), (1,)), ((), ())), preferred_element_type=F32)
    s = jnp.where(_chunk_mask(PAST_LEN, 0, DEC_SEQ, KV_PAD, PAST_LEN + DEC_SEQ), s, NEG)
    m = jnp.max(s, axis=-1, keepdims=True)
    p = jnp.exp2(s - m)
    l = jnp.sum(p, axis=-1, keepdims=True)
    acc = jnp.dot(p.astype(BF16), v_ref[0], preferred_element_type=F32)
    o_ref[...] = (acc / l).astype(o_ref.dtype)


def _attn_sample(q, k, v):
    q_off = SEQ // DEC_SEQ
    return pl.pallas_call(
        _attn_sample_kernel,
        grid=(DEC_BATCH, HEADS),
        in_specs=[pl.BlockSpec((1, DEC_SEQ, HEAD_PAD), lambda b, h: (h, q_off + b, 0)),
                  pl.BlockSpec((1, KV_PAD, HEAD_PAD), lambda b, h: (h, b, 0)),
                  pl.BlockSpec((1, KV_PAD, HEAD_V), lambda b, h: (h, b, 0))],
        out_specs=pl.BlockSpec((DEC_SEQ, HEAD_V), lambda b, h: (b, h)),
        out_shape=jax.ShapeDtypeStruct((S_ROWS, HEADS * HEAD_V), BF16),
        compiler_params=_params(2, 32),
        name="attn_sample",
    )(q, k, v)


def _gmlp_kernel(u_ref, v_ref, g_ref, b_ref, w_ref, bs_ref, a_ref, *rest, chunk, emit_vn):
    v = v_ref[...]
    mu = jnp.mean(v, axis=-1, keepdims=True)
    vc = v - mu
    var = jnp.mean(vc * vc, axis=-1, keepdims=True)
    vn = vc * lax.rsqrt(var + LN_EPS) * g_ref[...] + b_ref[...]
    if emit_vn:
        rest[0][...] = vn
    vnb = vn.astype(BF16)
    n_chunks = v.shape[0] // chunk
    causal = (lax.broadcasted_iota(jnp.int32, (chunk, GMLP_CHUNK), 0)
              >= lax.broadcasted_iota(jnp.int32, (chunk, GMLP_CHUNK), 1))
    for g in range(GMLP_GROUPS):
        cols = slice(g * GROUP_DIM, (g + 1) * GROUP_DIM)
        w = jnp.where(causal, w_ref[g, :chunk, :], 0.0).astype(BF16)
        rhs = jnp.concatenate([vnb[c * chunk:(c + 1) * chunk, cols] for c in range(n_chunks)], axis=1)
        if chunk < GMLP_CHUNK:
            rhs = jnp.concatenate([rhs, jnp.zeros((GMLP_CHUNK - chunk, rhs.shape[1]), BF16)], axis=0)
        mixed = jnp.dot(w, rhs, preferred_element_type=F32) + bs_ref[g, :chunk, :]
        for c in range(n_chunks):
            rows = slice(c * chunk, (c + 1) * chunk)
            gate = mixed[:, c * GROUP_DIM:(c + 1) * GROUP_DIM]
            a_ref[rows, cols] = (u_ref[rows, cols] * gate).astype(a_ref.dtype)


def _gmlp(proj, ln_g, ln_b, w_s, b_s3, row0, rows, tr, chunk, emit_vn):
    off = row0 // tr
    vec = pl.BlockSpec((1, GMLP_WIDTH), lambda i: (0, 0))
    out_specs = [pl.BlockSpec((tr, GMLP_WIDTH), lambda i: (i, 0))]
    out_shape = [jax.ShapeDtypeStruct((rows, GMLP_WIDTH), BF16)]
    if emit_vn:
        out_specs.append(pl.BlockSpec((tr, GMLP_WIDTH), lambda i: (i, 0)))
        out_shape.append(jax.ShapeDtypeStruct((rows, GMLP_WIDTH), F32))
    return pl.pallas_call(
        functools.partial(_gmlp_kernel, chunk=chunk, emit_vn=emit_vn),
        grid=(rows // tr,),
        in_specs=[pl.BlockSpec((tr, GMLP_WIDTH), lambda i: (i + off, 0)),
                  pl.BlockSpec((tr, GMLP_WIDTH), lambda i: (i + off, 1)),
                  vec, vec,
                  pl.BlockSpec((GMLP_GROUPS, GMLP_CHUNK, GMLP_CHUNK), lambda i: (0, 0, 0)),
                  pl.BlockSpec((GMLP_GROUPS, GMLP_CHUNK, 1), lambda i: (0, 0, 0))],
        out_specs=out_specs,
        out_shape=out_shape,
        compiler_params=_params(1, 40),
        name="gmlp_gate_vn" if emit_vn else "gmlp_gate",
    )(proj, proj, ln_g.reshape(1, GMLP_WIDTH), ln_b.reshape(1, GMLP_WIDTH), w_s, b_s3)


def _swap_halves(w):
    half = w.shape[-1] // 2
    return jnp.concatenate([w[..., half:], w[..., :half]], axis=-1)


def _rope_tables():
    pos = jnp.concatenate([jnp.arange(SEQ, dtype=jnp.int32),
                           jnp.tile(PAST_LEN + jnp.arange(DEC_SEQ, dtype=jnp.int32), DEC_BATCH)])
    inv = 1.0 / (ROPE_THETA ** (jnp.arange(0, QK_ROPE, 2, dtype=F32) / QK_ROPE))
    ang = pos.astype(F32)[:, None] * inv[None, :]
    cos, sin = jnp.cos(ang), jnp.sin(ang)
    zero = jnp.zeros((M_ROWS, LANES - QK_ROPE), F32)
    return (jnp.concatenate([cos, cos, zero], axis=1), jnp.concatenate([-sin, sin, zero], axis=1))


def kernel(x_prompt, x_sample, cache_mla_latent, cache_mla_krope, c_prompt, c_sample, w_ada, b_ada, ffn1_w_gate_up, ffn1_w_down, ln1_g, ln1_b, w_in, gmlp_ln_g, gmlp_ln_b, gmlp_w_s, gmlp_b_s, mla_q_norm_g, mla_w_uq, mla_kv_norm_g, mla_w_ukv, w_out, ln2_g, ln2_b, ffn2_w_gate_up, ffn2_w_down, ln3_g, ln3_b):
    wout = w_out.astype(BF16)
    wukv = mla_w_ukv.astype(BF16)
    win = jnp.concatenate([w_in, _swap_halves(w_in[:, -QK_ROPE:])], axis=1).astype(BF16)
    uq = mla_w_uq.reshape(Q_RANK, HEADS, QK_NOPE + QK_ROPE)
    wuq = jnp.concatenate([uq, _swap_halves(uq[..., QK_NOPE:])], axis=-1).reshape(Q_RANK, HEADS * HEAD_PAD).astype(BF16)
    b_s3 = gmlp_b_s[:, :, None]
    cc, ss = _rope_tables()

    c16 = jnp.concatenate([c_prompt, c_sample, jnp.zeros((16 - 1 - DEC_BATCH, D_MODEL), F32)], axis=0)
    mod = _modulation(c16, w_ada, b_ada).reshape(16, N_MOD, D_MODEL)
    segmod = jnp.concatenate([jnp.broadcast_to(mod[0:1], (SEQ // SEG, N_MOD, D_MODEL)), mod[1:1 + DEC_BATCH]], axis=0)
    segmod = segmod.transpose(1, 0, 2).reshape(N_MOD, M_ROWS // TR, TR // SEG, D_MODEL)
    SH1, SC1, G1, SH2, SC2, G2, SH3, SC3, G3 = range(N_MOD)

    x0 = jnp.concatenate([x_prompt.reshape(SEQ, D_MODEL), x_sample.reshape(S_ROWS, D_MODEL)], axis=0)

    h1 = _modulate(x0, segmod, SC1, SH1)
    y1 = _ffn_down(_swiglu_up(h1, ffn1_w_gate_up), ffn1_w_down)
    x1, h2 = _res_ln(x0, y1, segmod, G1, 0.5, ln1_g, ln1_b, SC2, SH2)

    proj = _matmul(h2, win, TM, 640, "in_proj")
    a_p = _gmlp(proj, gmlp_ln_g, gmlp_ln_b, gmlp_w_s, b_s3, 0, SEQ, 512, GMLP_CHUNK, False)[0]
    a_s, vn_s = _gmlp(proj, gmlp_ln_g, gmlp_ln_b, gmlp_w_s, b_s3, SEQ, S_ROWS, DEC_SEQ, DEC_SEQ, True)
    q = _q_proj(proj, mla_q_norm_g, wuq, cc, ss)
    lat_p, kr_p, kr128_p = _lat_krope(proj, mla_kv_norm_g, cc, ss, 0, SEQ)
    lat_s, kr_s, kr128_s = _lat_krope(proj, mla_kv_norm_g, cc, ss, SEQ, S_ROWS)

    k_p, v_p = _kv_proj(lat_p, kr128_p, wukv)
    o_p = _attn_prompt(q, k_p, v_p)

    pad_rows = KV_PAD - PAST_LEN - DEC_SEQ
    lat_all = jnp.concatenate([cache_mla_latent, lat_s.reshape(DEC_BATCH, DEC_SEQ, KV_RANK),
                               jnp.zeros((DEC_BATCH, pad_rows, KV_RANK), F32)], axis=1)
    cache_kr128 = jnp.pad(cache_mla_krope, ((0, 0), (0, 0), (0, LANES - QK_ROPE)))
    kr_all = jnp.concatenate([cache_kr128, kr128_s.reshape(DEC_BATCH, DEC_SEQ, LANES),
                              jnp.zeros((DEC_BATCH, pad_rows, LANES), F32)], axis=1)
    k_s, v_s = _kv_proj(lat_all.reshape(DEC_BATCH * KV_PAD, KV_RANK), kr_all.reshape(DEC_BATCH * KV_PAD, LANES), wukv)
    o_s = _attn_sample(q, k_s, v_s)

    a_all = jnp.concatenate([a_p, a_s], axis=0)
    o_all = jnp.concatenate([o_p, o_s], axis=0)
    mix = _matmul_cat(a_all, o_all, wout, TM, 512, "out_proj")
    x2, h3 = _res_ln(x1, mix, segmod, G2, 1.0, ln2_g, ln2_b, SC3, SH3)

    y3 = _ffn_down(_swiglu_up(h3, ffn2_w_gate_up), ffn2_w_down)
    (y_p,) = _res_ln(x2, y3, segmod, G3, 0.5, ln3_g, ln3_b, row0=0, rows=SEQ)
    (y_s,) = _res_ln(x2, y3, segmod, G3, 0.5, ln3_g, ln3_b, row0=SEQ, rows=S_ROWS)

    return (y_p.reshape(1, SEQ, D_MODEL),
            y_s.reshape(DEC_BATCH, DEC_SEQ, D_MODEL),
            lat_p.reshape(1, SEQ, KV_RANK),
            kr_p.reshape(1, SEQ, QK_ROPE),
            lat_s.reshape(DEC_BATCH, DEC_SEQ, KV_RANK),
            kr_s.reshape(DEC_BATCH, DEC_SEQ, QK_ROPE),
            vn_s.reshape(DEC_BATCH, DEC_SEQ, GMLP_WIDTH))
```

```python
import functools

import jax
import jax.numpy as jnp
from jax import lax
from jax.experimental import pallas as pl
from jax.experimental.pallas import tpu as pltpu

F32 = jnp.float32
BF16 = jnp.bfloat16

D_MODEL = 4096
SEQ = 8192
DEC_BATCH = 8
DEC_SEQ = 64
PAST_LEN = 2048
CHUNK = 64
CHUNK_SHIFT = 6
GMLP_CHUNK = 128
GROUP_DIM = 128
GMLP_WIDTH = D_MODEL // 2
GMLP_GROUPS = GMLP_WIDTH // GROUP_DIM
HEAD_V = 128
QK_NOPE = 128
QK_ROPE = 64
HEADS = (D_MODEL - GMLP_WIDTH) // HEAD_V
KV_RANK = 512
Q_RANK = D_MODEL // 4
D_FF = 256 * ((8 * D_MODEL // 3 + 255) // 256)
N_MOD = 9
ROPE_THETA = 10000.0
LN_EPS = 1e-5
RMS_EPS = 1e-6
ALPHA = 2.0 ** 0.25
Q_SCALE = (QK_NOPE + QK_ROPE) ** -0.5 * 1.4426950408889634
NEG = -1e30

LANES = 128
HEAD_PAD = 2 * LANES
S_ROWS = DEC_BATCH * DEC_SEQ
M_ROWS = SEQ + S_ROWS
SEG = DEC_SEQ
N_SEG = M_ROWS // SEG
TM = M_ROWS // 8
TR = 4 * SEG
IN_COLS = 2 * GMLP_WIDTH + Q_RANK + KV_RANK
IN_TILE = 512
KV_PAD = 17 * LANES
MIB = 2 ** 20


def _params(n_axes, vmem_mib):
    return pltpu.CompilerParams(dimension_semantics=("arbitrary",) * n_axes,
                                vmem_limit_bytes=vmem_mib * MIB)


def _mod_kernel(c_ref, w_ref, b_ref, o_ref):
    c = c_ref[...]
    s = (c * jax.nn.sigmoid(c)).astype(BF16)
    o_ref[...] = jnp.dot(s, w_ref[...].astype(BF16), preferred_element_type=F32) + b_ref[...]


def _modulation(c16, w_ada, b_ada):
    tn = 512
    n = w_ada.shape[1]
    return pl.pallas_call(
        _mod_kernel,
        grid=(n // tn,),
        in_specs=[pl.BlockSpec((16, D_MODEL), lambda j: (0, 0)),
                  pl.BlockSpec((D_MODEL, tn), lambda j: (0, j)),
                  pl.BlockSpec((1, tn), lambda j: (0, j))],
        out_specs=pl.BlockSpec((16, tn), lambda j: (0, j)),
        out_shape=jax.ShapeDtypeStruct((16, n), F32),
        compiler_params=_params(1, 40),
        name="adaln_mod",
    )(c16, w_ada, b_ada.reshape(1, n))


def _seg_spec(comp):
    return pl.BlockSpec((1, 1, TR // SEG, D_MODEL), lambda i, comp=comp: (comp, i, 0, 0))


N_TR_PROMPT = SEQ // TR


def _x_specs(split, off=0):
    if not split:
        return [pl.BlockSpec((TR, D_MODEL), lambda i: (i + off, 0))]
    return [pl.BlockSpec((TR, D_MODEL), lambda i: (jnp.minimum(i, N_TR_PROMPT - 1), 0)),
            pl.BlockSpec((TR, D_MODEL), lambda i: (jnp.maximum(i - N_TR_PROMPT, 0), 0))]


def _on_owner(x_refs, body):
    if len(x_refs) == 1:
        body(x_refs[0])
    else:
        i = pl.program_id(0)
        pl.when(i < N_TR_PROMPT)(lambda: body(x_refs[0]))
        pl.when(i >= N_TR_PROMPT)(lambda: body(x_refs[1]))


def _modulate_kernel(*refs, n_x):
    sc_ref, sh_ref, h_ref = refs[n_x:]

    def body(x_ref):
        for s in range(TR // SEG):
            rows = slice(s * SEG, (s + 1) * SEG)
            h = x_ref[rows, :] * (1.0 + sc_ref[0, 0, s:s + 1, :]) + sh_ref[0, 0, s:s + 1, :]
            h_ref[rows, :] = h.astype(h_ref.dtype)

    _on_owner(refs[:n_x], body)


def _modulate(xs, segmod, sc, sh):
    return pl.pallas_call(
        functools.partial(_modulate_kernel, n_x=len(xs)),
        grid=(M_ROWS // TR,),
        in_specs=_x_specs(len(xs) == 2) + [_seg_spec(sc), _seg_spec(sh)],
        out_specs=pl.BlockSpec((TR, D_MODEL), lambda i: (i, 0)),
        out_shape=jax.ShapeDtypeStruct((M_ROWS, D_MODEL), BF16),
        compiler_params=_params(1, 32),
        name="modulate",
    )(*xs, segmod, segmod)


def _ln_kernel(*refs, n_x, gate_scale, with_h):
    y_ref, gate_ref, g_ref, b_ref = refs[n_x:n_x + 4]
    if with_h:
        sc_ref, sh_ref, xo_ref, h_ref = refs[n_x + 4:]
    else:
        (xo_ref,) = refs[n_x + 4:]

    def body(x_ref):
        for s in range(TR // SEG):
            rows = slice(s * SEG, (s + 1) * SEG)
            z = ALPHA * x_ref[rows, :] + (gate_scale * gate_ref[0, 0, s:s + 1, :]) * y_ref[rows, :]
            mu = jnp.mean(z, axis=-1, keepdims=True)
            zc = z - mu
            var = jnp.mean(zc * zc, axis=-1, keepdims=True)
            xn = zc * lax.rsqrt(var + LN_EPS) * g_ref[...] + b_ref[...]
            xo_ref[rows, :] = xn
            if with_h:
                h = xn * (1.0 + sc_ref[0, 0, s:s + 1, :]) + sh_ref[0, 0, s:s + 1, :]
                h_ref[rows, :] = h.astype(h_ref.dtype)

    _on_owner(refs[:n_x], body)


def _res_ln(xs, y, segmod, gate, gate_scale, ln_g, ln_b, sc=None, sh=None, *, row0=0, rows=M_ROWS):
    with_h = sc is not None
    off = row0 // TR
    row_in = pl.BlockSpec((TR, D_MODEL), lambda i: (i + off, 0))
    row_out = pl.BlockSpec((TR, D_MODEL), lambda i: (i, 0))
    vec = pl.BlockSpec((1, D_MODEL), lambda i: (0, 0))

    def seg(comp):
        return pl.BlockSpec((1, 1, TR // SEG, D_MODEL), lambda i, comp=comp: (comp, i + off, 0, 0))

    in_specs = _x_specs(len(xs) == 2, off) + [row_in, seg(gate), vec, vec]
    args = [*xs, y, segmod, ln_g.reshape(1, D_MODEL), ln_b.reshape(1, D_MODEL)]
    out_specs = [row_out]
    out_shape = [jax.ShapeDtypeStruct((rows, D_MODEL), F32)]
    if with_h:
        in_specs += [seg(sc), seg(sh)]
        args += [segmod, segmod]
        out_specs.append(row_out)
        out_shape.append(jax.ShapeDtypeStruct((rows, D_MODEL), BF16))
    return pl.pallas_call(
        functools.partial(_ln_kernel, n_x=len(xs), gate_scale=gate_scale, with_h=with_h),
        grid=(rows // TR,),
        in_specs=in_specs,
        out_specs=out_specs,
        out_shape=out_shape,
        compiler_params=_params(1, 48),
        name="res_ln_mod" if with_h else "res_ln",
    )(*args)


def _dot(a, b):
    return lax.dot_general(a, b, (((1,), (0,)), ((), ())), preferred_element_type=F32)


def _gateup_kernel(h_ref, wg_ref, wu_ref, o_ref):
    h = h_ref[...]
    g = _dot(h, wg_ref[...])
    u = _dot(h, wu_ref[...])
    o_ref[...] = (g * jax.nn.sigmoid(g) * u).astype(o_ref.dtype)


def _swiglu_up(h, wgu):
    tn = 256
    nj = D_FF // tn
    return pl.pallas_call(
        _gateup_kernel,
        grid=(M_ROWS // TM, nj),
        in_specs=[pl.BlockSpec((TM, D_MODEL), lambda i, j: (i, 0)),
                  pl.BlockSpec((D_MODEL, tn), lambda i, j: (0, j)),
                  pl.BlockSpec((D_MODEL, tn), lambda i, j: (0, j + nj))],
        out_specs=pl.BlockSpec((TM, tn), lambda i, j: (i, j)),
        out_shape=jax.ShapeDtypeStruct((M_ROWS, D_FF), BF16),
        compiler_params=_params(2, 48),
        name="swiglu_up",
    )(h, wgu, wgu)


def _mm_kernel(a_ref, b_ref, o_ref):
    o_ref[...] = _dot(a_ref[...], b_ref[...]).astype(o_ref.dtype)


def _matmul(a, b, tm, tn, name, vmem_mib=48, a_mode=None, n_cols=None):
    m, k = a.shape
    n = b.shape[1] if n_cols is None else n_cols
    return pl.pallas_call(
        _mm_kernel,
        grid=(m // tm, n // tn),
        in_specs=[pl.BlockSpec((tm, k), lambda i, j: (i, 0), pipeline_mode=a_mode),
                  pl.BlockSpec((k, tn), lambda i, j: (0, j))],
        out_specs=pl.BlockSpec((tm, tn), lambda i, j: (i, j)),
        out_shape=jax.ShapeDtypeStruct((m, n), F32),
        compiler_params=_params(2, vmem_mib),
        name=name,
    )(a, b)


def _ffn_down(a, w_down):
    return _matmul(a, w_down, TM, 256, "ffn_down", vmem_mib=56, a_mode=pl.Buffered(1))


def _mm2_kernel(a1_ref, a2_ref, b_ref, o_ref):
    k1 = a1_ref.shape[1]
    acc = _dot(a1_ref[...], b_ref[:k1, :])
    acc += _dot(a2_ref[...], b_ref[k1:, :])
    o_ref[...] = acc


def _matmul_cat(a1, a2, b, tm, tn, name):
    m, k1 = a1.shape
    k2 = a2.shape[1]
    n = b.shape[1]
    return pl.pallas_call(
        _mm2_kernel,
        grid=(m // tm, n // tn),
        in_specs=[pl.BlockSpec((tm, k1), lambda i, j: (i, 0)),
                  pl.BlockSpec((tm, k2), lambda i, j: (i, 0)),
                  pl.BlockSpec((k1 + k2, tn), lambda i, j: (0, j))],
        out_specs=pl.BlockSpec((tm, tn), lambda i, j: (i, j)),
        out_shape=jax.ShapeDtypeStruct((m, n), F32),
        compiler_params=_params(2, 48),
        name=name,
    )(a1, a2, b)


def _rope128(blk, cc, ss):
    return blk * cc + pltpu.roll(blk, QK_ROPE, 1) * ss


def _q_kernel(cq_ref, g_ref, w_ref, cc_ref, ss_ref, o_ref):
    x = cq_ref[...]
    r = lax.rsqrt(jnp.mean(x * x, axis=-1, keepdims=True) + RMS_EPS)
    a = (x * r * g_ref[...]).astype(BF16)
    cc = cc_ref[...]
    ss = ss_ref[...]
    for h in range(HEADS):
        acc = _dot(a, w_ref[:, h * HEAD_PAD:(h + 1) * HEAD_PAD])
        o_ref[h, :, :LANES] = (acc[:, :LANES] * Q_SCALE).astype(o_ref.dtype)
        o_ref[h, :, LANES:] = (_rope128(acc[:, LANES:], cc, ss) * Q_SCALE).astype(o_ref.dtype)


def _q_proj(proj, q_norm_g, wuq, cc, ss):
    tm = TM // 2
    cq_block = (2 * GMLP_WIDTH) // Q_RANK
    return pl.pallas_call(
        _q_kernel,
        grid=(M_ROWS // tm,),
        in_specs=[pl.BlockSpec((tm, Q_RANK), lambda i: (i, cq_block)),
                  pl.BlockSpec((1, Q_RANK), lambda i: (0, 0)),
                  pl.BlockSpec((Q_RANK, HEADS * HEAD_PAD), lambda i: (0, 0)),
                  pl.BlockSpec((tm, LANES), lambda i: (i, 0)),
                  pl.BlockSpec((tm, LANES), lambda i: (i, 0))],
        out_specs=pl.BlockSpec((HEADS, tm, HEAD_PAD), lambda i: (0, i, 0)),
        out_shape=jax.ShapeDtypeStruct((HEADS, M_ROWS, HEAD_PAD), BF16),
        compiler_params=_params(1, 40),
        name="q_proj",
    )(proj, q_norm_g.reshape(1, Q_RANK), wuq, cc, ss)


def _latkr_kernel(ckv_ref, h_ref, wkr_ref, g_ref, cc_ref, ss_ref, lat_ref, kr_out_ref, kr128_ref):
    x = ckv_ref[...]
    r = lax.rsqrt(jnp.mean(x * x, axis=-1, keepdims=True) + RMS_EPS)
    lat_ref[...] = x * r * g_ref[...]
    rot = _rope128(_dot(h_ref[...], wkr_ref[...]), cc_ref[...], ss_ref[...])
    kr_out_ref[...] = rot[:, :QK_ROPE]
    kr128_ref[...] = rot


def _lat_krope(proj, h, wkr, kv_norm_g, cc, ss, row0, rows):
    tr = 512
    off = row0 // tr
    ckv_block = (2 * GMLP_WIDTH + Q_RANK) // KV_RANK
    return pl.pallas_call(
        _latkr_kernel,
        grid=(rows // tr,),
        in_specs=[pl.BlockSpec((tr, KV_RANK), lambda i: (i + off, ckv_block)),
                  pl.BlockSpec((tr, D_MODEL), lambda i: (i + off, 0)),
                  pl.BlockSpec((D_MODEL, LANES), lambda i: (0, 0)),
                  pl.BlockSpec((1, KV_RANK), lambda i: (0, 0)),
                  pl.BlockSpec((tr, LANES), lambda i: (i + off, 0)),
                  pl.BlockSpec((tr, LANES), lambda i: (i + off, 0))],
        out_specs=[pl.BlockSpec((tr, KV_RANK), lambda i: (i, 0)),
                   pl.BlockSpec((tr, QK_ROPE), lambda i: (i, 0)),
                   pl.BlockSpec((tr, LANES), lambda i: (i, 0))],
        out_shape=[jax.ShapeDtypeStruct((rows, KV_RANK), F32),
                   jax.ShapeDtypeStruct((rows, QK_ROPE), F32),
                   jax.ShapeDtypeStruct((rows, LANES), F32)],
        compiler_params=_params(1, 32),
        name="lat_krope",
    )(proj, h, wkr, kv_norm_g.reshape(1, KV_RANK), cc, ss)


def _kv_kernel(lat_ref, kr_ref, w_ref, k_ref, v_ref):
    a = lat_ref[...].astype(BF16)
    kr = kr_ref[...].astype(BF16)
    for h in range(HEADS):
        acc = jnp.dot(a, w_ref[:, h * HEAD_PAD:(h + 1) * HEAD_PAD], preferred_element_type=F32)
        k_ref[h, :, :LANES] = acc[:, :QK_NOPE].astype(BF16)
        k_ref[h, :, LANES:] = kr
        v_ref[h] = acc[:, QK_NOPE:].astype(BF16)


def _kv_proj(lat, kr128, wukv):
    rows = lat.shape[0]
    tr = 512
    return pl.pallas_call(
        _kv_kernel,
        grid=(rows // tr,),
        in_specs=[pl.BlockSpec((tr, KV_RANK), lambda i: (i, 0)),
                  pl.BlockSpec((tr, LANES), lambda i: (i, 0)),
                  pl.BlockSpec((KV_RANK, HEADS * HEAD_PAD), lambda i: (0, 0))],
        out_specs=[pl.BlockSpec((HEADS, tr, HEAD_PAD), lambda i: (0, i, 0)),
                   pl.BlockSpec((HEADS, tr, HEAD_V), lambda i: (0, i, 0))],
        out_shape=[jax.ShapeDtypeStruct((HEADS, rows, HEAD_PAD), BF16),
                   jax.ShapeDtypeStruct((HEADS, rows, HEAD_V), BF16)],
        compiler_params=_params(1, 40),
        name="kv_proj",
    )(lat, kr128, wukv)


def _kv_t_kernel(lat_ref, kr_ref, w_ref, wvt_ref, k_ref, vt_ref):
    a = lat_ref[...].astype(BF16)
    kr = kr_ref[...].astype(BF16)
    for h in range(HEADS):
        kn = _dot(a, w_ref[:, h * HEAD_PAD:h * HEAD_PAD + QK_NOPE])
        k_ref[h, :, :LANES] = kn.astype(BF16)
        k_ref[h, :, LANES:] = kr
        vt = lax.dot_general(wvt_ref[h], a, (((1,), (1,)), ((), ())), preferred_element_type=F32)
        vt_ref[h, 0] = vt.astype(BF16)


def _kv_proj_t(lat, kr128, wukv, wvt, tr):
    rows = lat.shape[0]
    return pl.pallas_call(
        _kv_t_kernel,
        grid=(rows // tr,),
        in_specs=[pl.BlockSpec((tr, KV_RANK), lambda i: (i, 0)),
                  pl.BlockSpec((tr, LANES), lambda i: (i, 0)),
                  pl.BlockSpec((KV_RANK, HEADS * HEAD_PAD), lambda i: (0, 0)),
                  pl.BlockSpec((HEADS, HEAD_V, KV_RANK), lambda i: (0, 0, 0))],
        out_specs=[pl.BlockSpec((HEADS, tr, HEAD_PAD), lambda i: (0, i, 0)),
                   pl.BlockSpec((HEADS, 1, HEAD_V, tr), lambda i: (0, i, 0, 0))],
        out_shape=[jax.ShapeDtypeStruct((HEADS, rows, HEAD_PAD), BF16),
                   jax.ShapeDtypeStruct((HEADS, rows // tr, HEAD_V, tr), BF16)],
        compiler_params=_params(1, 48),
        name="kv_proj_t",
    )(lat, kr128, wukv, wvt)


def _chunk_mask(q0, k0, tq, tk, k_valid):
    qc = (q0 + lax.broadcasted_iota(jnp.int32, (tq, tk), 0)) >> CHUNK_SHIFT
    kp = k0 + lax.broadcasted_iota(jnp.int32, (tq, tk), 1)
    return (qc >= (kp >> CHUNK_SHIFT)) & (kp < k_valid)


ATTN_TILE = 1024


def _softmax_pv_t(s, vt, carry):
    m, l, acc = carry
    m_new = jnp.maximum(m, jnp.max(s, axis=0, keepdims=True))
    p = jnp.exp2(s - m_new)
    a = jnp.exp2(m - m_new)
    l = a * l + jnp.sum(p, axis=0, keepdims=True)
    acc = a * acc + jnp.dot(vt, p.astype(BF16), preferred_element_type=F32)
    return m_new, l, acc


def _chunk_mask_t(q_rel0, nk, nq):
    kc = lax.broadcasted_iota(jnp.int32, (nk, nq), 0) >> CHUNK_SHIFT
    qc = (q_rel0 + lax.broadcasted_iota(jnp.int32, (nk, nq), 1)) >> CHUNK_SHIFT
    return qc >= kc


def _attn_prompt_kernel(q_ref, k_ref, vt_ref, o_ref, s_even, s_odd, *, rb):
    qi = pl.program_id(1)
    n_rb = ATTN_TILE // rb

    def scores(j, s_ref):
        k = k_ref[0, pl.ds(pl.multiple_of(j * ATTN_TILE, ATTN_TILE), ATTN_TILE), :]
        s_ref[...] = lax.dot_general(k, q_ref[0], (((1,), (1,)), ((), ())), preferred_element_type=F32)

    def make_step(s_cur, s_next):
        def step(j, stats):
            scores(j + 1, s_next)
            vt = vt_ref[0, j]
            return tuple(_softmax_pv_t(s_cur[:, r * rb:(r + 1) * rb], vt, stats[r]) for r in range(n_rb))
        return step

    step_even, step_odd = make_step(s_even, s_odd), make_step(s_odd, s_even)

    def finish(s_ref, stats):
        for r in range(n_rb):
            nk = (r + 1) * rb
            s_r = jnp.where(_chunk_mask_t(r * rb, nk, rb), s_ref[:nk, r * rb:(r + 1) * rb], NEG)
            _, l, acc = _softmax_pv_t(s_r, vt_ref[0, qi, :, :nk], stats[r])
            o_ref[r * rb:(r + 1) * rb, :] = (acc / l).T.astype(o_ref.dtype)

    init = tuple((jnp.full((1, rb), NEG, F32), jnp.zeros((1, rb), F32), jnp.zeros((HEAD_V, rb), F32))
                 for _ in range(n_rb))
    scores(0, s_even)
    stats = lax.fori_loop(
        0, qi, lambda j, st: lax.cond((j & 1) == 0, lambda: step_even(j, st), lambda: step_odd(j, st)), init)
    lax.cond((qi & 1) == 0, lambda: finish(s_even, stats), lambda: finish(s_odd, stats))


def _attn_prompt(q, k, vt):
    return pl.pallas_call(
        functools.partial(_attn_prompt_kernel, rb=512),
        grid=(HEADS, SEQ // ATTN_TILE),
        in_specs=[pl.BlockSpec((1, ATTN_TILE, HEAD_PAD), lambda h, i: (h, i, 0)),
                  pl.BlockSpec((1, SEQ, HEAD_PAD), lambda h, i: (h, 0, 0)),
                  pl.BlockSpec((1, SEQ // ATTN_TILE, HEAD_V, ATTN_TILE), lambda h, i: (h, 0, 0, 0))],
        out_specs=pl.BlockSpec((ATTN_TILE, HEAD_V), lambda h, i: (i, h)),
        out_shape=jax.ShapeDtypeStruct((M_ROWS, HEADS * HEAD_V), BF16),
        scratch_shapes=[pltpu.VMEM((ATTN_TILE, ATTN_TILE), F32), pltpu.VMEM((ATTN_TILE, ATTN_TILE), F32)],
        compiler_params=_params(2, 48),
        name="attn_prompt",
    )(q, k, vt)


def _attn_sample_kernel(q_ref, k_ref, v_ref, o_prev_ref, o_ref):
    del o_prev_ref
    s = lax.dot_general(q_ref[0], k_ref[0], (((1,), (1,)), ((), ())), preferred_element_type=F32)
    s = jnp.where(_chunk_mask(PAST_LEN, 0, DEC_SEQ, KV_PAD, PAST_LEN + DEC_SEQ), s, NEG)
    m = jnp.max(s, axis=-1, keepdims=True)
    p = jnp.exp2(s - m)
    l = jnp.sum(p, axis=-1, keepdims=True)
    acc = jnp.dot(p.astype(BF16), v_ref[0], preferred_element_type=F32)
    o_ref[...] = (acc / l).astype(o_ref.dtype)


def _attn_sample(q, k, v, o_prev):
    q_off = SEQ // DEC_SEQ
    return pl.pallas_call(
        _attn_sample_kernel,
        grid=(DEC_BATCH, HEADS),
        in_specs=[pl.BlockSpec((1, DEC_SEQ, HEAD_PAD), lambda b, h: (h, q_off + b, 0)),
                  pl.BlockSpec((1, KV_PAD, HEAD_PAD), lambda b, h: (h, b, 0)),
                  pl.BlockSpec((1, KV_PAD, HEAD_V), lambda b, h: (h, b, 0)),
                  pl.BlockSpec(memory_space=pl.ANY)],
        out_specs=pl.BlockSpec((DEC_SEQ, HEAD_V), lambda b, h: (q_off + b, h)),
        out_shape=jax.ShapeDtypeStruct((M_ROWS, HEADS * HEAD_V), BF16),
        input_output_aliases={3: 0},
        compiler_params=_params(2, 32),
        name="attn_sample",
    )(q, k, v, o_prev)


def _gmlp_kernel(u_ref, v_ref, g_ref, b_ref, w_ref, bs_ref, *rest, chunk, emit_vn):
    if emit_vn:
        _, a_ref, vn_ref = rest
    else:
        (a_ref,) = rest
    v = v_ref[...]
    mu = jnp.mean(v, axis=-1, keepdims=True)
    vc = v - mu
    var = jnp.mean(vc * vc, axis=-1, keepdims=True)
    vn = vc * lax.rsqrt(var + LN_EPS) * g_ref[...] + b_ref[...]
    if emit_vn:
        vn_ref[...] = vn
    vnb = vn.astype(BF16)
    n_chunks = v.shape[0] // chunk
    causal = (lax.broadcasted_iota(jnp.int32, (chunk, GMLP_CHUNK), 0)
              >= lax.broadcasted_iota(jnp.int32, (chunk, GMLP_CHUNK), 1))
    for g in range(GMLP_GROUPS):
        cols = slice(g * GROUP_DIM, (g + 1) * GROUP_DIM)
        w = jnp.where(causal, w_ref[g, :chunk, :], 0.0).astype(BF16)
        rhs = jnp.concatenate([vnb[c * chunk:(c + 1) * chunk, cols] for c in range(n_chunks)], axis=1)
        if chunk < GMLP_CHUNK:
            rhs = jnp.concatenate([rhs, jnp.zeros((GMLP_CHUNK - chunk, rhs.shape[1]), BF16)], axis=0)
        mixed = jnp.dot(w, rhs, preferred_element_type=F32) + bs_ref[g, :chunk, :]
        for c in range(n_chunks):
            rows = slice(c * chunk, (c + 1) * chunk)
            gate = mixed[:, c * GROUP_DIM:(c + 1) * GROUP_DIM]
            a_ref[rows, cols] = (u_ref[rows, cols] * gate).astype(a_ref.dtype)


def _gmlp(proj, ln_g, ln_b, w_s, b_s3, row0, rows, tr, chunk, a_prev=None):
    emit_vn = a_prev is not None
    off = row0 // tr
    vec = pl.BlockSpec((1, GMLP_WIDTH), lambda i: (0, 0))
    in_specs = [pl.BlockSpec((tr, GMLP_WIDTH), lambda i: (i + off, 0)),
                pl.BlockSpec((tr, GMLP_WIDTH), lambda i: (i + off, 1)),
                vec, vec,
                pl.BlockSpec((GMLP_GROUPS, GMLP_CHUNK, GMLP_CHUNK), lambda i: (0, 0, 0)),
                pl.BlockSpec((GMLP_GROUPS, GMLP_CHUNK, 1), lambda i: (0, 0, 0))]
    args = [proj, proj, ln_g.reshape(1, GMLP_WIDTH), ln_b.reshape(1, GMLP_WIDTH), w_s, b_s3]
    out_specs = [pl.BlockSpec((tr, GMLP_WIDTH), lambda i: (i + off, 0))]
    out_shape = [jax.ShapeDtypeStruct((M_ROWS, GMLP_WIDTH), BF16)]
    aliases = {}
    if emit_vn:
        in_specs.append(pl.BlockSpec(memory_space=pl.ANY))
        args.append(a_prev)
        aliases = {len(args) - 1: 0}
        out_specs.append(pl.BlockSpec((tr, GMLP_WIDTH), lambda i: (i, 0)))
        out_shape.append(jax.ShapeDtypeStruct((rows, GMLP_WIDTH), F32))
    return pl.pallas_call(
        functools.partial(_gmlp_kernel, chunk=chunk, emit_vn=emit_vn),
        grid=(rows // tr,),
        in_specs=in_specs,
        out_specs=out_specs,
        out_shape=out_shape,
        input_output_aliases=aliases,
        compiler_params=_params(1, 40),
        name="gmlp_gate_vn" if emit_vn else "gmlp_gate",
    )(*args)


def _swap_halves(w):
    half = w.shape[-1] // 2
    return jnp.concatenate([w[..., half:], w[..., :half]], axis=-1)


def _rope_tables():
    pos = jnp.concatenate([jnp.arange(SEQ, dtype=jnp.int32),
                           jnp.tile(PAST_LEN + jnp.arange(DEC_SEQ, dtype=jnp.int32), DEC_BATCH)])
    inv = 1.0 / (ROPE_THETA ** (jnp.arange(0, QK_ROPE, 2, dtype=F32) / QK_ROPE))
    ang = pos.astype(F32)[:, None] * inv[None, :]
    cos, sin = jnp.cos(ang), jnp.sin(ang)
    zero = jnp.zeros((M_ROWS, LANES - QK_ROPE), F32)
    return (jnp.concatenate([cos, cos, zero], axis=1), jnp.concatenate([-sin, sin, zero], axis=1))


def kernel(x_prompt, x_sample, cache_mla_latent, cache_mla_krope, c_prompt, c_sample, w_ada, b_ada, ffn1_w_gate_up, ffn1_w_down, ln1_g, ln1_b, w_in, gmlp_ln_g, gmlp_ln_b, gmlp_w_s, gmlp_b_s, mla_q_norm_g, mla_w_uq, mla_kv_norm_g, mla_w_ukv, w_out, ln2_g, ln2_b, ffn2_w_gate_up, ffn2_w_down, ln3_g, ln3_b):
    wukv = mla_w_ukv.astype(BF16)
    wvt = mla_w_ukv.reshape(KV_RANK, HEADS, HEAD_PAD)[:, :, QK_NOPE:].transpose(1, 2, 0).astype(BF16)
    w_kr = w_in[:, IN_COLS:]
    wkr = jnp.concatenate([w_kr, _swap_halves(w_kr)], axis=1)
    uq = mla_w_uq.reshape(Q_RANK, HEADS, QK_NOPE + QK_ROPE)
    wuq = jnp.concatenate([uq, _swap_halves(uq[..., QK_NOPE:])], axis=-1).reshape(Q_RANK, HEADS * HEAD_PAD).astype(BF16)
    b_s3 = gmlp_b_s[:, :, None]
    cc, ss = _rope_tables()

    c16 = jnp.concatenate([c_prompt, c_sample, jnp.zeros((16 - 1 - DEC_BATCH, D_MODEL), F32)], axis=0)
    mod = _modulation(c16, w_ada, b_ada).reshape(16, N_MOD, D_MODEL)
    segmod = jnp.concatenate([jnp.broadcast_to(mod[0:1], (SEQ // SEG, N_MOD, D_MODEL)), mod[1:1 + DEC_BATCH]], axis=0)
    segmod = segmod.transpose(1, 0, 2).reshape(N_MOD, M_ROWS // TR, TR // SEG, D_MODEL)
    SH1, SC1, G1, SH2, SC2, G2, SH3, SC3, G3 = range(N_MOD)

    x0 = (x_prompt.reshape(SEQ, D_MODEL), x_sample.reshape(S_ROWS, D_MODEL))

    h1 = _modulate(x0, segmod, SC1, SH1)
    y1 = _ffn_down(_swiglu_up(h1, ffn1_w_gate_up), ffn1_w_down)
    x1, h2 = _res_ln(x0, y1, segmod, G1, 0.5, ln1_g, ln1_b, SC2, SH2)

    proj = _matmul(h2, w_in, TM, IN_TILE, "in_proj", n_cols=IN_COLS)
    (a_mix,) = _gmlp(proj, gmlp_ln_g, gmlp_ln_b, gmlp_w_s, b_s3, 0, SEQ, 512, GMLP_CHUNK)
    a_mix, vn_s = _gmlp(proj, gmlp_ln_g, gmlp_ln_b, gmlp_w_s, b_s3, SEQ, S_ROWS, DEC_SEQ, DEC_SEQ, a_prev=a_mix)
    q = _q_proj(proj, mla_q_norm_g, wuq, cc, ss)
    lat_p, kr_p, kr128_p = _lat_krope(proj, h2, wkr, mla_kv_norm_g, cc, ss, 0, SEQ)
    lat_s, kr_s, kr128_s = _lat_krope(proj, h2, wkr, mla_kv_norm_g, cc, ss, SEQ, S_ROWS)

    k_p, vt_p = _kv_proj_t(lat_p, kr128_p, wukv, wvt, ATTN_TILE)
    o_mix = _attn_prompt(q, k_p, vt_p)

    pad_rows = KV_PAD - PAST_LEN - DEC_SEQ
    lat_all = jnp.concatenate([cache_mla_latent, lat_s.reshape(DEC_BATCH, DEC_SEQ, KV_RANK),
                               jnp.zeros((DEC_BATCH, pad_rows, KV_RANK), F32)], axis=1)
    cache_kr128 = jnp.pad(cache_mla_krope, ((0, 0), (0, 0), (0, LANES - QK_ROPE)))
    kr_all = jnp.concatenate([cache_kr128, kr128_s.reshape(DEC_BATCH, DEC_SEQ, LANES),
                              jnp.zeros((DEC_BATCH, pad_rows, LANES), F32)], axis=1)
    k_s, v_s = _kv_proj(lat_all.reshape(DEC_BATCH * KV_PAD, KV_RANK), kr_all.reshape(DEC_BATCH * KV_PAD, LANES), wukv)
    o_mix = _attn_sample(q, k_s, v_s, o_mix)

    mix = _matmul_cat(a_mix, o_mix, w_out, TM, 512, "out_proj")
    x2, h3 = _res_ln((x1,), mix, segmod, G2, 1.0, ln2_g, ln2_b, SC3, SH3)

    y3 = _ffn_down(_swiglu_up(h3, ffn2_w_gate_up), ffn2_w_down)
    (y_p,) = _res_ln((x2,), y3, segmod, G3, 0.5, ln3_g, ln3_b, row0=0, rows=SEQ)
    (y_s,) = _res_ln((x2,), y3, segmod, G3, 0.5, ln3_g, ln3_b, row0=SEQ, rows=S_ROWS)

    return (y_p.reshape(1, SEQ, D_MODEL),
            y_s.reshape(DEC_BATCH, DEC_SEQ, D_MODEL),
            lat_p.reshape(1, SEQ, KV_RANK),
            kr_p.reshape(1, SEQ, QK_ROPE),
            lat_s.reshape(DEC_BATCH, DEC_SEQ, KV_RANK),
            kr_s.reshape(DEC_BATCH, DEC_SEQ, QK_ROPE),
            vn_s.reshape(DEC_BATCH, DEC_SEQ, GMLP_WIDTH))
```

```python
import functools

import jax
import jax.numpy as jnp
from jax import lax
from jax.experimental import pallas as pl
from jax.experimental.pallas import tpu as pltpu

F32 = jnp.float32
BF16 = jnp.bfloat16

D_MODEL = 4096
SEQ = 8192
DEC_BATCH = 8
DEC_SEQ = 64
PAST_LEN = 2048
CHUNK = 64
CHUNK_SHIFT = 6
GMLP_CHUNK = 128
GROUP_DIM = 128
GMLP_WIDTH = D_MODEL // 2
GMLP_GROUPS = GMLP_WIDTH // GROUP_DIM
HEAD_V = 128
QK_NOPE = 128
QK_ROPE = 64
HEADS = (D_MODEL - GMLP_WIDTH) // HEAD_V
KV_RANK = 512
Q_RANK = D_MODEL // 4
D_FF = 256 * ((8 * D_MODEL // 3 + 255) // 256)
N_MOD = 9
ROPE_THETA = 10000.0
LN_EPS = 1e-5
RMS_EPS = 1e-6
ALPHA = 2.0 ** 0.25
Q_SCALE = (QK_NOPE + QK_ROPE) ** -0.5 * 1.4426950408889634
NEG = -1e30

LANES = 128
HEAD_PAD = 2 * LANES
S_ROWS = DEC_BATCH * DEC_SEQ
M_ROWS = SEQ + S_ROWS
SEG = DEC_SEQ
N_SEG = M_ROWS // SEG
TM = M_ROWS // 8
TR = 4 * SEG
IN_COLS = 2 * GMLP_WIDTH + Q_RANK + KV_RANK
IN_TILE = 512
KV_PAD = 17 * LANES
MIB = 2 ** 20


def _params(n_axes, vmem_mib):
    return pltpu.CompilerParams(dimension_semantics=("arbitrary",) * n_axes,
                                vmem_limit_bytes=vmem_mib * MIB)


def _mod_kernel(c_ref, w_ref, b_ref, o_ref):
    c = c_ref[...]
    s = (c * jax.nn.sigmoid(c)).astype(BF16)
    o_ref[...] = jnp.dot(s, w_ref[...].astype(BF16), preferred_element_type=F32) + b_ref[...]


def _modulation(c16, w_ada, b_ada):
    tn = 512
    n = w_ada.shape[1]
    return pl.pallas_call(
        _mod_kernel,
        grid=(n // tn,),
        in_specs=[pl.BlockSpec((16, D_MODEL), lambda j: (0, 0)),
                  pl.BlockSpec((D_MODEL, tn), lambda j: (0, j)),
                  pl.BlockSpec((1, tn), lambda j: (0, j))],
        out_specs=pl.BlockSpec((16, tn), lambda j: (0, j)),
        out_shape=jax.ShapeDtypeStruct((16, n), F32),
        compiler_params=_params(1, 40),
        name="adaln_mod",
    )(c16, w_ada, b_ada.reshape(1, n))


def _seg_spec(comp):
    return pl.BlockSpec((1, 1, TR // SEG, D_MODEL), lambda i, comp=comp: (comp, i, 0, 0))


N_TR_PROMPT = SEQ // TR


def _x_specs(split, off=0):
    if not split:
        return [pl.BlockSpec((TR, D_MODEL), lambda i: (i + off, 0))]
    return [pl.BlockSpec((TR, D_MODEL), lambda i: (jnp.minimum(i, N_TR_PROMPT - 1), 0)),
            pl.BlockSpec((TR, D_MODEL), lambda i: (jnp.maximum(i - N_TR_PROMPT, 0), 0))]


def _on_owner(x_refs, body):
    if len(x_refs) == 1:
        body(x_refs[0])
    else:
        i = pl.program_id(0)
        pl.when(i < N_TR_PROMPT)(lambda: body(x_refs[0]))
        pl.when(i >= N_TR_PROMPT)(lambda: body(x_refs[1]))


def _modulate_kernel(*refs, n_x):
    sc_ref, sh_ref, h_ref = refs[n_x:]

    def body(x_ref):
        for s in range(TR // SEG):
            rows = slice(s * SEG, (s + 1) * SEG)
            h = x_ref[rows, :] * (1.0 + sc_ref[0, 0, s:s + 1, :]) + sh_ref[0, 0, s:s + 1, :]
            h_ref[rows, :] = h.astype(h_ref.dtype)

    _on_owner(refs[:n_x], body)


def _modulate(xs, segmod, sc, sh):
    return pl.pallas_call(
        functools.partial(_modulate_kernel, n_x=len(xs)),
        grid=(M_ROWS // TR,),
        in_specs=_x_specs(len(xs) == 2) + [_seg_spec(sc), _seg_spec(sh)],
        out_specs=pl.BlockSpec((TR, D_MODEL), lambda i: (i, 0)),
        out_shape=jax.ShapeDtypeStruct((M_ROWS, D_MODEL), BF16),
        compiler_params=_params(1, 32),
        name="modulate",
    )(*xs, segmod, segmod)


def _ln_kernel(*refs, n_x, gate_scale, with_h):
    y_ref, gate_ref, g_ref, b_ref = refs[n_x:n_x + 4]
    if with_h:
        sc_ref, sh_ref, xo_ref, h_ref = refs[n_x + 4:]
    else:
        (xo_ref,) = refs[n_x + 4:]

    def body(x_ref):
        for s in range(TR // SEG):
            rows = slice(s * SEG, (s + 1) * SEG)
            z = ALPHA * x_ref[rows, :] + (gate_scale * gate_ref[0, 0, s:s + 1, :]) * y_ref[rows, :]
            mu = jnp.mean(z, axis=-1, keepdims=True)
            zc = z - mu
            var = jnp.mean(zc * zc, axis=-1, keepdims=True)
            xn = zc * lax.rsqrt(var + LN_EPS) * g_ref[...] + b_ref[...]
            xo_ref[rows, :] = xn
            if with_h:
                h = xn * (1.0 + sc_ref[0, 0, s:s + 1, :]) + sh_ref[0, 0, s:s + 1, :]
                h_ref[rows, :] = h.astype(h_ref.dtype)

    _on_owner(refs[:n_x], body)


def _res_ln(xs, y, segmod, gate, gate_scale, ln_g, ln_b, sc=None, sh=None, *, row0=0, rows=M_ROWS):
    with_h = sc is not None
    off = row0 // TR
    row_in = pl.BlockSpec((TR, D_MODEL), lambda i: (i + off, 0))
    row_out = pl.BlockSpec((TR, D_MODEL), lambda i: (i, 0))
    vec = pl.BlockSpec((1, D_MODEL), lambda i: (0, 0))

    def seg(comp):
        return pl.BlockSpec((1, 1, TR // SEG, D_MODEL), lambda i, comp=comp: (comp, i + off, 0, 0))

    in_specs = _x_specs(len(xs) == 2, off) + [row_in, seg(gate), vec, vec]
    args = [*xs, y, segmod, ln_g.reshape(1, D_MODEL), ln_b.reshape(1, D_MODEL)]
    out_specs = [row_out]
    out_shape = [jax.ShapeDtypeStruct((rows, D_MODEL), F32)]
    if with_h:
        in_specs += [seg(sc), seg(sh)]
        args += [segmod, segmod]
        out_specs.append(row_out)
        out_shape.append(jax.ShapeDtypeStruct((rows, D_MODEL), BF16))
    return pl.pallas_call(
        functools.partial(_ln_kernel, n_x=len(xs), gate_scale=gate_scale, with_h=with_h),
        grid=(rows // TR,),
        in_specs=in_specs,
        out_specs=out_specs,
        out_shape=out_shape,
        compiler_params=_params(1, 48),
        name="res_ln_mod" if with_h else "res_ln",
    )(*args)


def _dot(a, b):
    return lax.dot_general(a, b, (((1,), (0,)), ((), ())), preferred_element_type=F32)


def _gateup_kernel(h_ref, wg_ref, wu_ref, o_ref):
    h = h_ref[...]
    g = _dot(h, wg_ref[...])
    u = _dot(h, wu_ref[...])
    o_ref[...] = (g * jax.nn.sigmoid(g) * u).astype(o_ref.dtype)


def _swiglu_up(h, wgu):
    tm = 2 * TM
    tn = 256
    nj = D_FF // tn
    return pl.pallas_call(
        _gateup_kernel,
        grid=(M_ROWS // tm, nj),
        in_specs=[pl.BlockSpec((tm, D_MODEL), lambda i, j: (i, 0), pipeline_mode=pl.Buffered(1)),
                  pl.BlockSpec((D_MODEL, tn), lambda i, j: (0, j)),
                  pl.BlockSpec((D_MODEL, tn), lambda i, j: (0, j + nj))],
        out_specs=pl.BlockSpec((tm, tn), lambda i, j: (i, j)),
        out_shape=jax.ShapeDtypeStruct((M_ROWS, D_FF), BF16),
        compiler_params=_params(2, 56),
        name="swiglu_up",
    )(h, wgu, wgu)


def _mm_kernel(a_ref, b_ref, o_ref):
    o_ref[...] = _dot(a_ref[...], b_ref[...]).astype(o_ref.dtype)


def _matmul(a, b, tm, tn, name, vmem_mib=48, a_mode=None):
    m, k = a.shape
    n = b.shape[1]
    return pl.pallas_call(
        _mm_kernel,
        grid=(m // tm, n // tn),
        in_specs=[pl.BlockSpec((tm, k), lambda i, j: (i, 0), pipeline_mode=a_mode),
                  pl.BlockSpec((k, tn), lambda i, j: (0, j))],
        out_specs=pl.BlockSpec((tm, tn), lambda i, j: (i, j)),
        out_shape=jax.ShapeDtypeStruct((m, n), F32),
        compiler_params=_params(2, vmem_mib),
        name=name,
    )(a, b)


def _mm_nt_kernel(a_ref, bt_ref, o_ref):
    o_ref[...] = lax.dot_general(a_ref[...], bt_ref[...], (((1,), (1,)), ((), ())), preferred_element_type=F32)


def _matmul_nt(a, bt, tm, tn, name, n_cols):
    m, k = a.shape
    return pl.pallas_call(
        _mm_nt_kernel,
        grid=(m // tm, n_cols // tn),
        in_specs=[pl.BlockSpec((tm, k), lambda i, j: (i, 0)),
                  pl.BlockSpec((tn, k), lambda i, j: (j, 0))],
        out_specs=pl.BlockSpec((tm, tn), lambda i, j: (i, j)),
        out_shape=jax.ShapeDtypeStruct((m, n_cols), F32),
        compiler_params=_params(2, 48),
        name=name,
    )(a, bt)


def _ffn_down(a, w_down):
    return _matmul(a, w_down, TM, 256, "ffn_down", vmem_mib=56, a_mode=pl.Buffered(1))


def _mm2_kernel(a1_ref, a2_ref, b_ref, o_ref):
    k1 = a1_ref.shape[1]
    acc = _dot(a1_ref[...], b_ref[:k1, :])
    acc += _dot(a2_ref[...], b_ref[k1:, :])
    o_ref[...] = acc


def _matmul_cat(a1, a2, b, tm, tn, name):
    m, k1 = a1.shape
    k2 = a2.shape[1]
    n = b.shape[1]
    return pl.pallas_call(
        _mm2_kernel,
        grid=(m // tm, n // tn),
        in_specs=[pl.BlockSpec((tm, k1), lambda i, j: (i, 0)),
                  pl.BlockSpec((tm, k2), lambda i, j: (i, 0)),
                  pl.BlockSpec((k1 + k2, tn), lambda i, j: (0, j))],
        out_specs=pl.BlockSpec((tm, tn), lambda i, j: (i, j)),
        out_shape=jax.ShapeDtypeStruct((m, n), F32),
        compiler_params=_params(2, 48),
        name=name,
    )(a1, a2, b)


def _rope128(blk, cc, ss):
    return blk * cc + pltpu.roll(blk, QK_ROPE, 1) * ss


def _q_kernel(cq_ref, g_ref, w_ref, cc_ref, ss_ref, o_ref):
    x = cq_ref[...]
    r = lax.rsqrt(jnp.mean(x * x, axis=-1, keepdims=True) + RMS_EPS)
    a = (x * r * g_ref[...]).astype(BF16)
    cc = cc_ref[...]
    ss = ss_ref[...]
    for h in range(HEADS):
        acc = _dot(a, w_ref[:, h * HEAD_PAD:(h + 1) * HEAD_PAD])
        o_ref[h, :, :LANES] = (acc[:, :LANES] * Q_SCALE).astype(o_ref.dtype)
        o_ref[h, :, LANES:] = (_rope128(acc[:, LANES:], cc, ss) * Q_SCALE).astype(o_ref.dtype)


def _q_proj(proj, q_norm_g, wuq, cc, ss):
    tm = TM // 2
    cq_block = (2 * GMLP_WIDTH) // Q_RANK
    return pl.pallas_call(
        _q_kernel,
        grid=(M_ROWS // tm,),
        in_specs=[pl.BlockSpec((tm, Q_RANK), lambda i: (i, cq_block)),
                  pl.BlockSpec((1, Q_RANK), lambda i: (0, 0)),
                  pl.BlockSpec((Q_RANK, HEADS * HEAD_PAD), lambda i: (0, 0)),
                  pl.BlockSpec((tm, LANES), lambda i: (i, 0)),
                  pl.BlockSpec((tm, LANES), lambda i: (i, 0))],
        out_specs=pl.BlockSpec((HEADS, tm, HEAD_PAD), lambda i: (0, i, 0)),
        out_shape=jax.ShapeDtypeStruct((HEADS, M_ROWS, HEAD_PAD), BF16),
        compiler_params=_params(1, 40),
        name="q_proj",
    )(proj, q_norm_g.reshape(1, Q_RANK), wuq, cc, ss)


def _latkr_kernel(ckv_ref, h_ref, wkr_ref, g_ref, cc_ref, ss_ref, lat_ref, kr_out_ref, kr128_ref):
    x = ckv_ref[...]
    r = lax.rsqrt(jnp.mean(x * x, axis=-1, keepdims=True) + RMS_EPS)
    lat_ref[...] = x * r * g_ref[...]
    blk = lax.dot_general(h_ref[...], wkr_ref[...], (((1,), (1,)), ((), ())), preferred_element_type=F32)
    rot = _rope128(blk, cc_ref[...], ss_ref[...])
    kr_out_ref[...] = rot[:, :QK_ROPE]
    kr128_ref[...] = rot


def _lat_krope(proj, h, wkr, kv_norm_g, cc, ss, row0, rows):
    tr = 512
    off = row0 // tr
    ckv_block = (2 * GMLP_WIDTH + Q_RANK) // KV_RANK
    return pl.pallas_call(
        _latkr_kernel,
        grid=(rows // tr,),
        in_specs=[pl.BlockSpec((tr, KV_RANK), lambda i: (i + off, ckv_block)),
                  pl.BlockSpec((tr, D_MODEL), lambda i: (i + off, 0)),
                  pl.BlockSpec((LANES, D_MODEL), lambda i: (0, 0)),
                  pl.BlockSpec((1, KV_RANK), lambda i: (0, 0)),
                  pl.BlockSpec((tr, LANES), lambda i: (i + off, 0)),
                  pl.BlockSpec((tr, LANES), lambda i: (i + off, 0))],
        out_specs=[pl.BlockSpec((tr, KV_RANK), lambda i: (i, 0)),
                   pl.BlockSpec((tr, QK_ROPE), lambda i: (i, 0)),
                   pl.BlockSpec((tr, LANES), lambda i: (i, 0))],
        out_shape=[jax.ShapeDtypeStruct((rows, KV_RANK), F32),
                   jax.ShapeDtypeStruct((rows, QK_ROPE), F32),
                   jax.ShapeDtypeStruct((rows, LANES), F32)],
        compiler_params=_params(1, 32),
        name="lat_krope",
    )(proj, h, wkr, kv_norm_g.reshape(1, KV_RANK), cc, ss)


def _kv_t_kernel(lat_ref, kr_ref, w_ref, wvt_ref, k_ref, vt_ref):
    a = lat_ref[...].astype(BF16)
    kr = kr_ref[...].astype(BF16)
    for h in range(HEADS):
        kn = _dot(a, w_ref[:, h * HEAD_PAD:h * HEAD_PAD + QK_NOPE])
        k_ref[h, :, :LANES] = kn.astype(BF16)
        k_ref[h, :, LANES:] = kr
        vt = lax.dot_general(wvt_ref[h], a, (((1,), (1,)), ((), ())), preferred_element_type=F32)
        vt_ref[h, 0] = vt.astype(BF16)


def _kv_proj_t(lat, kr128, wukv, wvt, tr):
    rows = lat.shape[0]
    return pl.pallas_call(
        _kv_t_kernel,
        grid=(rows // tr,),
        in_specs=[pl.BlockSpec((tr, KV_RANK), lambda i: (i, 0)),
                  pl.BlockSpec((tr, LANES), lambda i: (i, 0)),
                  pl.BlockSpec((KV_RANK, HEADS * HEAD_PAD), lambda i: (0, 0)),
                  pl.BlockSpec((HEADS, HEAD_V, KV_RANK), lambda i: (0, 0, 0))],
        out_specs=[pl.BlockSpec((HEADS, tr, HEAD_PAD), lambda i: (0, i, 0)),
                   pl.BlockSpec((HEADS, 1, HEAD_V, tr), lambda i: (0, i, 0, 0))],
        out_shape=[jax.ShapeDtypeStruct((HEADS, rows, HEAD_PAD), BF16),
                   jax.ShapeDtypeStruct((HEADS, rows // tr, HEAD_V, tr), BF16)],
        compiler_params=_params(1, 48),
        name="kv_proj_t",
    )(lat, kr128, wukv, wvt)


ATTN_TILE = 1024


def _softmax_pv_t(s, s_max, vt, carry):
    m, l, acc = carry
    m_new = jnp.maximum(m, s_max)
    p = jnp.exp2(s - m_new)
    a = jnp.exp2(m - m_new)
    l = a * l + jnp.sum(p, axis=0, keepdims=True)
    acc = a * acc + jnp.dot(vt, p.astype(BF16), preferred_element_type=F32)
    return m_new, l, acc


def _chunk_mask_t(q_rel0, nk, nq):
    kc = lax.broadcasted_iota(jnp.int32, (nk, nq), 0) >> CHUNK_SHIFT
    qc = (q_rel0 + lax.broadcasted_iota(jnp.int32, (nk, nq), 1)) >> CHUNK_SHIFT
    return qc >= kc


def _attn_prompt_kernel(q_ref, k_ref, vt_ref, o_ref, s_even, s_odd, *, rb):
    qi = pl.program_id(1)
    n_rb = ATTN_TILE // rb

    def scores(j, s_ref):
        k = k_ref[0, pl.ds(pl.multiple_of(j * ATTN_TILE, ATTN_TILE), ATTN_TILE), :]
        s = lax.dot_general(k, q_ref[0], (((1,), (1,)), ((), ())), preferred_element_type=F32)
        s_ref[...] = s
        return jnp.max(s, axis=0, keepdims=True)

    def make_step(s_cur, s_next):
        def step(j, carry):
            stats, s_max = carry
            next_max = scores(j + 1, s_next)
            vt = vt_ref[0, j]
            stats = tuple(_softmax_pv_t(s_cur[:, r * rb:(r + 1) * rb], s_max[:, r * rb:(r + 1) * rb], vt, stats[r])
                          for r in range(n_rb))
            return stats, next_max
        return step

    step_even, step_odd = make_step(s_even, s_odd), make_step(s_odd, s_even)

    def finish(s_ref, stats):
        for r in range(n_rb):
            nk = (r + 1) * rb
            s_r = jnp.where(_chunk_mask_t(r * rb, nk, rb), s_ref[:nk, r * rb:(r + 1) * rb], NEG)
            _, l, acc = _softmax_pv_t(s_r, jnp.max(s_r, axis=0, keepdims=True), vt_ref[0, qi, :, :nk], stats[r])
            o_ref[r * rb:(r + 1) * rb, :] = (acc / l).T.astype(o_ref.dtype)

    init = tuple((jnp.full((1, rb), NEG, F32), jnp.zeros((1, rb), F32), jnp.zeros((HEAD_V, rb), F32))
                 for _ in range(n_rb))
    carry = (init, scores(0, s_even))
    carry = lax.fori_loop(0, qi >> 1, lambda p, c: step_odd(2 * p + 1, step_even(2 * p, c)), carry)
    odd = (qi & 1) == 1
    stats, _ = lax.cond(odd, lambda: step_even(qi - 1, carry), lambda: carry)
    lax.cond(odd, lambda: finish(s_odd, stats), lambda: finish(s_even, stats))


def _attn_prompt(q, k, vt):
    return pl.pallas_call(
        functools.partial(_attn_prompt_kernel, rb=512),
        grid=(HEADS, SEQ // ATTN_TILE),
        in_specs=[pl.BlockSpec((1, ATTN_TILE, HEAD_PAD), lambda h, i: (h, i, 0)),
                  pl.BlockSpec((1, SEQ, HEAD_PAD), lambda h, i: (h, 0, 0)),
                  pl.BlockSpec((1, SEQ // ATTN_TILE, HEAD_V, ATTN_TILE), lambda h, i: (h, 0, 0, 0))],
        out_specs=pl.BlockSpec((ATTN_TILE, HEAD_V), lambda h, i: (i, h)),
        out_shape=jax.ShapeDtypeStruct((M_ROWS, HEADS * HEAD_V), BF16),
        scratch_shapes=[pltpu.VMEM((ATTN_TILE, ATTN_TILE), F32), pltpu.VMEM((ATTN_TILE, ATTN_TILE), F32)],
        compiler_params=_params(2, 48),
        name="attn_prompt",
    )(q, k, vt)


def _attn_sample_kernel(q_ref, lat_ref, kr_ref, w_ref, o_prev_ref, o_ref, qa_scr):
    del o_prev_ref
    nt = (((1,), (1,)), ((), ()))
    lat = lat_ref[0].astype(BF16)
    keys = jnp.concatenate([lat, kr_ref[0].astype(BF16)], axis=1)
    for h in range(HEADS):
        rows = slice(h * DEC_SEQ, (h + 1) * DEC_SEQ)
        w_uk = w_ref[:, h * HEAD_PAD:h * HEAD_PAD + QK_NOPE]
        q_lat = lax.dot_general(q_ref[h, :, :LANES], w_uk, nt, preferred_element_type=F32)
        qa_scr[rows, :KV_RANK] = q_lat.astype(BF16)
        qa_scr[rows, KV_RANK:] = q_ref[h, :, LANES:]
    s = lax.dot_general(qa_scr[...], keys, nt, preferred_element_type=F32)
    shape = (HEADS * DEC_SEQ, KV_PAD)
    q_pos = PAST_LEN + (lax.broadcasted_iota(jnp.int32, shape, 0) & (DEC_SEQ - 1))
    k_pos = lax.broadcasted_iota(jnp.int32, shape, 1)
    visible = ((q_pos >> CHUNK_SHIFT) >= (k_pos >> CHUNK_SHIFT)) & (k_pos < PAST_LEN + DEC_SEQ)
    s = jnp.where(visible, s, NEG)
    m = jnp.max(s, axis=-1, keepdims=True)
    p = jnp.exp2(s - m)
    l = jnp.sum(p, axis=-1, keepdims=True)
    o_lat = (jnp.dot(p.astype(BF16), lat, preferred_element_type=F32) / l).astype(BF16)
    for h in range(HEADS):
        w_uv = w_ref[:, h * HEAD_PAD + QK_NOPE:(h + 1) * HEAD_PAD]
        o_h = _dot(o_lat[h * DEC_SEQ:(h + 1) * DEC_SEQ, :], w_uv)
        o_ref[:, h * HEAD_V:(h + 1) * HEAD_V] = o_h.astype(o_ref.dtype)


def _attn_sample(q, lat_all, kr_all, wukv, o_prev):
    q_off = SEQ // DEC_SEQ
    return pl.pallas_call(
        _attn_sample_kernel,
        grid=(DEC_BATCH,),
        in_specs=[pl.BlockSpec((HEADS, DEC_SEQ, HEAD_PAD), lambda b: (0, q_off + b, 0)),
                  pl.BlockSpec((1, KV_PAD, KV_RANK), lambda b: (b, 0, 0)),
                  pl.BlockSpec((1, KV_PAD, LANES), lambda b: (b, 0, 0)),
                  pl.BlockSpec((KV_RANK, HEADS * HEAD_PAD), lambda b: (0, 0)),
                  pl.BlockSpec(memory_space=pl.ANY)],
        out_specs=pl.BlockSpec((DEC_SEQ, HEADS * HEAD_V), lambda b: (q_off + b, 0)),
        out_shape=jax.ShapeDtypeStruct((M_ROWS, HEADS * HEAD_V), BF16),
        scratch_shapes=[pltpu.VMEM((HEADS * DEC_SEQ, KV_RANK + LANES), BF16)],
        input_output_aliases={4: 0},
        compiler_params=_params(1, 48),
        name="attn_sample",
    )(q, lat_all, kr_all, wukv, o_prev)


def _gmlp_kernel(u_ref, v_ref, g_ref, b_ref, w_ref, bs_ref, *rest, chunk, emit_vn):
    if emit_vn:
        _, a_ref, vn_ref = rest
    else:
        (a_ref,) = rest
    v = v_ref[...]
    mu = jnp.mean(v, axis=-1, keepdims=True)
    vc = v - mu
    var = jnp.mean(vc * vc, axis=-1, keepdims=True)
    vn = vc * lax.rsqrt(var + LN_EPS) * g_ref[...] + b_ref[...]
    if emit_vn:
        vn_ref[...] = vn
    vnb = vn.astype(BF16)
    n_chunks = v.shape[0] // chunk
    causal = (lax.broadcasted_iota(jnp.int32, (chunk, GMLP_CHUNK), 0)
              >= lax.broadcasted_iota(jnp.int32, (chunk, GMLP_CHUNK), 1))
    for g in range(GMLP_GROUPS):
        cols = slice(g * GROUP_DIM, (g + 1) * GROUP_DIM)
        w = jnp.where(causal, w_ref[g, :chunk, :], 0.0).astype(BF16)
        rhs = jnp.concatenate([vnb[c * chunk:(c + 1) * chunk, cols] for c in range(n_chunks)], axis=1)
        if chunk < GMLP_CHUNK:
            rhs = jnp.concatenate([rhs, jnp.zeros((GMLP_CHUNK - chunk, rhs.shape[1]), BF16)], axis=0)
        mixed = jnp.dot(w, rhs, preferred_element_type=F32) + bs_ref[g, :chunk, :]
        for c in range(n_chunks):
            rows = slice(c * chunk, (c + 1) * chunk)
            gate = mixed[:, c * GROUP_DIM:(c + 1) * GROUP_DIM]
            a_ref[rows, cols] = (u_ref[rows, cols] * gate).astype(a_ref.dtype)


def _gmlp(proj, ln_g, ln_b, w_s, b_s3, row0, rows, tr, chunk, a_prev=None):
    emit_vn = a_prev is not None
    off = row0 // tr
    vec = pl.BlockSpec((1, GMLP_WIDTH), lambda i: (0, 0))
    in_specs = [pl.BlockSpec((tr, GMLP_WIDTH), lambda i: (i + off, 0)),
                pl.BlockSpec((tr, GMLP_WIDTH), lambda i: (i + off, 1)),
                vec, vec,
                pl.BlockSpec((GMLP_GROUPS, GMLP_CHUNK, GMLP_CHUNK), lambda i: (0, 0, 0)),
                pl.BlockSpec((GMLP_GROUPS, GMLP_CHUNK, 1), lambda i: (0, 0, 0))]
    args = [proj, proj, ln_g.reshape(1, GMLP_WIDTH), ln_b.reshape(1, GMLP_WIDTH), w_s, b_s3]
    out_specs = [pl.BlockSpec((tr, GMLP_WIDTH), lambda i: (i + off, 0))]
    out_shape = [jax.ShapeDtypeStruct((M_ROWS, GMLP_WIDTH), BF16)]
    aliases = {}
    if emit_vn:
        in_specs.append(pl.BlockSpec(memory_space=pl.ANY))
        args.append(a_prev)
        aliases = {len(args) - 1: 0}
        out_specs.append(pl.BlockSpec((tr, GMLP_WIDTH), lambda i: (i, 0)))
        out_shape.append(jax.ShapeDtypeStruct((rows, GMLP_WIDTH), F32))
    return pl.pallas_call(
        functools.partial(_gmlp_kernel, chunk=chunk, emit_vn=emit_vn),
        grid=(rows // tr,),
        in_specs=in_specs,
        out_specs=out_specs,
        out_shape=out_shape,
        input_output_aliases=aliases,
        compiler_params=_params(1, 40),
        name="gmlp_gate_vn" if emit_vn else "gmlp_gate",
    )(*args)


def _swap_halves(w):
    half = w.shape[-1] // 2
    return jnp.concatenate([w[..., half:], w[..., :half]], axis=-1)


def _rope_tables():
    pos = jnp.concatenate([jnp.arange(SEQ, dtype=jnp.int32),
                           jnp.tile(PAST_LEN + jnp.arange(DEC_SEQ, dtype=jnp.int32), DEC_BATCH)])
    inv = 1.0 / (ROPE_THETA ** (jnp.arange(0, QK_ROPE, 2, dtype=F32) / QK_ROPE))
    ang = pos.astype(F32)[:, None] * inv[None, :]
    cos, sin = jnp.cos(ang), jnp.sin(ang)
    zero = jnp.zeros((M_ROWS, LANES - QK_ROPE), F32)
    return (jnp.concatenate([cos, cos, zero], axis=1), jnp.concatenate([-sin, sin, zero], axis=1))


def kernel(x_prompt, x_sample, cache_mla_latent, cache_mla_krope, c_prompt, c_sample, w_ada, b_ada, ffn1_w_gate_up, ffn1_w_down, ln1_g, ln1_b, w_in, gmlp_ln_g, gmlp_ln_b, gmlp_w_s, gmlp_b_s, mla_q_norm_g, mla_w_uq, mla_kv_norm_g, mla_w_ukv, w_out, ln2_g, ln2_b, ffn2_w_gate_up, ffn2_w_down, ln3_g, ln3_b):
    wukv = mla_w_ukv.astype(BF16)
    wvt = mla_w_ukv.reshape(KV_RANK, HEADS, HEAD_PAD)[:, :, QK_NOPE:].transpose(1, 2, 0).astype(BF16)
    w_in_t = w_in.T
    w_kr_t = w_in_t[IN_COLS:]
    wkr = jnp.concatenate([w_kr_t, w_kr_t[QK_ROPE // 2:], w_kr_t[:QK_ROPE // 2]], axis=0)
    uq = mla_w_uq.reshape(Q_RANK, HEADS, QK_NOPE + QK_ROPE)
    wuq = jnp.concatenate([uq, _swap_halves(uq[..., QK_NOPE:])], axis=-1).reshape(Q_RANK, HEADS * HEAD_PAD).astype(BF16)
    b_s3 = gmlp_b_s[:, :, None]
    cc, ss = _rope_tables()

    c16 = jnp.concatenate([c_prompt, c_sample, jnp.zeros((16 - 1 - DEC_BATCH, D_MODEL), F32)], axis=0)
    mod = _modulation(c16, w_ada, b_ada).reshape(16, N_MOD, D_MODEL)
    segmod = jnp.concatenate([jnp.broadcast_to(mod[0:1], (SEQ // SEG, N_MOD, D_MODEL)), mod[1:1 + DEC_BATCH]], axis=0)
    segmod = segmod.transpose(1, 0, 2).reshape(N_MOD, M_ROWS // TR, TR // SEG, D_MODEL)
    SH1, SC1, G1, SH2, SC2, G2, SH3, SC3, G3 = range(N_MOD)

    x0 = (x_prompt.reshape(SEQ, D_MODEL), x_sample.reshape(S_ROWS, D_MODEL))

    h1 = _modulate(x0, segmod, SC1, SH1)
    y1 = _ffn_down(_swiglu_up(h1, ffn1_w_gate_up), ffn1_w_down)
    x1, h2 = _res_ln(x0, y1, segmod, G1, 0.5, ln1_g, ln1_b, SC2, SH2)

    proj = _matmul_nt(h2, w_in_t, TM, IN_TILE, "in_proj", n_cols=IN_COLS)
    (a_mix,) = _gmlp(proj, gmlp_ln_g, gmlp_ln_b, gmlp_w_s, b_s3, 0, SEQ, 512, GMLP_CHUNK)
    a_mix, vn_s = _gmlp(proj, gmlp_ln_g, gmlp_ln_b, gmlp_w_s, b_s3, SEQ, S_ROWS, DEC_SEQ, DEC_SEQ, a_prev=a_mix)
    q = _q_proj(proj, mla_q_norm_g, wuq, cc, ss)
    lat_p, kr_p, kr128_p = _lat_krope(proj, h2, wkr, mla_kv_norm_g, cc, ss, 0, SEQ)
    lat_s, kr_s, kr128_s = _lat_krope(proj, h2, wkr, mla_kv_norm_g, cc, ss, SEQ, S_ROWS)

    k_p, vt_p = _kv_proj_t(lat_p, kr128_p, wukv, wvt, ATTN_TILE)
    o_mix = _attn_prompt(q, k_p, vt_p)

    pad_rows = KV_PAD - PAST_LEN - DEC_SEQ
    lat_all = jnp.concatenate([cache_mla_latent, lat_s.reshape(DEC_BATCH, DEC_SEQ, KV_RANK),
                               jnp.zeros((DEC_BATCH, pad_rows, KV_RANK), F32)], axis=1)
    cache_kr128 = jnp.pad(cache_mla_krope, ((0, 0), (0, 0), (0, LANES - QK_ROPE)))
    kr_all = jnp.concatenate([cache_kr128, kr128_s.reshape(DEC_BATCH, DEC_SEQ, LANES),
                              jnp.zeros((DEC_BATCH, pad_rows, LANES), F32)], axis=1)
    o_mix = _attn_sample(q, lat_all, kr_all, wukv, o_mix)

    mix = _matmul_cat(a_mix, o_mix, w_out, TM, 512, "out_proj")
    x2, h3 = _res_ln((x1,), mix, segmod, G2, 1.0, ln2_g, ln2_b, SC3, SH3)

    y3 = _ffn_down(_swiglu_up(h3, ffn2_w_gate_up), ffn2_w_down)
    (y_p,) = _res_ln((x2,), y3, segmod, G3, 0.5, ln3_g, ln3_b, row0=0, rows=SEQ)
    (y_s,) = _res_ln((x2,), y3, segmod, G3, 0.5, ln3_g, ln3_b, row0=SEQ, rows=S_ROWS)

    return (y_p.reshape(1, SEQ, D_MODEL),
            y_s.reshape(DEC_BATCH, DEC_SEQ, D_MODEL),
            lat_p.reshape(1, SEQ, KV_RANK),
            kr_p.reshape(1, SEQ, QK_ROPE),
            lat_s.reshape(DEC_BATCH, DEC_SEQ, KV_RANK),
            kr_s.reshape(DEC_BATCH, DEC_SEQ, QK_ROPE),
            vn_s.reshape(DEC_BATCH, DEC_SEQ, GMLP_WIDTH))
```

```python
import functools

import jax
import jax.numpy as jnp
from jax import lax
from jax.experimental import pallas as pl
from jax.experimental.pallas import tpu as pltpu

F32 = jnp.float32
BF16 = jnp.bfloat16

D_MODEL = 4096
SEQ = 8192
DEC_BATCH = 8
DEC_SEQ = 64
PAST_LEN = 2048
CHUNK = 64
CHUNK_SHIFT = 6
GMLP_CHUNK = 128
GROUP_DIM = 128
GMLP_WIDTH = D_MODEL // 2
GMLP_GROUPS = GMLP_WIDTH // GROUP_DIM
HEAD_V = 128
QK_NOPE = 128
QK_ROPE = 64
HEADS = (D_MODEL - GMLP_WIDTH) // HEAD_V
KV_RANK = 512
Q_RANK = D_MODEL // 4
D_FF = 256 * ((8 * D_MODEL // 3 + 255) // 256)
N_MOD = 9
ROPE_THETA = 10000.0
LN_EPS = 1e-5
RMS_EPS = 1e-6
ALPHA = 2.0 ** 0.25
Q_SCALE = (QK_NOPE + QK_ROPE) ** -0.5 * 1.4426950408889634
NEG = -1e30

LANES = 128
HEAD_PAD = 2 * LANES
S_ROWS = DEC_BATCH * DEC_SEQ
M_ROWS = SEQ + S_ROWS
SEG = DEC_SEQ
N_SEG = M_ROWS // SEG
TM = M_ROWS // 8
TR = 4 * SEG
IN_COLS = 2 * GMLP_WIDTH + Q_RANK + KV_RANK
IN_TILE = 512
KV_PAD = 17 * LANES
MIB = 2 ** 20


def _params(n_axes, vmem_mib):
    return pltpu.CompilerParams(dimension_semantics=("arbitrary",) * n_axes,
                                vmem_limit_bytes=vmem_mib * MIB)


C_ROWS = 16


def _adaln_tile(c_ref, w_ref, b_ref):
    c = c_ref[...]
    s = (c * jax.nn.sigmoid(c)).astype(BF16)
    return lax.dot_general(s, w_ref[...], (((1,), (0,)), ((), ())), preferred_element_type=F32) + b_ref[...]


def _adaln_specs(tn, comp0, block_of_step):
    per_comp = D_MODEL // tn

    def col(*idx):
        return comp0 * per_comp + block_of_step(*idx)

    in_specs = [pl.BlockSpec((C_ROWS, D_MODEL), lambda *idx: (0, 0)),
                pl.BlockSpec((D_MODEL, tn), lambda *idx: (0, col(*idx))),
                pl.BlockSpec((1, tn), lambda *idx: (0, col(*idx)))]
    out_spec = pl.BlockSpec((1, C_ROWS, tn),
                            lambda *idx: (block_of_step(*idx) // per_comp, 0, block_of_step(*idx) % per_comp))
    return in_specs, out_spec


def _mod_kernel(c_ref, w_ref, b_ref, o_ref):
    o_ref[0] = _adaln_tile(c_ref, w_ref, b_ref)


def _modulation(c16, w_ada, b_ada, comp0, n_comp):
    tn = 512
    in_specs, out_spec = _adaln_specs(tn, comp0, lambda j: j)
    return pl.pallas_call(
        _mod_kernel,
        grid=(n_comp * D_MODEL // tn,),
        in_specs=in_specs,
        out_specs=out_spec,
        out_shape=jax.ShapeDtypeStruct((n_comp, C_ROWS, D_MODEL), F32),
        compiler_params=_params(1, 40),
        name="adaln_mod",
    )(c16, w_ada, b_ada.reshape(1, -1))


def _segment_table(mod):
    n = mod.shape[0]
    seg = jnp.concatenate([jnp.broadcast_to(mod[:, 0:1], (n, SEQ // SEG, D_MODEL)), mod[:, 1:1 + DEC_BATCH]], axis=1)
    return seg.reshape(n, M_ROWS // TR, TR // SEG, D_MODEL)


def _seg_spec(comp, off=0):
    return pl.BlockSpec((1, 1, TR // SEG, D_MODEL), lambda i: (comp, i + off, 0, 0))


N_TR_PROMPT = SEQ // TR


def _x_specs(split, off=0):
    if not split:
        return [pl.BlockSpec((TR, D_MODEL), lambda i: (i + off, 0))]
    return [pl.BlockSpec((TR, D_MODEL), lambda i: (jnp.minimum(i, N_TR_PROMPT - 1), 0)),
            pl.BlockSpec((TR, D_MODEL), lambda i: (jnp.maximum(i - N_TR_PROMPT, 0), 0))]


def _on_owner(x_refs, body):
    if len(x_refs) == 1:
        body(x_refs[0])
    else:
        i = pl.program_id(0)
        pl.when(i < N_TR_PROMPT)(lambda: body(x_refs[0]))
        pl.when(i >= N_TR_PROMPT)(lambda: body(x_refs[1]))


def _modulate_kernel(*refs, n_x):
    sc_ref, sh_ref, h_ref = refs[n_x:]

    def body(x_ref):
        for s in range(TR // SEG):
            rows = slice(s * SEG, (s + 1) * SEG)
            h = x_ref[rows, :] * (1.0 + sc_ref[0, 0, s:s + 1, :]) + sh_ref[0, 0, s:s + 1, :]
            h_ref[rows, :] = h.astype(h_ref.dtype)

    _on_owner(refs[:n_x], body)


def _modulate(xs, sc, sh):
    return pl.pallas_call(
        functools.partial(_modulate_kernel, n_x=len(xs)),
        grid=(M_ROWS // TR,),
        in_specs=_x_specs(len(xs) == 2) + [_seg_spec(sc[1]), _seg_spec(sh[1])],
        out_specs=pl.BlockSpec((TR, D_MODEL), lambda i: (i, 0)),
        out_shape=jax.ShapeDtypeStruct((M_ROWS, D_MODEL), BF16),
        compiler_params=_params(1, 32),
        name="modulate",
    )(*xs, sc[0], sh[0])


def _ln_kernel(*refs, n_x, gate_scale, with_h):
    y_ref, gate_ref, g_ref, b_ref = refs[n_x:n_x + 4]
    if with_h:
        sc_ref, sh_ref, xo_ref, h_ref = refs[n_x + 4:]
    else:
        (xo_ref,) = refs[n_x + 4:]

    def body(x_ref):
        for s in range(TR // SEG):
            rows = slice(s * SEG, (s + 1) * SEG)
            z = ALPHA * x_ref[rows, :] + (gate_scale * gate_ref[0, 0, s:s + 1, :]) * y_ref[rows, :]
            mu = jnp.mean(z, axis=-1, keepdims=True)
            zc = z - mu
            var = jnp.mean(zc * zc, axis=-1, keepdims=True)
            xn = zc * lax.rsqrt(var + LN_EPS) * g_ref[...] + b_ref[...]
            xo_ref[rows, :] = xn
            if with_h:
                h = xn * (1.0 + sc_ref[0, 0, s:s + 1, :]) + sh_ref[0, 0, s:s + 1, :]
                h_ref[rows, :] = h.astype(h_ref.dtype)

    _on_owner(refs[:n_x], body)


def _res_ln(xs, y, gate, gate_scale, ln_g, ln_b, sc=None, sh=None, *, row0=0, rows=M_ROWS):
    with_h = sc is not None
    off = row0 // TR
    row_in = pl.BlockSpec((TR, D_MODEL), lambda i: (i + off, 0))
    row_out = pl.BlockSpec((TR, D_MODEL), lambda i: (i, 0))
    vec = pl.BlockSpec((1, D_MODEL), lambda i: (0, 0))

    in_specs = _x_specs(len(xs) == 2, off) + [row_in, _seg_spec(gate[1], off), vec, vec]
    args = [*xs, y, gate[0], ln_g.reshape(1, D_MODEL), ln_b.reshape(1, D_MODEL)]
    out_specs = [row_out]
    out_shape = [jax.ShapeDtypeStruct((rows, D_MODEL), F32)]
    if with_h:
        in_specs += [_seg_spec(sc[1], off), _seg_spec(sh[1], off)]
        args += [sc[0], sh[0]]
        out_specs.append(row_out)
        out_shape.append(jax.ShapeDtypeStruct((rows, D_MODEL), BF16))
    return pl.pallas_call(
        functools.partial(_ln_kernel, n_x=len(xs), gate_scale=gate_scale, with_h=with_h),
        grid=(rows // TR,),
        in_specs=in_specs,
        out_specs=out_specs,
        out_shape=out_shape,
        compiler_params=_params(1, 48),
        name="res_ln_mod" if with_h else "res_ln",
    )(*args)


def _dot(a, b):
    return lax.dot_general(a, b, (((1,), (0,)), ((), ())), preferred_element_type=F32)


def _gateup_kernel(h_ref, wg_ref, wu_ref, wd_ref, o_ref, wd_out_ref):
    h = h_ref[...]
    g = _dot(h, wg_ref[...])
    u = _dot(h, wu_ref[...])
    o_ref[...] = (g * jax.nn.sigmoid(g) * u).astype(o_ref.dtype)

    @pl.when(pl.program_id(0) == 0)
    def _():
        wd_out_ref[...] = wd_ref[...].astype(wd_out_ref.dtype)


def _swiglu_up(h, wgu, w_down):
    tm = 2 * TM
    tn = 256
    nj = D_FF // tn
    slab = pl.BlockSpec((tn, D_MODEL), lambda i, j: (jnp.where(i == 0, j, nj - 1), 0))
    return pl.pallas_call(
        _gateup_kernel,
        grid=(M_ROWS // tm, nj),
        in_specs=[pl.BlockSpec((tm, D_MODEL), lambda i, j: (i, 0), pipeline_mode=pl.Buffered(1)),
                  pl.BlockSpec((D_MODEL, tn), lambda i, j: (0, j)),
                  pl.BlockSpec((D_MODEL, tn), lambda i, j: (0, j + nj)),
                  slab],
        out_specs=[pl.BlockSpec((tm, tn), lambda i, j: (i, j)), slab],
        out_shape=[jax.ShapeDtypeStruct((M_ROWS, D_FF), BF16),
                   jax.ShapeDtypeStruct((D_FF, D_MODEL), BF16)],
        compiler_params=_params(2, 56),
        name="swiglu_up",
    )(h, wgu, wgu, w_down)


def _mm_kernel(a_ref, b_ref, o_ref):
    o_ref[...] = _dot(a_ref[...], b_ref[...]).astype(o_ref.dtype)


def _matmul(a, b, tm, tn, name, vmem_mib=48, a_mode=None):
    m, k = a.shape
    n = b.shape[1]
    return pl.pallas_call(
        _mm_kernel,
        grid=(m // tm, n // tn),
        in_specs=[pl.BlockSpec((tm, k), lambda i, j: (i, 0), pipeline_mode=a_mode),
                  pl.BlockSpec((k, tn), lambda i, j: (0, j))],
        out_specs=pl.BlockSpec((tm, tn), lambda i, j: (i, j)),
        out_shape=jax.ShapeDtypeStruct((m, n), F32),
        compiler_params=_params(2, vmem_mib),
        name=name,
    )(a, b)


def _mm_nt_kernel(a_ref, bt_ref, o_ref):
    o_ref[...] = lax.dot_general(a_ref[...], bt_ref[...], (((1,), (1,)), ((), ())), preferred_element_type=F32)


def _matmul_nt(a, bt, tm, tn, name, n_cols):
    m, k = a.shape
    return pl.pallas_call(
        _mm_nt_kernel,
        grid=(m // tm, n_cols // tn),
        in_specs=[pl.BlockSpec((tm, k), lambda i, j: (i, 0)),
                  pl.BlockSpec((tn, k), lambda i, j: (j, 0))],
        out_specs=pl.BlockSpec((tm, tn), lambda i, j: (i, j)),
        out_shape=jax.ShapeDtypeStruct((m, n_cols), F32),
        compiler_params=_params(2, 48),
        name=name,
    )(a, bt)


def _mm_adaln_kernel(a_ref, b_ref, c_ref, w_ref, bias_ref, o_ref, mod_ref):
    o_ref[...] = _dot(a_ref[...], b_ref[...])
    mod_ref[0] = _adaln_tile(c_ref, w_ref, bias_ref)


def _ffn_down(a, w_down_bf16, adaln=None):
    tm, tn = TM // 2, 512
    if adaln is None:
        return _matmul(a, w_down_bf16, tm, tn, "ffn_down", vmem_mib=56)
    c16, w_ada, b_ada, comp0, n_comp = adaln
    m, k = a.shape
    n = w_down_bf16.shape[1]
    side_tn = 256
    n_side = n_comp * D_MODEL // side_tn
    assert n_side <= (m // tm) * (n // tn)
    side_in, side_out = _adaln_specs(side_tn, comp0, lambda i, j: jnp.minimum(i * (n // tn) + j, n_side - 1))
    return pl.pallas_call(
        _mm_adaln_kernel,
        grid=(m // tm, n // tn),
        in_specs=[pl.BlockSpec((tm, k), lambda i, j: (i, 0)),
                  pl.BlockSpec((k, tn), lambda i, j: (0, j))] + side_in,
        out_specs=[pl.BlockSpec((tm, tn), lambda i, j: (i, j)), side_out],
        out_shape=[jax.ShapeDtypeStruct((m, n), F32),
                   jax.ShapeDtypeStruct((n_comp, C_ROWS, D_MODEL), F32)],
        compiler_params=_params(2, 60),
        name="ffn_down_adaln",
    )(a, w_down_bf16, c16, w_ada, b_ada.reshape(1, -1))


def _mm2_kernel(a1_ref, a2_ref, b_ref, o_ref):
    k1 = a1_ref.shape[1]
    acc = _dot(a1_ref[...], b_ref[:k1, :])
    acc += _dot(a2_ref[...], b_ref[k1:, :])
    o_ref[...] = acc


def _matmul_cat(a1, a2, b, tm, tn, name):
    m, k1 = a1.shape
    k2 = a2.shape[1]
    n = b.shape[1]
    return pl.pallas_call(
        _mm2_kernel,
        grid=(m // tm, n // tn),
        in_specs=[pl.BlockSpec((tm, k1), lambda i, j: (i, 0)),
                  pl.BlockSpec((tm, k2), lambda i, j: (i, 0)),
                  pl.BlockSpec((k1 + k2, tn), lambda i, j: (0, j))],
        out_specs=pl.BlockSpec((tm, tn), lambda i, j: (i, j)),
        out_shape=jax.ShapeDtypeStruct((m, n), F32),
        compiler_params=_params(2, 48),
        name=name,
    )(a1, a2, b)


def _rope128(blk, cc, ss):
    return blk * cc + pltpu.roll(blk, QK_ROPE, 1) * ss


def _q_kernel(cq_ref, g_ref, w_ref, cc_ref, ss_ref, o_ref):
    x = cq_ref[...]
    r = lax.rsqrt(jnp.mean(x * x, axis=-1, keepdims=True) + RMS_EPS)
    a = (x * r * g_ref[...]).astype(BF16)
    cc = cc_ref[...]
    ss = ss_ref[...]
    for h in range(HEADS):
        acc = _dot(a, w_ref[:, h * HEAD_PAD:(h + 1) * HEAD_PAD])
        o_ref[h, :, :LANES] = (acc[:, :LANES] * Q_SCALE).astype(o_ref.dtype)
        o_ref[h, :, LANES:] = (_rope128(acc[:, LANES:], cc, ss) * Q_SCALE).astype(o_ref.dtype)


def _q_proj(proj, q_norm_g, wuq, cc, ss):
    tm = TM // 2
    cq_block = (2 * GMLP_WIDTH) // Q_RANK
    return pl.pallas_call(
        _q_kernel,
        grid=(M_ROWS // tm,),
        in_specs=[pl.BlockSpec((tm, Q_RANK), lambda i: (i, cq_block)),
                  pl.BlockSpec((1, Q_RANK), lambda i: (0, 0)),
                  pl.BlockSpec((Q_RANK, HEADS * HEAD_PAD), lambda i: (0, 0)),
                  pl.BlockSpec((tm, LANES), lambda i: (i, 0)),
                  pl.BlockSpec((tm, LANES), lambda i: (i, 0))],
        out_specs=pl.BlockSpec((HEADS, tm, HEAD_PAD), lambda i: (0, i, 0)),
        out_shape=jax.ShapeDtypeStruct((HEADS, M_ROWS, HEAD_PAD), BF16),
        compiler_params=_params(1, 40),
        name="q_proj",
    )(proj, q_norm_g.reshape(1, Q_RANK), wuq, cc, ss)


def _latkr_kernel(ckv_ref, h_ref, wkr_ref, g_ref, cc_ref, ss_ref, lat_ref, kr_out_ref, kr128_ref):
    x = ckv_ref[...]
    r = lax.rsqrt(jnp.mean(x * x, axis=-1, keepdims=True) + RMS_EPS)
    lat_ref[...] = x * r * g_ref[...]
    blk = lax.dot_general(h_ref[...], wkr_ref[...], (((1,), (1,)), ((), ())), preferred_element_type=F32)
    rot = _rope128(blk, cc_ref[...], ss_ref[...])
    kr_out_ref[...] = rot[:, :QK_ROPE]
    kr128_ref[...] = rot


def _lat_krope(proj, h, wkr, kv_norm_g, cc, ss, row0, rows):
    tr = 512
    off = row0 // tr
    ckv_block = (2 * GMLP_WIDTH + Q_RANK) // KV_RANK
    return pl.pallas_call(
        _latkr_kernel,
        grid=(rows // tr,),
        in_specs=[pl.BlockSpec((tr, KV_RANK), lambda i: (i + off, ckv_block)),
                  pl.BlockSpec((tr, D_MODEL), lambda i: (i + off, 0)),
                  pl.BlockSpec((LANES, D_MODEL), lambda i: (0, 0)),
                  pl.BlockSpec((1, KV_RANK), lambda i: (0, 0)),
                  pl.BlockSpec((tr, LANES), lambda i: (i + off, 0)),
                  pl.BlockSpec((tr, LANES), lambda i: (i + off, 0))],
        out_specs=[pl.BlockSpec((tr, KV_RANK), lambda i: (i, 0)),
                   pl.BlockSpec((tr, QK_ROPE), lambda i: (i, 0)),
                   pl.BlockSpec((tr, LANES), lambda i: (i, 0))],
        out_shape=[jax.ShapeDtypeStruct((rows, KV_RANK), F32),
                   jax.ShapeDtypeStruct((rows, QK_ROPE), F32),
                   jax.ShapeDtypeStruct((rows, LANES), F32)],
        compiler_params=_params(1, 32),
        name="lat_krope",
    )(proj, h, wkr, kv_norm_g.reshape(1, KV_RANK), cc, ss)


def _kv_t_kernel(lat_ref, kr_ref, w_ref, wvt_ref, k_ref, vt_ref):
    a = lat_ref[...].astype(BF16)
    kr = kr_ref[...].astype(BF16)
    for h in range(HEADS):
        kn = _dot(a, w_ref[:, h * HEAD_PAD:h * HEAD_PAD + QK_NOPE])
        k_ref[h, :, :LANES] = kn.astype(BF16)
        k_ref[h, :, LANES:] = kr
        vt = lax.dot_general(wvt_ref[h], a, (((1,), (1,)), ((), ())), preferred_element_type=F32)
        vt_ref[h, 0] = vt.astype(BF16)


def _kv_proj_t(lat, kr128, wukv, wvt, tr):
    rows = lat.shape[0]
    return pl.pallas_call(
        _kv_t_kernel,
        grid=(rows // tr,),
        in_specs=[pl.BlockSpec((tr, KV_RANK), lambda i: (i, 0)),
                  pl.BlockSpec((tr, LANES), lambda i: (i, 0)),
                  pl.BlockSpec((KV_RANK, HEADS * HEAD_PAD), lambda i: (0, 0)),
                  pl.BlockSpec((HEADS, HEAD_V, KV_RANK), lambda i: (0, 0, 0))],
        out_specs=[pl.BlockSpec((HEADS, tr, HEAD_PAD), lambda i: (0, i, 0)),
                   pl.BlockSpec((HEADS, 1, HEAD_V, tr), lambda i: (0, i, 0, 0))],
        out_shape=[jax.ShapeDtypeStruct((HEADS, rows, HEAD_PAD), BF16),
                   jax.ShapeDtypeStruct((HEADS, rows // tr, HEAD_V, tr), BF16)],
        compiler_params=_params(1, 48),
        name="kv_proj_t",
    )(lat, kr128, wukv, wvt)


ATTN_TILE = 1024


def _softmax_pv_t(s, s_max, vt, carry):
    m, l, acc = carry
    m_new = jnp.maximum(m, s_max)
    p = jnp.exp2(s - m_new)
    a = jnp.exp2(m - m_new)
    l = a * l + jnp.sum(p, axis=0, keepdims=True)
    acc = a * acc + jnp.dot(vt, p.astype(BF16), preferred_element_type=F32)
    return m_new, l, acc


def _chunk_mask_t(q_rel0, nk, nq):
    kc = lax.broadcasted_iota(jnp.int32, (nk, nq), 0) >> CHUNK_SHIFT
    qc = (q_rel0 + lax.broadcasted_iota(jnp.int32, (nk, nq), 1)) >> CHUNK_SHIFT
    return qc >= kc


def _attn_prompt_kernel(q_ref, k_ref, vt_ref, o_ref, s_even, s_odd, *, rb):
    qi = pl.program_id(1)
    n_rb = ATTN_TILE // rb

    def scores(j, s_ref):
        k = k_ref[0, pl.ds(pl.multiple_of(j * ATTN_TILE, ATTN_TILE), ATTN_TILE), :]
        s = lax.dot_general(k, q_ref[0], (((1,), (1,)), ((), ())), preferred_element_type=F32)
        s_ref[...] = s
        return jnp.max(s, axis=0, keepdims=True)

    def make_step(s_cur, s_next):
        def step(j, carry):
            stats, s_max = carry
            next_max = scores(j + 1, s_next)
            vt = vt_ref[0, j]
            stats = tuple(_softmax_pv_t(s_cur[:, r * rb:(r + 1) * rb], s_max[:, r * rb:(r + 1) * rb], vt, stats[r])
                          for r in range(n_rb))
            return stats, next_max
        return step

    step_even, step_odd = make_step(s_even, s_odd), make_step(s_odd, s_even)

    def finish(s_ref, stats):
        for r in range(n_rb):
            nk = (r + 1) * rb
            s_r = jnp.where(_chunk_mask_t(r * rb, nk, rb), s_ref[:nk, r * rb:(r + 1) * rb], NEG)
            _, l, acc = _softmax_pv_t(s_r, jnp.max(s_r, axis=0, keepdims=True), vt_ref[0, qi, :, :nk], stats[r])
            o_ref[r * rb:(r + 1) * rb, :] = (acc / l).T.astype(o_ref.dtype)

    init = tuple((jnp.full((1, rb), NEG, F32), jnp.zeros((1, rb), F32), jnp.zeros((HEAD_V, rb), F32))
                 for _ in range(n_rb))
    carry = (init, scores(0, s_even))
    carry = lax.fori_loop(0, qi >> 1, lambda p, c: step_odd(2 * p + 1, step_even(2 * p, c)), carry)
    odd = (qi & 1) == 1
    stats, _ = lax.cond(odd, lambda: step_even(qi - 1, carry), lambda: carry)
    lax.cond(odd, lambda: finish(s_odd, stats), lambda: finish(s_even, stats))


def _attn_prompt(q, k, vt):
    return pl.pallas_call(
        functools.partial(_attn_prompt_kernel, rb=512),
        grid=(HEADS, SEQ // ATTN_TILE),
        in_specs=[pl.BlockSpec((1, ATTN_TILE, HEAD_PAD), lambda h, i: (h, i, 0)),
                  pl.BlockSpec((1, SEQ, HEAD_PAD), lambda h, i: (h, 0, 0)),
                  pl.BlockSpec((1, SEQ // ATTN_TILE, HEAD_V, ATTN_TILE), lambda h, i: (h, 0, 0, 0))],
        out_specs=pl.BlockSpec((ATTN_TILE, HEAD_V), lambda h, i: (i, h)),
        out_shape=jax.ShapeDtypeStruct((M_ROWS, HEADS * HEAD_V), BF16),
        scratch_shapes=[pltpu.VMEM((ATTN_TILE, ATTN_TILE), F32), pltpu.VMEM((ATTN_TILE, ATTN_TILE), F32)],
        compiler_params=_params(2, 48),
        name="attn_prompt",
    )(q, k, vt)


def _attn_sample_kernel(q_ref, lat_ref, kr_ref, w_ref, o_prev_ref, o_ref, qa_scr):
    del o_prev_ref
    nt = (((1,), (1,)), ((), ()))
    lat = lat_ref[0].astype(BF16)
    keys = jnp.concatenate([lat, kr_ref[0].astype(BF16)], axis=1)
    for h in range(HEADS):
        rows = slice(h * DEC_SEQ, (h + 1) * DEC_SEQ)
        w_uk = w_ref[:, h * HEAD_PAD:h * HEAD_PAD + QK_NOPE]
        q_lat = lax.dot_general(q_ref[h, :, :LANES], w_uk, nt, preferred_element_type=F32)
        qa_scr[rows, :KV_RANK] = q_lat.astype(BF16)
        qa_scr[rows, KV_RANK:] = q_ref[h, :, LANES:]
    s = lax.dot_general(qa_scr[...], keys, nt, preferred_element_type=F32)
    shape = (HEADS * DEC_SEQ, KV_PAD)
    q_pos = PAST_LEN + (lax.broadcasted_iota(jnp.int32, shape, 0) & (DEC_SEQ - 1))
    k_pos = lax.broadcasted_iota(jnp.int32, shape, 1)
    visible = ((q_pos >> CHUNK_SHIFT) >= (k_pos >> CHUNK_SHIFT)) & (k_pos < PAST_LEN + DEC_SEQ)
    s = jnp.where(visible, s, NEG)
    m = jnp.max(s, axis=-1, keepdims=True)
    p = jnp.exp2(s - m)
    l = jnp.sum(p, axis=-1, keepdims=True)
    o_lat = (jnp.dot(p.astype(BF16), lat, preferred_element_type=F32) / l).astype(BF16)
    for h in range(HEADS):
        w_uv = w_ref[:, h * HEAD_PAD + QK_NOPE:(h + 1) * HEAD_PAD]
        o_h = _dot(o_lat[h * DEC_SEQ:(h + 1) * DEC_SEQ, :], w_uv)
        o_ref[:, h * HEAD_V:(h + 1) * HEAD_V] = o_h.astype(o_ref.dtype)


def _attn_sample(q, lat_all, kr_all, wukv, o_prev):
    q_off = SEQ // DEC_SEQ
    return pl.pallas_call(
        _attn_sample_kernel,
        grid=(DEC_BATCH,),
        in_specs=[pl.BlockSpec((HEADS, DEC_SEQ, HEAD_PAD), lambda b: (0, q_off + b, 0)),
                  pl.BlockSpec((1, KV_PAD, KV_RANK), lambda b: (b, 0, 0)),
                  pl.BlockSpec((1, KV_PAD, LANES), lambda b: (b, 0, 0)),
                  pl.BlockSpec((KV_RANK, HEADS * HEAD_PAD), lambda b: (0, 0)),
                  pl.BlockSpec(memory_space=pl.ANY)],
        out_specs=pl.BlockSpec((DEC_SEQ, HEADS * HEAD_V), lambda b: (q_off + b, 0)),
        out_shape=jax.ShapeDtypeStruct((M_ROWS, HEADS * HEAD_V), BF16),
        scratch_shapes=[pltpu.VMEM((HEADS * DEC_SEQ, KV_RANK + LANES), BF16)],
        input_output_aliases={4: 0},
        compiler_params=_params(1, 48),
        name="attn_sample",
    )(q, lat_all, kr_all, wukv, o_prev)


def _gmlp_kernel(u_ref, v_ref, g_ref, b_ref, w_ref, bs_ref, *rest, chunk, emit_vn):
    if emit_vn:
        _, a_ref, vn_ref = rest
    else:
        (a_ref,) = rest
    v = v_ref[...]
    mu = jnp.mean(v, axis=-1, keepdims=True)
    vc = v - mu
    var = jnp.mean(vc * vc, axis=-1, keepdims=True)
    vn = vc * lax.rsqrt(var + LN_EPS) * g_ref[...] + b_ref[...]
    if emit_vn:
        vn_ref[...] = vn
    vnb = vn.astype(BF16)
    n_chunks = v.shape[0] // chunk
    causal = (lax.broadcasted_iota(jnp.int32, (chunk, GMLP_CHUNK), 0)
              >= lax.broadcasted_iota(jnp.int32, (chunk, GMLP_CHUNK), 1))
    for g in range(GMLP_GROUPS):
        cols = slice(g * GROUP_DIM, (g + 1) * GROUP_DIM)
        w = jnp.where(causal, w_ref[g, :chunk, :], 0.0).astype(BF16)
        rhs = jnp.concatenate([vnb[c * chunk:(c + 1) * chunk, cols] for c in range(n_chunks)], axis=1)
        if chunk < GMLP_CHUNK:
            rhs = jnp.concatenate([rhs, jnp.zeros((GMLP_CHUNK - chunk, rhs.shape[1]), BF16)], axis=0)
        mixed = jnp.dot(w, rhs, preferred_element_type=F32) + bs_ref[g, :chunk, :]
        for c in range(n_chunks):
            rows = slice(c * chunk, (c + 1) * chunk)
            gate = mixed[:, c * GROUP_DIM:(c + 1) * GROUP_DIM]
            a_ref[rows, cols] = (u_ref[rows, cols] * gate).astype(a_ref.dtype)


def _gmlp(proj, ln_g, ln_b, w_s, b_s3, row0, rows, tr, chunk, a_prev=None):
    emit_vn = a_prev is not None
    off = row0 // tr
    vec = pl.BlockSpec((1, GMLP_WIDTH), lambda i: (0, 0))
    in_specs = [pl.BlockSpec((tr, GMLP_WIDTH), lambda i: (i + off, 0)),
                pl.BlockSpec((tr, GMLP_WIDTH), lambda i: (i + off, 1)),
                vec, vec,
                pl.BlockSpec((GMLP_GROUPS, GMLP_CHUNK, GMLP_CHUNK), lambda i: (0, 0, 0)),
                pl.BlockSpec((GMLP_GROUPS, GMLP_CHUNK, 1), lambda i: (0, 0, 0))]
    args = [proj, proj, ln_g.reshape(1, GMLP_WIDTH), ln_b.reshape(1, GMLP_WIDTH), w_s, b_s3]
    out_specs = [pl.BlockSpec((tr, GMLP_WIDTH), lambda i: (i + off, 0))]
    out_shape = [jax.ShapeDtypeStruct((M_ROWS, GMLP_WIDTH), BF16)]
    aliases = {}
    if emit_vn:
        in_specs.append(pl.BlockSpec(memory_space=pl.ANY))
        args.append(a_prev)
        aliases = {len(args) - 1: 0}
        out_specs.append(pl.BlockSpec((tr, GMLP_WIDTH), lambda i: (i, 0)))
        out_shape.append(jax.ShapeDtypeStruct((rows, GMLP_WIDTH), F32))
    return pl.pallas_call(
        functools.partial(_gmlp_kernel, chunk=chunk, emit_vn=emit_vn),
        grid=(rows // tr,),
        in_specs=in_specs,
        out_specs=out_specs,
        out_shape=out_shape,
        input_output_aliases=aliases,
        compiler_params=_params(1, 40),
        name="gmlp_gate_vn" if emit_vn else "gmlp_gate",
    )(*args)


def _swap_halves(w):
    half = w.shape[-1] // 2
    return jnp.concatenate([w[..., half:], w[..., :half]], axis=-1)


def _rope_tables():
    pos = jnp.concatenate([jnp.arange(SEQ, dtype=jnp.int32),
                           jnp.tile(PAST_LEN + jnp.arange(DEC_SEQ, dtype=jnp.int32), DEC_BATCH)])
    inv = 1.0 / (ROPE_THETA ** (jnp.arange(0, QK_ROPE, 2, dtype=F32) / QK_ROPE))
    ang = pos.astype(F32)[:, None] * inv[None, :]
    cos, sin = jnp.cos(ang), jnp.sin(ang)
    zero = jnp.zeros((M_ROWS, LANES - QK_ROPE), F32)
    return (jnp.concatenate([cos, cos, zero], axis=1), jnp.concatenate([-sin, sin, zero], axis=1))


def kernel(x_prompt, x_sample, cache_mla_latent, cache_mla_krope, c_prompt, c_sample, w_ada, b_ada, ffn1_w_gate_up, ffn1_w_down, ln1_g, ln1_b, w_in, gmlp_ln_g, gmlp_ln_b, gmlp_w_s, gmlp_b_s, mla_q_norm_g, mla_w_uq, mla_kv_norm_g, mla_w_ukv, w_out, ln2_g, ln2_b, ffn2_w_gate_up, ffn2_w_down, ln3_g, ln3_b):
    wukv = mla_w_ukv.astype(BF16)
    wvt = mla_w_ukv.reshape(KV_RANK, HEADS, HEAD_PAD)[:, :, QK_NOPE:].transpose(1, 2, 0).astype(BF16)
    w_in_t = w_in.T
    w_kr_t = w_in_t[IN_COLS:]
    wkr = jnp.concatenate([w_kr_t, w_kr_t[QK_ROPE // 2:], w_kr_t[:QK_ROPE // 2]], axis=0)
    uq = mla_w_uq.reshape(Q_RANK, HEADS, QK_NOPE + QK_ROPE)
    wuq = jnp.concatenate([uq, _swap_halves(uq[..., QK_NOPE:])], axis=-1).reshape(Q_RANK, HEADS * HEAD_PAD).astype(BF16)
    b_s3 = gmlp_b_s[:, :, None]
    cc, ss = _rope_tables()

    c16 = jnp.concatenate([c_prompt, c_sample, jnp.zeros((C_ROWS - 1 - DEC_BATCH, D_MODEL), F32)], axis=0)
    n_first = 2
    first = _segment_table(_modulation(c16, w_ada, b_ada, 0, n_first))
    SH1, SC1 = (first, 0), (first, 1)

    x0 = (x_prompt.reshape(SEQ, D_MODEL), x_sample.reshape(S_ROWS, D_MODEL))

    h1 = _modulate(x0, SC1, SH1)
    y1, mod_rest = _ffn_down(*_swiglu_up(h1, ffn1_w_gate_up, ffn1_w_down),
                             adaln=(c16, w_ada, b_ada, n_first, N_MOD - n_first))
    rest = _segment_table(mod_rest)
    G1, SH2, SC2, G2, SH3, SC3, G3 = ((rest, i) for i in range(N_MOD - n_first))
    x1, h2 = _res_ln(x0, y1, G1, 0.5, ln1_g, ln1_b, SC2, SH2)

    proj = _matmul_nt(h2, w_in_t, TM, IN_TILE, "in_proj", n_cols=IN_COLS)
    (a_mix,) = _gmlp(proj, gmlp_ln_g, gmlp_ln_b, gmlp_w_s, b_s3, 0, SEQ, 512, GMLP_CHUNK)
    a_mix, vn_s = _gmlp(proj, gmlp_ln_g, gmlp_ln_b, gmlp_w_s, b_s3, SEQ, S_ROWS, DEC_SEQ, DEC_SEQ, a_prev=a_mix)
    q = _q_proj(proj, mla_q_norm_g, wuq, cc, ss)
    lat_p, kr_p, kr128_p = _lat_krope(proj, h2, wkr, mla_kv_norm_g, cc, ss, 0, SEQ)
    lat_s, kr_s, kr128_s = _lat_krope(proj, h2, wkr, mla_kv_norm_g, cc, ss, SEQ, S_ROWS)

    k_p, vt_p = _kv_proj_t(lat_p, kr128_p, wukv, wvt, ATTN_TILE)
    o_mix = _attn_prompt(q, k_p, vt_p)

    pad_rows = KV_PAD - PAST_LEN - DEC_SEQ
    lat_all = jnp.concatenate([cache_mla_latent, lat_s.reshape(DEC_BATCH, DEC_SEQ, KV_RANK),
                               jnp.zeros((DEC_BATCH, pad_rows, KV_RANK), F32)], axis=1)
    cache_kr128 = jnp.pad(cache_mla_krope, ((0, 0), (0, 0), (0, LANES - QK_ROPE)))
    kr_all = jnp.concatenate([cache_kr128, kr128_s.reshape(DEC_BATCH, DEC_SEQ, LANES),
                              jnp.zeros((DEC_BATCH, pad_rows, LANES), F32)], axis=1)
    o_mix = _attn_sample(q, lat_all, kr_all, wukv, o_mix)

    mix = _matmul_cat(a_mix, o_mix, w_out, TM, 512, "out_proj")
    x2, h3 = _res_ln((x1,), mix, G2, 1.0, ln2_g, ln2_b, SC3, SH3)

    y3 = _ffn_down(*_swiglu_up(h3, ffn2_w_gate_up, ffn2_w_down))
    (y_p,) = _res_ln((x2,), y3, G3, 0.5, ln3_g, ln3_b, row0=0, rows=SEQ)
    (y_s,) = _res_ln((x2,), y3, G3, 0.5, ln3_g, ln3_b, row0=SEQ, rows=S_ROWS)

    return (y_p.reshape(1, SEQ, D_MODEL),
            y_s.reshape(DEC_BATCH, DEC_SEQ, D_MODEL),
            lat_p.reshape(1, SEQ, KV_RANK),
            kr_p.reshape(1, SEQ, QK_ROPE),
            lat_s.reshape(DEC_BATCH, DEC_SEQ, KV_RANK),
            kr_s.reshape(DEC_BATCH, DEC_SEQ, QK_ROPE),
            vn_s.reshape(DEC_BATCH, DEC_SEQ, GMLP_WIDTH))
```

```python
import functools

import jax
import jax.numpy as jnp
from jax import lax
from jax.experimental import pallas as pl
from jax.experimental.pallas import tpu as pltpu

F32 = jnp.float32
BF16 = jnp.bfloat16

D_MODEL = 4096
SEQ = 8192
DEC_BATCH = 8
DEC_SEQ = 64
PAST_LEN = 2048
CHUNK = 64
CHUNK_SHIFT = 6
GMLP_CHUNK = 128
GROUP_DIM = 128
GMLP_WIDTH = D_MODEL // 2
GMLP_GROUPS = GMLP_WIDTH // GROUP_DIM
HEAD_V = 128
QK_NOPE = 128
QK_ROPE = 64
HEADS = (D_MODEL - GMLP_WIDTH) // HEAD_V
KV_RANK = 512
Q_RANK = D_MODEL // 4
D_FF = 256 * ((8 * D_MODEL // 3 + 255) // 256)
N_MOD = 9
ROPE_THETA = 10000.0
LN_EPS = 1e-5
RMS_EPS = 1e-6
ALPHA = 2.0 ** 0.25
Q_SCALE = (QK_NOPE + QK_ROPE) ** -0.5 * 1.4426950408889634
NEG = -1e30

LANES = 128
HEAD_PAD = 2 * LANES
S_ROWS = DEC_BATCH * DEC_SEQ
M_ROWS = SEQ + S_ROWS
SEG = DEC_SEQ
N_SEG = M_ROWS // SEG
TM = M_ROWS // 8
TR = 4 * SEG
IN_COLS = 2 * GMLP_WIDTH + Q_RANK + KV_RANK
IN_TILE = 512
KV_PAD = 17 * LANES
MIB = 2 ** 20


def _params(n_axes, vmem_mib):
    return pltpu.CompilerParams(dimension_semantics=("arbitrary",) * n_axes,
                                vmem_limit_bytes=vmem_mib * MIB)


C_ROWS = 16


def _adaln_tile(c_ref, w_ref, b_ref):
    c = c_ref[...]
    s = (c * jax.nn.sigmoid(c)).astype(BF16)
    return lax.dot_general(s, w_ref[...], (((1,), (0,)), ((), ())), preferred_element_type=F32) + b_ref[...]


def _adaln_specs(tn, comp0, block_of_step):
    per_comp = D_MODEL // tn

    def col(*idx):
        return comp0 * per_comp + block_of_step(*idx)

    in_specs = [pl.BlockSpec((C_ROWS, D_MODEL), lambda *idx: (0, 0)),
                pl.BlockSpec((D_MODEL, tn), lambda *idx: (0, col(*idx))),
                pl.BlockSpec((1, tn), lambda *idx: (0, col(*idx)))]
    out_spec = pl.BlockSpec((1, C_ROWS, tn),
                            lambda *idx: (block_of_step(*idx) // per_comp, 0, block_of_step(*idx) % per_comp))
    return in_specs, out_spec


def _mod_kernel(c_ref, w_ref, b_ref, o_ref):
    o_ref[0] = _adaln_tile(c_ref, w_ref, b_ref)


def _modulation(c16, w_ada, b_ada, comp0, n_comp):
    tn = 512
    in_specs, out_spec = _adaln_specs(tn, comp0, lambda j: j)
    return pl.pallas_call(
        _mod_kernel,
        grid=(n_comp * D_MODEL // tn,),
        in_specs=in_specs,
        out_specs=out_spec,
        out_shape=jax.ShapeDtypeStruct((n_comp, C_ROWS, D_MODEL), F32),
        compiler_params=_params(1, 40),
        name="adaln_mod",
    )(c16, w_ada, b_ada.reshape(1, -1))


def _segment_table(mod):
    n = mod.shape[0]
    return jnp.concatenate([jnp.broadcast_to(mod[:, 0:1], (n, SEQ // SEG, D_MODEL)), mod[:, 1:1 + DEC_BATCH]], axis=1)


def _seg_operand(entry, tile_rows, width, index_map):
    table, comp = entry
    view = table.reshape(table.shape[0], M_ROWS // tile_rows, tile_rows // SEG, D_MODEL)
    spec = pl.BlockSpec((1, 1, tile_rows // SEG, width), lambda *idx: (comp,) + tuple(index_map(*idx)))
    return view, spec


def _split_x_specs(tile_rows, width, col):
    n_prompt = SEQ // tile_rows
    return [pl.BlockSpec((tile_rows, width), lambda *idx: (jnp.minimum(idx[0], n_prompt - 1), col(*idx))),
            pl.BlockSpec((tile_rows, width), lambda *idx: (jnp.maximum(idx[0] - n_prompt, 0), col(*idx)))]


def _on_owner(x_refs, tile_rows, body):
    if len(x_refs) == 1:
        body(x_refs[0])
    else:
        i = pl.program_id(0)
        n_prompt = SEQ // tile_rows
        pl.when(i < n_prompt)(lambda: body(x_refs[0]))
        pl.when(i >= n_prompt)(lambda: body(x_refs[1]))


def _modulate_kernel(xp_ref, xs_ref, sc_ref, sh_ref, h_ref):
    def body(x_ref):
        for s in range(TR // SEG):
            rows = slice(s * SEG, (s + 1) * SEG)
            h = x_ref[rows, :] * (1.0 + sc_ref[0, 0, s:s + 1, :]) + sh_ref[0, 0, s:s + 1, :]
            h_ref[rows, :] = h.astype(h_ref.dtype)

    _on_owner((xp_ref, xs_ref), TR, body)


def _modulate(xs, sc, sh):
    sc_arr, sc_spec = _seg_operand(sc, TR, D_MODEL, lambda i: (i, 0, 0))
    sh_arr, sh_spec = _seg_operand(sh, TR, D_MODEL, lambda i: (i, 0, 0))
    return pl.pallas_call(
        _modulate_kernel,
        grid=(M_ROWS // TR,),
        in_specs=_split_x_specs(TR, D_MODEL, lambda i: 0) + [sc_spec, sh_spec],
        out_specs=pl.BlockSpec((TR, D_MODEL), lambda i: (i, 0)),
        out_shape=jax.ShapeDtypeStruct((M_ROWS, D_MODEL), BF16),
        compiler_params=_params(1, 32),
        name="modulate",
    )(*xs, sc_arr, sh_arr)


def _gated_residual(x_ref, acc, gate_ref, gate_scale, z_ref):
    for s in range(acc.shape[0] // SEG):
        rows = slice(s * SEG, (s + 1) * SEG)
        z_ref[rows, :] = ALPHA * x_ref[rows, :] + (gate_scale * gate_ref[0, 0, s:s + 1, :]) * acc[rows, :]


def _ln_kernel(z_ref, g_ref, b_ref, *rest, with_h):
    if with_h:
        sc_ref, sh_ref, xo_ref, h_ref = rest
    else:
        (xo_ref,) = rest
    for s in range(TR // SEG):
        rows = slice(s * SEG, (s + 1) * SEG)
        z = z_ref[rows, :]
        mu = jnp.mean(z, axis=-1, keepdims=True)
        zc = z - mu
        var = jnp.mean(zc * zc, axis=-1, keepdims=True)
        xn = zc * lax.rsqrt(var + LN_EPS) * g_ref[...] + b_ref[...]
        xo_ref[rows, :] = xn
        if with_h:
            h = xn * (1.0 + sc_ref[0, 0, s:s + 1, :]) + sh_ref[0, 0, s:s + 1, :]
            h_ref[rows, :] = h.astype(h_ref.dtype)


def _layernorm(z, ln_g, ln_b, sc=None, sh=None, *, row0=0, rows=M_ROWS):
    with_h = sc is not None
    off = row0 // TR
    row_out = pl.BlockSpec((TR, D_MODEL), lambda i: (i, 0))
    vec = pl.BlockSpec((1, D_MODEL), lambda i: (0, 0))
    in_specs = [pl.BlockSpec((TR, D_MODEL), lambda i: (i + off, 0)), vec, vec]
    args = [z, ln_g.reshape(1, D_MODEL), ln_b.reshape(1, D_MODEL)]
    out_specs = [row_out]
    out_shape = [jax.ShapeDtypeStruct((rows, D_MODEL), F32)]
    if with_h:
        for entry in (sc, sh):
            arr, spec = _seg_operand(entry, TR, D_MODEL, lambda i: (i + off, 0, 0))
            in_specs.append(spec)
            args.append(arr)
        out_specs.append(row_out)
        out_shape.append(jax.ShapeDtypeStruct((rows, D_MODEL), BF16))
    return pl.pallas_call(
        functools.partial(_ln_kernel, with_h=with_h),
        grid=(rows // TR,),
        in_specs=in_specs,
        out_specs=out_specs,
        out_shape=out_shape,
        compiler_params=_params(1, 40),
        name="ln_mod" if with_h else "ln",
    )(*args)


def _dot(a, b):
    return lax.dot_general(a, b, (((1,), (0,)), ((), ())), preferred_element_type=F32)


def _gateup_kernel(h_ref, wg_ref, wu_ref, wd_ref, o_ref, wd_out_ref):
    h = h_ref[...]
    g = _dot(h, wg_ref[...])
    u = _dot(h, wu_ref[...])
    o_ref[...] = (g * jax.nn.sigmoid(g) * u).astype(o_ref.dtype)

    @pl.when(pl.program_id(0) == 0)
    def _():
        wd_out_ref[...] = wd_ref[...].astype(wd_out_ref.dtype)


def _swiglu_up(h, wgu, w_down):
    tm = 2 * TM
    tn = 256
    nj = D_FF // tn
    slab = pl.BlockSpec((tn, D_MODEL), lambda i, j: (jnp.where(i == 0, j, nj - 1), 0))
    return pl.pallas_call(
        _gateup_kernel,
        grid=(M_ROWS // tm, nj),
        in_specs=[pl.BlockSpec((tm, D_MODEL), lambda i, j: (i, 0), pipeline_mode=pl.Buffered(1)),
                  pl.BlockSpec((D_MODEL, tn), lambda i, j: (0, j)),
                  pl.BlockSpec((D_MODEL, tn), lambda i, j: (0, j + nj)),
                  slab],
        out_specs=[pl.BlockSpec((tm, tn), lambda i, j: (i, j)), slab],
        out_shape=[jax.ShapeDtypeStruct((M_ROWS, D_FF), BF16),
                   jax.ShapeDtypeStruct((D_FF, D_MODEL), BF16)],
        compiler_params=_params(2, 56),
        name="swiglu_up",
    )(h, wgu, wgu, w_down)


def _mm_nt_kernel(a_ref, bt_ref, o_ref):
    o_ref[...] = lax.dot_general(a_ref[...], bt_ref[...], (((1,), (1,)), ((), ())), preferred_element_type=F32)


def _matmul_nt(a, bt, tm, tn, name, n_cols):
    m, k = a.shape
    return pl.pallas_call(
        _mm_nt_kernel,
        grid=(m // tm, n_cols // tn),
        in_specs=[pl.BlockSpec((tm, k), lambda i, j: (i, 0)),
                  pl.BlockSpec((tn, k), lambda i, j: (j, 0))],
        out_specs=pl.BlockSpec((tm, tn), lambda i, j: (i, j)),
        out_shape=jax.ShapeDtypeStruct((m, n_cols), F32),
        compiler_params=_params(2, 48),
        name=name,
    )(a, bt)


def _down_kernel(*refs, n_x, tile_rows, gate_scale, with_adaln):
    a_ref, b_ref = refs[:2]
    x_refs = refs[2:2 + n_x]
    gate_ref = refs[2 + n_x]
    if with_adaln:
        c_ref, w_ref, bias_ref, z_ref, mod_ref = refs[3 + n_x:]
    else:
        (z_ref,) = refs[3 + n_x:]
    acc = _dot(a_ref[...], b_ref[...])
    _on_owner(x_refs, tile_rows, lambda x_ref: _gated_residual(x_ref, acc, gate_ref, gate_scale, z_ref))
    if with_adaln:
        mod_ref[0] = _adaln_tile(c_ref, w_ref, bias_ref)


def _ffn_down(a, w_down_bf16, xs, gate, adaln=None):
    tm, tn = 8 * SEG, 512
    m, k = a.shape
    n = w_down_bf16.shape[1]
    grid = (m // tm, n // tn)
    gate_arr, gate_spec = _seg_operand(gate, tm, tn, lambda i, j: (i, 0, j))
    x_specs = (_split_x_specs(tm, tn, lambda i, j: j) if len(xs) == 2
               else [pl.BlockSpec((tm, tn), lambda i, j: (i, j))])
    in_specs = [pl.BlockSpec((tm, k), lambda i, j: (i, 0)),
                pl.BlockSpec((k, tn), lambda i, j: (0, j))] + x_specs + [gate_spec]
    args = [a, w_down_bf16, *xs, gate_arr]
    out_specs = [pl.BlockSpec((tm, tn), lambda i, j: (i, j))]
    out_shape = [jax.ShapeDtypeStruct((m, n), F32)]
    if adaln is not None:
        c16, w_ada, b_ada, comp0, n_comp = adaln
        side_tn = 256
        n_side = n_comp * D_MODEL // side_tn
        assert n_side <= grid[0] * grid[1]
        side_in, side_out = _adaln_specs(side_tn, comp0, lambda i, j: jnp.minimum(i * grid[1] + j, n_side - 1))
        in_specs += side_in
        args += [c16, w_ada, b_ada.reshape(1, -1)]
        out_specs.append(side_out)
        out_shape.append(jax.ShapeDtypeStruct((n_comp, C_ROWS, D_MODEL), F32))
    return pl.pallas_call(
        functools.partial(_down_kernel, n_x=len(xs), tile_rows=tm, gate_scale=0.5, with_adaln=adaln is not None),
        grid=grid,
        in_specs=in_specs,
        out_specs=out_specs,
        out_shape=out_shape,
        compiler_params=_params(2, 60),
        name="ffn_down" if adaln is None else "ffn_down_adaln",
    )(*args)


def _out_proj_kernel(a1_ref, a2_ref, b_ref, x_ref, gate_ref, z_ref):
    k1 = a1_ref.shape[1]
    acc = _dot(a1_ref[...], b_ref[:k1, :])
    acc += _dot(a2_ref[...], b_ref[k1:, :])
    _gated_residual(x_ref, acc, gate_ref, 1.0, z_ref)


def _out_proj(a1, a2, b, x, gate):
    tm, tn = TM, 512
    m, k1 = a1.shape
    k2 = a2.shape[1]
    n = b.shape[1]
    gate_arr, gate_spec = _seg_operand(gate, tm, tn, lambda i, j: (i, 0, j))
    tile = pl.BlockSpec((tm, tn), lambda i, j: (i, j))
    return pl.pallas_call(
        _out_proj_kernel,
        grid=(m // tm, n // tn),
        in_specs=[pl.BlockSpec((tm, k1), lambda i, j: (i, 0)),
                  pl.BlockSpec((tm, k2), lambda i, j: (i, 0)),
                  pl.BlockSpec((k1 + k2, tn), lambda i, j: (0, j)),
                  tile, gate_spec],
        out_specs=tile,
        out_shape=jax.ShapeDtypeStruct((m, n), F32),
        compiler_params=_params(2, 56),
        name="out_proj",
    )(a1, a2, b, x, gate_arr)


def _rope128(blk, cc, ss):
    return blk * cc + pltpu.roll(blk, QK_ROPE, 1) * ss


def _q_kernel(cq_ref, g_ref, w_ref, cc_ref, ss_ref, o_ref):
    x = cq_ref[...]
    r = lax.rsqrt(jnp.mean(x * x, axis=-1, keepdims=True) + RMS_EPS)
    a = (x * r * g_ref[...]).astype(BF16)
    cc = cc_ref[...]
    ss = ss_ref[...]
    for h in range(HEADS):
        acc = _dot(a, w_ref[:, h * HEAD_PAD:(h + 1) * HEAD_PAD])
        o_ref[h, :, :LANES] = (acc[:, :LANES] * Q_SCALE).astype(o_ref.dtype)
        o_ref[h, :, LANES:] = (_rope128(acc[:, LANES:], cc, ss) * Q_SCALE).astype(o_ref.dtype)


def _q_proj(proj, q_norm_g, wuq, cc, ss):
    tm = TM // 2
    cq_block = (2 * GMLP_WIDTH) // Q_RANK
    return pl.pallas_call(
        _q_kernel,
        grid=(M_ROWS // tm,),
        in_specs=[pl.BlockSpec((tm, Q_RANK), lambda i: (i, cq_block)),
                  pl.BlockSpec((1, Q_RANK), lambda i: (0, 0)),
                  pl.BlockSpec((Q_RANK, HEADS * HEAD_PAD), lambda i: (0, 0)),
                  pl.BlockSpec((tm, LANES), lambda i: (i, 0)),
                  pl.BlockSpec((tm, LANES), lambda i: (i, 0))],
        out_specs=pl.BlockSpec((HEADS, tm, HEAD_PAD), lambda i: (0, i, 0)),
        out_shape=jax.ShapeDtypeStruct((HEADS, M_ROWS, HEAD_PAD), BF16),
        compiler_params=_params(1, 40),
        name="q_proj",
    )(proj, q_norm_g.reshape(1, Q_RANK), wuq, cc, ss)


def _latkr_kernel(ckv_ref, h_ref, wkr_ref, g_ref, cc_ref, ss_ref, lat_ref, kr_out_ref, kr128_ref):
    x = ckv_ref[...]
    r = lax.rsqrt(jnp.mean(x * x, axis=-1, keepdims=True) + RMS_EPS)
    lat_ref[...] = x * r * g_ref[...]
    blk = lax.dot_general(h_ref[...], wkr_ref[...], (((1,), (1,)), ((), ())), preferred_element_type=F32)
    rot = _rope128(blk, cc_ref[...], ss_ref[...])
    kr_out_ref[...] = rot[:, :QK_ROPE]
    kr128_ref[...] = rot


def _lat_krope(proj, h, wkr, kv_norm_g, cc, ss, row0, rows):
    tr = 512
    off = row0 // tr
    ckv_block = (2 * GMLP_WIDTH + Q_RANK) // KV_RANK
    return pl.pallas_call(
        _latkr_kernel,
        grid=(rows // tr,),
        in_specs=[pl.BlockSpec((tr, KV_RANK), lambda i: (i + off, ckv_block)),
                  pl.BlockSpec((tr, D_MODEL), lambda i: (i + off, 0)),
                  pl.BlockSpec((LANES, D_MODEL), lambda i: (0, 0)),
                  pl.BlockSpec((1, KV_RANK), lambda i: (0, 0)),
                  pl.BlockSpec((tr, LANES), lambda i: (i + off, 0)),
                  pl.BlockSpec((tr, LANES), lambda i: (i + off, 0))],
        out_specs=[pl.BlockSpec((tr, KV_RANK), lambda i: (i, 0)),
                   pl.BlockSpec((tr, QK_ROPE), lambda i: (i, 0)),
                   pl.BlockSpec((tr, LANES), lambda i: (i, 0))],
        out_shape=[jax.ShapeDtypeStruct((rows, KV_RANK), F32),
                   jax.ShapeDtypeStruct((rows, QK_ROPE), F32),
                   jax.ShapeDtypeStruct((rows, LANES), F32)],
        compiler_params=_params(1, 32),
        name="lat_krope",
    )(proj, h, wkr, kv_norm_g.reshape(1, KV_RANK), cc, ss)


def _kv_t_kernel(lat_ref, kr_ref, w_ref, wvt_ref, k_ref, vt_ref):
    a = lat_ref[...].astype(BF16)
    kr = kr_ref[...].astype(BF16)
    for h in range(HEADS):
        kn = _dot(a, w_ref[:, h * HEAD_PAD:h * HEAD_PAD + QK_NOPE])
        k_ref[h, :, :LANES] = kn.astype(BF16)
        k_ref[h, :, LANES:] = kr
        vt = lax.dot_general(wvt_ref[h], a, (((1,), (1,)), ((), ())), preferred_element_type=F32)
        vt_ref[h, 0] = vt.astype(BF16)


def _kv_proj_t(lat, kr128, wukv, wvt, tr):
    rows = lat.shape[0]
    return pl.pallas_call(
        _kv_t_kernel,
        grid=(rows // tr,),
        in_specs=[pl.BlockSpec((tr, KV_RANK), lambda i: (i, 0)),
                  pl.BlockSpec((tr, LANES), lambda i: (i, 0)),
                  pl.BlockSpec((KV_RANK, HEADS * HEAD_PAD), lambda i: (0, 0)),
                  pl.BlockSpec((HEADS, HEAD_V, KV_RANK), lambda i: (0, 0, 0))],
        out_specs=[pl.BlockSpec((HEADS, tr, HEAD_PAD), lambda i: (0, i, 0)),
                   pl.BlockSpec((HEADS, 1, HEAD_V, tr), lambda i: (0, i, 0, 0))],
        out_shape=[jax.ShapeDtypeStruct((HEADS, rows, HEAD_PAD), BF16),
                   jax.ShapeDtypeStruct((HEADS, rows // tr, HEAD_V, tr), BF16)],
        compiler_params=_params(1, 48),
        name="kv_proj_t",
    )(lat, kr128, wukv, wvt)


ATTN_TILE = 1024


def _softmax_pv_t(s, s_max, vt, carry):
    m, l, acc = carry
    m_new = jnp.maximum(m, s_max)
    p = jnp.exp2(s - m_new)
    a = jnp.exp2(m - m_new)
    l = a * l + jnp.sum(p, axis=0, keepdims=True)
    acc = a * acc + jnp.dot(vt, p.astype(BF16), preferred_element_type=F32)
    return m_new, l, acc


def _chunk_mask_t(q_rel0, nk, nq):
    kc = lax.broadcasted_iota(jnp.int32, (nk, nq), 0) >> CHUNK_SHIFT
    qc = (q_rel0 + lax.broadcasted_iota(jnp.int32, (nk, nq), 1)) >> CHUNK_SHIFT
    return qc >= kc


def _attn_prompt_kernel(q_ref, k_ref, vt_ref, o_ref, s_even, s_odd, *, rb):
    qi = pl.program_id(1)
    n_rb = ATTN_TILE // rb

    def scores(j, s_ref):
        k = k_ref[0, pl.ds(pl.multiple_of(j * ATTN_TILE, ATTN_TILE), ATTN_TILE), :]
        s = lax.dot_general(k, q_ref[0], (((1,), (1,)), ((), ())), preferred_element_type=F32)
        s_ref[...] = s
        return jnp.max(s, axis=0, keepdims=True)

    def make_step(s_cur, s_next):
        def step(j, carry):
            stats, s_max = carry
            next_max = scores(j + 1, s_next)
            vt = vt_ref[0, j]
            stats = tuple(_softmax_pv_t(s_cur[:, r * rb:(r + 1) * rb], s_max[:, r * rb:(r + 1) * rb], vt, stats[r])
                          for r in range(n_rb))
            return stats, next_max
        return step

    step_even, step_odd = make_step(s_even, s_odd), make_step(s_odd, s_even)

    def finish(s_ref, stats):
        for r in range(n_rb):
            nk = (r + 1) * rb
            s_r = jnp.where(_chunk_mask_t(r * rb, nk, rb), s_ref[:nk, r * rb:(r + 1) * rb], NEG)
            _, l, acc = _softmax_pv_t(s_r, jnp.max(s_r, axis=0, keepdims=True), vt_ref[0, qi, :, :nk], stats[r])
            o_ref[r * rb:(r + 1) * rb, :] = (acc / l).T.astype(o_ref.dtype)

    init = tuple((jnp.full((1, rb), NEG, F32), jnp.zeros((1, rb), F32), jnp.zeros((HEAD_V, rb), F32))
                 for _ in range(n_rb))
    carry = (init, scores(0, s_even))
    carry = lax.fori_loop(0, qi >> 1, lambda p, c: step_odd(2 * p + 1, step_even(2 * p, c)), carry)
    odd = (qi & 1) == 1
    stats, _ = lax.cond(odd, lambda: step_even(qi - 1, carry), lambda: carry)
    lax.cond(odd, lambda: finish(s_odd, stats), lambda: finish(s_even, stats))


def _attn_prompt(q, k, vt):
    return pl.pallas_call(
        functools.partial(_attn_prompt_kernel, rb=512),
        grid=(HEADS, SEQ // ATTN_TILE),
        in_specs=[pl.BlockSpec((1, ATTN_TILE, HEAD_PAD), lambda h, i: (h, i, 0)),
                  pl.BlockSpec((1, SEQ, HEAD_PAD), lambda h, i: (h, 0, 0)),
                  pl.BlockSpec((1, SEQ // ATTN_TILE, HEAD_V, ATTN_TILE), lambda h, i: (h, 0, 0, 0))],
        out_specs=pl.BlockSpec((ATTN_TILE, HEAD_V), lambda h, i: (i, h)),
        out_shape=jax.ShapeDtypeStruct((M_ROWS, HEADS * HEAD_V), BF16),
        scratch_shapes=[pltpu.VMEM((ATTN_TILE, ATTN_TILE), F32), pltpu.VMEM((ATTN_TILE, ATTN_TILE), F32)],
        compiler_params=_params(2, 48),
        name="attn_prompt",
    )(q, k, vt)


def _attn_sample_kernel(q_ref, lat_ref, kr_ref, w_ref, o_prev_ref, o_ref, qa_scr):
    del o_prev_ref
    nt = (((1,), (1,)), ((), ()))
    lat = lat_ref[0].astype(BF16)
    keys = jnp.concatenate([lat, kr_ref[0].astype(BF16)], axis=1)
    for h in range(HEADS):
        rows = slice(h * DEC_SEQ, (h + 1) * DEC_SEQ)
        w_uk = w_ref[:, h * HEAD_PAD:h * HEAD_PAD + QK_NOPE]
        q_lat = lax.dot_general(q_ref[h, :, :LANES], w_uk, nt, preferred_element_type=F32)
        qa_scr[rows, :KV_RANK] = q_lat.astype(BF16)
        qa_scr[rows, KV_RANK:] = q_ref[h, :, LANES:]
    s = lax.dot_general(qa_scr[...], keys, nt, preferred_element_type=F32)
    shape = (HEADS * DEC_SEQ, KV_PAD)
    q_pos = PAST_LEN + (lax.broadcasted_iota(jnp.int32, shape, 0) & (DEC_SEQ - 1))
    k_pos = lax.broadcasted_iota(jnp.int32, shape, 1)
    visible = ((q_pos >> CHUNK_SHIFT) >= (k_pos >> CHUNK_SHIFT)) & (k_pos < PAST_LEN + DEC_SEQ)
    s = jnp.where(visible, s, NEG)
    m = jnp.max(s, axis=-1, keepdims=True)
    p = jnp.exp2(s - m)
    l = jnp.sum(p, axis=-1, keepdims=True)
    o_lat = (jnp.dot(p.astype(BF16), lat, preferred_element_type=F32) / l).astype(BF16)
    for h in range(HEADS):
        w_uv = w_ref[:, h * HEAD_PAD + QK_NOPE:(h + 1) * HEAD_PAD]
        o_h = _dot(o_lat[h * DEC_SEQ:(h + 1) * DEC_SEQ, :], w_uv)
        o_ref[:, h * HEAD_V:(h + 1) * HEAD_V] = o_h.astype(o_ref.dtype)


def _attn_sample(q, lat_all, kr_all, wukv, o_prev):
    q_off = SEQ // DEC_SEQ
    return pl.pallas_call(
        _attn_sample_kernel,
        grid=(DEC_BATCH,),
        in_specs=[pl.BlockSpec((HEADS, DEC_SEQ, HEAD_PAD), lambda b: (0, q_off + b, 0)),
                  pl.BlockSpec((1, KV_PAD, KV_RANK), lambda b: (b, 0, 0)),
                  pl.BlockSpec((1, KV_PAD, LANES), lambda b: (b, 0, 0)),
                  pl.BlockSpec((KV_RANK, HEADS * HEAD_PAD), lambda b: (0, 0)),
                  pl.BlockSpec(memory_space=pl.ANY)],
        out_specs=pl.BlockSpec((DEC_SEQ, HEADS * HEAD_V), lambda b: (q_off + b, 0)),
        out_shape=jax.ShapeDtypeStruct((M_ROWS, HEADS * HEAD_V), BF16),
        scratch_shapes=[pltpu.VMEM((HEADS * DEC_SEQ, KV_RANK + LANES), BF16)],
        input_output_aliases={4: 0},
        compiler_params=_params(1, 48),
        name="attn_sample",
    )(q, lat_all, kr_all, wukv, o_prev)


def _gmlp_kernel(u_ref, v_ref, g_ref, b_ref, w_ref, bs_ref, *rest, chunk, emit_vn):
    if emit_vn:
        _, a_ref, vn_ref = rest
    else:
        (a_ref,) = rest
    v = v_ref[...]
    mu = jnp.mean(v, axis=-1, keepdims=True)
    vc = v - mu
    var = jnp.mean(vc * vc, axis=-1, keepdims=True)
    vn = vc * lax.rsqrt(var + LN_EPS) * g_ref[...] + b_ref[...]
    if emit_vn:
        vn_ref[...] = vn
    vnb = vn.astype(BF16)
    n_chunks = v.shape[0] // chunk
    causal = (lax.broadcasted_iota(jnp.int32, (chunk, GMLP_CHUNK), 0)
              >= lax.broadcasted_iota(jnp.int32, (chunk, GMLP_CHUNK), 1))
    for g in range(GMLP_GROUPS):
        cols = slice(g * GROUP_DIM, (g + 1) * GROUP_DIM)
        w = jnp.where(causal, w_ref[g, :chunk, :], 0.0).astype(BF16)
        rhs = jnp.concatenate([vnb[c * chunk:(c + 1) * chunk, cols] for c in range(n_chunks)], axis=1)
        if chunk < GMLP_CHUNK:
            rhs = jnp.concatenate([rhs, jnp.zeros((GMLP_CHUNK - chunk, rhs.shape[1]), BF16)], axis=0)
        mixed = jnp.dot(w, rhs, preferred_element_type=F32) + bs_ref[g, :chunk, :]
        for c in range(n_chunks):
            rows = slice(c * chunk, (c + 1) * chunk)
            gate = mixed[:, c * GROUP_DIM:(c + 1) * GROUP_DIM]
            a_ref[rows, cols] = (u_ref[rows, cols] * gate).astype(a_ref.dtype)


def _gmlp(proj, ln_g, ln_b, w_s, b_s3, row0, rows, tr, chunk, a_prev=None):
    emit_vn = a_prev is not None
    off = row0 // tr
    vec = pl.BlockSpec((1, GMLP_WIDTH), lambda i: (0, 0))
    in_specs = [pl.BlockSpec((tr, GMLP_WIDTH), lambda i: (i + off, 0)),
                pl.BlockSpec((tr, GMLP_WIDTH), lambda i: (i + off, 1)),
                vec, vec,
                pl.BlockSpec((GMLP_GROUPS, GMLP_CHUNK, GMLP_CHUNK), lambda i: (0, 0, 0)),
                pl.BlockSpec((GMLP_GROUPS, GMLP_CHUNK, 1), lambda i: (0, 0, 0))]
    args = [proj, proj, ln_g.reshape(1, GMLP_WIDTH), ln_b.reshape(1, GMLP_WIDTH), w_s, b_s3]
    out_specs = [pl.BlockSpec((tr, GMLP_WIDTH), lambda i: (i + off, 0))]
    out_shape = [jax.ShapeDtypeStruct((M_ROWS, GMLP_WIDTH), BF16)]
    aliases = {}
    if emit_vn:
        in_specs.append(pl.BlockSpec(memory_space=pl.ANY))
        args.append(a_prev)
        aliases = {len(args) - 1: 0}
        out_specs.append(pl.BlockSpec((tr, GMLP_WIDTH), lambda i: (i, 0)))
        out_shape.append(jax.ShapeDtypeStruct((rows, GMLP_WIDTH), F32))
    return pl.pallas_call(
        functools.partial(_gmlp_kernel, chunk=chunk, emit_vn=emit_vn),
        grid=(rows // tr,),
        in_specs=in_specs,
        out_specs=out_specs,
        out_shape=out_shape,
        input_output_aliases=aliases,
        compiler_params=_params(1, 40),
        name="gmlp_gate_vn" if emit_vn else "gmlp_gate",
    )(*args)


def _swap_halves(w):
    half = w.shape[-1] // 2
    return jnp.concatenate([w[..., half:], w[..., :half]], axis=-1)


def _rope_tables():
    pos = jnp.concatenate([jnp.arange(SEQ, dtype=jnp.int32),
                           jnp.tile(PAST_LEN + jnp.arange(DEC_SEQ, dtype=jnp.int32), DEC_BATCH)])
    inv = 1.0 / (ROPE_THETA ** (jnp.arange(0, QK_ROPE, 2, dtype=F32) / QK_ROPE))
    ang = pos.astype(F32)[:, None] * inv[None, :]
    cos, sin = jnp.cos(ang), jnp.sin(ang)
    zero = jnp.zeros((M_ROWS, LANES - QK_ROPE), F32)
    return (jnp.concatenate([cos, cos, zero], axis=1), jnp.concatenate([-sin, sin, zero], axis=1))


def kernel(x_prompt, x_sample, cache_mla_latent, cache_mla_krope, c_prompt, c_sample, w_ada, b_ada, ffn1_w_gate_up, ffn1_w_down, ln1_g, ln1_b, w_in, gmlp_ln_g, gmlp_ln_b, gmlp_w_s, gmlp_b_s, mla_q_norm_g, mla_w_uq, mla_kv_norm_g, mla_w_ukv, w_out, ln2_g, ln2_b, ffn2_w_gate_up, ffn2_w_down, ln3_g, ln3_b):
    wukv = mla_w_ukv.astype(BF16)
    wvt = mla_w_ukv.reshape(KV_RANK, HEADS, HEAD_PAD)[:, :, QK_NOPE:].transpose(1, 2, 0).astype(BF16)
    w_in_t = w_in.T
    w_kr_t = w_in_t[IN_COLS:]
    wkr = jnp.concatenate([w_kr_t, w_kr_t[QK_ROPE // 2:], w_kr_t[:QK_ROPE // 2]], axis=0)
    uq = mla_w_uq.reshape(Q_RANK, HEADS, QK_NOPE + QK_ROPE)
    wuq = jnp.concatenate([uq, _swap_halves(uq[..., QK_NOPE:])], axis=-1).reshape(Q_RANK, HEADS * HEAD_PAD).astype(BF16)
    b_s3 = gmlp_b_s[:, :, None]
    cc, ss = _rope_tables()

    c16 = jnp.concatenate([c_prompt, c_sample, jnp.zeros((C_ROWS - 1 - DEC_BATCH, D_MODEL), F32)], axis=0)
    n_first = 3
    first = _segment_table(_modulation(c16, w_ada, b_ada, 0, n_first))
    SH1, SC1, G1 = ((first, i) for i in range(n_first))

    x0 = (x_prompt.reshape(SEQ, D_MODEL), x_sample.reshape(S_ROWS, D_MODEL))

    h1 = _modulate(x0, SC1, SH1)
    z1, mod_rest = _ffn_down(*_swiglu_up(h1, ffn1_w_gate_up, ffn1_w_down), x0, G1,
                             adaln=(c16, w_ada, b_ada, n_first, N_MOD - n_first))
    rest = _segment_table(mod_rest)
    SH2, SC2, G2, SH3, SC3, G3 = ((rest, i) for i in range(N_MOD - n_first))
    x1, h2 = _layernorm(z1, ln1_g, ln1_b, SC2, SH2)

    proj = _matmul_nt(h2, w_in_t, TM, IN_TILE, "in_proj", n_cols=IN_COLS)
    (a_mix,) = _gmlp(proj, gmlp_ln_g, gmlp_ln_b, gmlp_w_s, b_s3, 0, SEQ, 512, GMLP_CHUNK)
    a_mix, vn_s = _gmlp(proj, gmlp_ln_g, gmlp_ln_b, gmlp_w_s, b_s3, SEQ, S_ROWS, DEC_SEQ, DEC_SEQ, a_prev=a_mix)
    q = _q_proj(proj, mla_q_norm_g, wuq, cc, ss)
    lat_p, kr_p, kr128_p = _lat_krope(proj, h2, wkr, mla_kv_norm_g, cc, ss, 0, SEQ)
    lat_s, kr_s, kr128_s = _lat_krope(proj, h2, wkr, mla_kv_norm_g, cc, ss, SEQ, S_ROWS)

    k_p, vt_p = _kv_proj_t(lat_p, kr128_p, wukv, wvt, ATTN_TILE)
    o_mix = _attn_prompt(q, k_p, vt_p)

    pad_rows = KV_PAD - PAST_LEN - DEC_SEQ
    lat_all = jnp.concatenate([cache_mla_latent, lat_s.reshape(DEC_BATCH, DEC_SEQ, KV_RANK),
                               jnp.zeros((DEC_BATCH, pad_rows, KV_RANK), F32)], axis=1)
    cache_kr128 = jnp.pad(cache_mla_krope, ((0, 0), (0, 0), (0, LANES - QK_ROPE)))
    kr_all = jnp.concatenate([cache_kr128, kr128_s.reshape(DEC_BATCH, DEC_SEQ, LANES),
                              jnp.zeros((DEC_BATCH, pad_rows, LANES), F32)], axis=1)
    o_mix = _attn_sample(q, lat_all, kr_all, wukv, o_mix)

    z2 = _out_proj(a_mix, o_mix, w_out, x1, G2)
    x2, h3 = _layernorm(z2, ln2_g, ln2_b, SC3, SH3)

    (z3,) = _ffn_down(*_swiglu_up(h3, ffn2_w_gate_up, ffn2_w_down), (x2,), G3)
    (y_p,) = _layernorm(z3, ln3_g, ln3_b, row0=0, rows=SEQ)
    (y_s,) = _layernorm(z3, ln3_g, ln3_b, row0=SEQ, rows=S_ROWS)

    return (y_p.reshape(1, SEQ, D_MODEL),
            y_s.reshape(DEC_BATCH, DEC_SEQ, D_MODEL),
            lat_p.reshape(1, SEQ, KV_RANK),
            kr_p.reshape(1, SEQ, QK_ROPE),
            lat_s.reshape(DEC_BATCH, DEC_SEQ, KV_RANK),
            kr_s.reshape(DEC_BATCH, DEC_SEQ, QK_ROPE),
            vn_s.reshape(DEC_BATCH, DEC_SEQ, GMLP_WIDTH))
```

```python
import functools

import jax
import jax.numpy as jnp
from jax import lax
from jax.experimental import pallas as pl
from jax.experimental.pallas import tpu as pltpu

F32 = jnp.float32
BF16 = jnp.bfloat16

D_MODEL = 4096
SEQ = 8192
DEC_BATCH = 8
DEC_SEQ = 64
PAST_LEN = 2048
CHUNK = 64
CHUNK_SHIFT = 6
GMLP_CHUNK = 128
GROUP_DIM = 128
GMLP_WIDTH = D_MODEL // 2
GMLP_GROUPS = GMLP_WIDTH // GROUP_DIM
HEAD_V = 128
QK_NOPE = 128
QK_ROPE = 64
HEADS = (D_MODEL - GMLP_WIDTH) // HEAD_V
KV_RANK = 512
Q_RANK = D_MODEL // 4
D_FF = 256 * ((8 * D_MODEL // 3 + 255) // 256)
N_MOD = 9
ROPE_THETA = 10000.0
LN_EPS = 1e-5
RMS_EPS = 1e-6
ALPHA = 2.0 ** 0.25
Q_SCALE = (QK_NOPE + QK_ROPE) ** -0.5 * 1.4426950408889634
NEG = -1e30

LANES = 128
HEAD_PAD = 2 * LANES
S_ROWS = DEC_BATCH * DEC_SEQ
M_ROWS = SEQ + S_ROWS
SEG = DEC_SEQ
N_SEG = M_ROWS // SEG
TM = M_ROWS // 8
TR = 8 * SEG
IN_COLS = 2 * GMLP_WIDTH + Q_RANK + KV_RANK
IN_TILE = 512
KV_PAD = 17 * LANES
MIB = 2 ** 20


def _params(n_axes, vmem_mib):
    return pltpu.CompilerParams(dimension_semantics=("arbitrary",) * n_axes,
                                vmem_limit_bytes=vmem_mib * MIB)


C_ROWS = 16


def _adaln_tile(c_ref, w_ref, b_ref):
    c = c_ref[...]
    s = (c * jax.nn.sigmoid(c)).astype(BF16)
    return lax.dot_general(s, w_ref[...], (((1,), (0,)), ((), ())), preferred_element_type=F32) + b_ref[...]


def _adaln_specs(tn, comp0, block_of_step):
    per_comp = D_MODEL // tn

    def col(*idx):
        return comp0 * per_comp + block_of_step(*idx)

    in_specs = [pl.BlockSpec((C_ROWS, D_MODEL), lambda *idx: (0, 0)),
                pl.BlockSpec((D_MODEL, tn), lambda *idx: (0, col(*idx))),
                pl.BlockSpec((1, tn), lambda *idx: (0, col(*idx)))]
    out_spec = pl.BlockSpec((1, C_ROWS, tn),
                            lambda *idx: (block_of_step(*idx) // per_comp, 0, block_of_step(*idx) % per_comp))
    return in_specs, out_spec


def _mod_kernel(c_ref, w_ref, b_ref, o_ref):
    o_ref[0] = _adaln_tile(c_ref, w_ref, b_ref)


def _modulation(c16, w_ada, b_ada, comp0, n_comp):
    tn = 512
    in_specs, out_spec = _adaln_specs(tn, comp0, lambda j: j)
    return pl.pallas_call(
        _mod_kernel,
        grid=(n_comp * D_MODEL // tn,),
        in_specs=in_specs,
        out_specs=out_spec,
        out_shape=jax.ShapeDtypeStruct((n_comp, C_ROWS, D_MODEL), F32),
        compiler_params=_params(1, 40),
        name="adaln_mod",
    )(c16, w_ada, b_ada.reshape(1, -1))


def _segment_table(mod):
    n = mod.shape[0]
    return jnp.concatenate([jnp.broadcast_to(mod[:, 0:1], (n, SEQ // SEG, D_MODEL)), mod[:, 1:1 + DEC_BATCH]], axis=1)


def _seg_operand(entry, tile_rows, width, index_map):
    table, comp = entry
    view = table.reshape(table.shape[0], M_ROWS // tile_rows, tile_rows // SEG, D_MODEL)
    spec = pl.BlockSpec((1, 1, tile_rows // SEG, width), lambda *idx: (comp,) + tuple(index_map(*idx)))
    return view, spec


def _split_x_specs(tile_rows, width, col):
    n_prompt = SEQ // tile_rows
    return [pl.BlockSpec((tile_rows, width), lambda *idx: (jnp.minimum(idx[0], n_prompt - 1), col(*idx))),
            pl.BlockSpec((tile_rows, width), lambda *idx: (jnp.maximum(idx[0] - n_prompt, 0), col(*idx)))]


def _on_owner(x_refs, tile_rows, body):
    if len(x_refs) == 1:
        body(x_refs[0])
    else:
        i = pl.program_id(0)
        n_prompt = SEQ // tile_rows
        pl.when(i < n_prompt)(lambda: body(x_refs[0]))
        pl.when(i >= n_prompt)(lambda: body(x_refs[1]))


def _modulate_kernel(xp_ref, xs_ref, sc_ref, sh_ref, h_ref):
    def body(x_ref):
        for s in range(TR // SEG):
            rows = slice(s * SEG, (s + 1) * SEG)
            h = x_ref[rows, :] * (1.0 + sc_ref[0, 0, s:s + 1, :]) + sh_ref[0, 0, s:s + 1, :]
            h_ref[rows, :] = h.astype(h_ref.dtype)

    _on_owner((xp_ref, xs_ref), TR, body)


def _modulate(xs, sc, sh):
    sc_arr, sc_spec = _seg_operand(sc, TR, D_MODEL, lambda i: (i, 0, 0))
    sh_arr, sh_spec = _seg_operand(sh, TR, D_MODEL, lambda i: (i, 0, 0))
    return pl.pallas_call(
        _modulate_kernel,
        grid=(M_ROWS // TR,),
        in_specs=_split_x_specs(TR, D_MODEL, lambda i: 0) + [sc_spec, sh_spec],
        out_specs=pl.BlockSpec((TR, D_MODEL), lambda i: (i, 0)),
        out_shape=jax.ShapeDtypeStruct((M_ROWS, D_MODEL), BF16),
        compiler_params=_params(1, 48),
        name="modulate",
    )(*xs, sc_arr, sh_arr)


def _gated_residual(x_refs, tile_rows, acc, gate_ref, gate_scale, z_ref):
    if len(x_refs) == 2:
        from_prompt = pl.program_id(0) < SEQ // tile_rows
    for s in range(acc.shape[0] // SEG):
        rows = slice(s * SEG, (s + 1) * SEG)
        x = x_refs[0][rows, :]
        if len(x_refs) == 2:
            x = jnp.where(from_prompt, x, x_refs[1][rows, :])
        z_ref[rows, :] = ALPHA * x + (gate_scale * gate_ref[0, 0, s:s + 1, :]) * acc[rows, :]


def _ln_kernel(z_ref, g_ref, b_ref, *rest, with_h):
    if with_h:
        sc_ref, sh_ref, xo_ref, h_ref = rest
    else:
        (xo_ref,) = rest
    for s in range(TR // SEG):
        rows = slice(s * SEG, (s + 1) * SEG)
        z = z_ref[rows, :]
        mu = jnp.mean(z, axis=-1, keepdims=True)
        zc = z - mu
        var = jnp.mean(zc * zc, axis=-1, keepdims=True)
        xn = zc * lax.rsqrt(var + LN_EPS) * g_ref[...] + b_ref[...]
        xo_ref[rows, :] = xn
        if with_h:
            h = xn * (1.0 + sc_ref[0, 0, s:s + 1, :]) + sh_ref[0, 0, s:s + 1, :]
            h_ref[rows, :] = h.astype(h_ref.dtype)


def _layernorm(z, ln_g, ln_b, sc=None, sh=None, *, row0=0, rows=M_ROWS):
    with_h = sc is not None
    off = row0 // TR
    row_out = pl.BlockSpec((TR, D_MODEL), lambda i: (i, 0))
    vec = pl.BlockSpec((1, D_MODEL), lambda i: (0, 0))
    in_specs = [pl.BlockSpec((TR, D_MODEL), lambda i: (i + off, 0)), vec, vec]
    args = [z, ln_g.reshape(1, D_MODEL), ln_b.reshape(1, D_MODEL)]
    out_specs = [row_out]
    out_shape = [jax.ShapeDtypeStruct((rows, D_MODEL), F32)]
    if with_h:
        for entry in (sc, sh):
            arr, spec = _seg_operand(entry, TR, D_MODEL, lambda i: (i + off, 0, 0))
            in_specs.append(spec)
            args.append(arr)
        out_specs.append(row_out)
        out_shape.append(jax.ShapeDtypeStruct((rows, D_MODEL), BF16))
    return pl.pallas_call(
        functools.partial(_ln_kernel, with_h=with_h),
        grid=(rows // TR,),
        in_specs=in_specs,
        out_specs=out_specs,
        out_shape=out_shape,
        compiler_params=_params(1, 56),
        name="ln_mod" if with_h else "ln",
    )(*args)


def _dot(a, b):
    return lax.dot_general(a, b, (((1,), (0,)), ((), ())), preferred_element_type=F32)


def _gateup_kernel(h_ref, wg_ref, wu_ref, wd_ref, o_ref, wd_out_ref):
    h = h_ref[...]
    g = _dot(h, wg_ref[...])
    u = _dot(h, wu_ref[...])
    o_ref[...] = (g * jax.nn.sigmoid(g) * u).astype(o_ref.dtype)

    @pl.when(pl.program_id(0) == 0)
    def _():
        wd_out_ref[...] = wd_ref[...].astype(wd_out_ref.dtype)


def _swiglu_up(h, wgu, w_down):
    tm = 2 * TM
    tn = 256
    nj = D_FF // tn
    slab = pl.BlockSpec((tn, D_MODEL), lambda i, j: (jnp.where(i == 0, j, nj - 1), 0))
    return pl.pallas_call(
        _gateup_kernel,
        grid=(M_ROWS // tm, nj),
        in_specs=[pl.BlockSpec((tm, D_MODEL), lambda i, j: (i, 0), pipeline_mode=pl.Buffered(1)),
                  pl.BlockSpec((D_MODEL, tn), lambda i, j: (0, j)),
                  pl.BlockSpec((D_MODEL, tn), lambda i, j: (0, j + nj)),
                  slab],
        out_specs=[pl.BlockSpec((tm, tn), lambda i, j: (i, j)), slab],
        out_shape=[jax.ShapeDtypeStruct((M_ROWS, D_FF), BF16),
                   jax.ShapeDtypeStruct((D_FF, D_MODEL), BF16)],
        compiler_params=_params(2, 56),
        name="swiglu_up",
    )(h, wgu, wgu, w_down)


def _mm_nt_kernel(a_ref, bt_ref, o_ref):
    o_ref[...] = lax.dot_general(a_ref[...], bt_ref[...], (((1,), (1,)), ((), ())), preferred_element_type=F32)


def _matmul_nt(a, bt, tm, tn, name, n_cols):
    m, k = a.shape
    return pl.pallas_call(
        _mm_nt_kernel,
        grid=(m // tm, n_cols // tn),
        in_specs=[pl.BlockSpec((tm, k), lambda i, j: (i, 0)),
                  pl.BlockSpec((tn, k), lambda i, j: (j, 0))],
        out_specs=pl.BlockSpec((tm, tn), lambda i, j: (i, j)),
        out_shape=jax.ShapeDtypeStruct((m, n_cols), F32),
        compiler_params=_params(2, 48),
        name=name,
    )(a, bt)


def _down_kernel(*refs, n_x, tile_rows, gate_scale, with_adaln):
    a_ref, b_ref = refs[:2]
    x_refs = refs[2:2 + n_x]
    gate_ref = refs[2 + n_x]
    if with_adaln:
        c_ref, w_ref, bias_ref, z_ref, mod_ref = refs[3 + n_x:]
    else:
        (z_ref,) = refs[3 + n_x:]
    acc = _dot(a_ref[...], b_ref[...])
    _gated_residual(x_refs, tile_rows, acc, gate_ref, gate_scale, z_ref)
    if with_adaln:
        mod_ref[0] = _adaln_tile(c_ref, w_ref, bias_ref)


def _ffn_down(a, w_down_bf16, xs, gate, adaln=None):
    tm, tn = TR, 512
    m, k = a.shape
    n = w_down_bf16.shape[1]
    grid = (m // tm, n // tn)
    gate_arr, gate_spec = _seg_operand(gate, tm, tn, lambda i, j: (i, 0, j))
    x_specs = (_split_x_specs(tm, tn, lambda i, j: j) if len(xs) == 2
               else [pl.BlockSpec((tm, tn), lambda i, j: (i, j))])
    in_specs = [pl.BlockSpec((tm, k), lambda i, j: (i, 0)),
                pl.BlockSpec((k, tn), lambda i, j: (0, j))] + x_specs + [gate_spec]
    args = [a, w_down_bf16, *xs, gate_arr]
    out_specs = [pl.BlockSpec((tm, tn), lambda i, j: (i, j))]
    out_shape = [jax.ShapeDtypeStruct((m, n), F32)]
    if adaln is not None:
        c16, w_ada, b_ada, comp0, n_comp = adaln
        side_tn = 256
        n_side = n_comp * D_MODEL // side_tn
        assert n_side <= grid[0] * grid[1]
        side_in, side_out = _adaln_specs(side_tn, comp0, lambda i, j: jnp.minimum(i * grid[1] + j, n_side - 1))
        in_specs += side_in
        args += [c16, w_ada, b_ada.reshape(1, -1)]
        out_specs.append(side_out)
        out_shape.append(jax.ShapeDtypeStruct((n_comp, C_ROWS, D_MODEL), F32))
    return pl.pallas_call(
        functools.partial(_down_kernel, n_x=len(xs), tile_rows=tm, gate_scale=0.5, with_adaln=adaln is not None),
        grid=grid,
        in_specs=in_specs,
        out_specs=out_specs,
        out_shape=out_shape,
        compiler_params=_params(2, 60),
        name="ffn_down" if adaln is None else "ffn_down_adaln",
    )(*args)


def _out_proj_kernel(a1_ref, a2_ref, b_ref, x_ref, gate_ref, z_ref):
    k1 = a1_ref.shape[1]
    acc = _dot(a1_ref[...], b_ref[:k1, :])
    acc += _dot(a2_ref[...], b_ref[k1:, :])
    _gated_residual((x_ref,), TM, acc, gate_ref, 1.0, z_ref)


def _out_proj(a1, a2, b, x, gate):
    tm, tn = TM, 512
    m, k1 = a1.shape
    k2 = a2.shape[1]
    n = b.shape[1]
    gate_arr, gate_spec = _seg_operand(gate, tm, tn, lambda i, j: (i, 0, j))
    tile = pl.BlockSpec((tm, tn), lambda i, j: (i, j))
    return pl.pallas_call(
        _out_proj_kernel,
        grid=(m // tm, n // tn),
        in_specs=[pl.BlockSpec((tm, k1), lambda i, j: (i, 0)),
                  pl.BlockSpec((tm, k2), lambda i, j: (i, 0)),
                  pl.BlockSpec((k1 + k2, tn), lambda i, j: (0, j)),
                  tile, gate_spec],
        out_specs=tile,
        out_shape=jax.ShapeDtypeStruct((m, n), F32),
        compiler_params=_params(2, 56),
        name="out_proj",
    )(a1, a2, b, x, gate_arr)


def _rope128(blk, cc, ss):
    return blk * cc + pltpu.roll(blk, QK_ROPE, 1) * ss


def _q_kernel(cq_ref, g_ref, w_ref, cc_ref, ss_ref, o_ref):
    x = cq_ref[...]
    r = lax.rsqrt(jnp.mean(x * x, axis=-1, keepdims=True) + RMS_EPS)
    a = (x * r * g_ref[...]).astype(BF16)
    cc = cc_ref[...]
    ss = ss_ref[...]
    for h in range(HEADS):
        acc = _dot(a, w_ref[:, h * HEAD_PAD:(h + 1) * HEAD_PAD])
        o_ref[h, :, :LANES] = (acc[:, :LANES] * Q_SCALE).astype(o_ref.dtype)
        o_ref[h, :, LANES:] = (_rope128(acc[:, LANES:], cc, ss) * Q_SCALE).astype(o_ref.dtype)


def _q_proj(proj, q_norm_g, wuq, cc, ss):
    tm = TM // 2
    cq_block = (2 * GMLP_WIDTH) // Q_RANK
    return pl.pallas_call(
        _q_kernel,
        grid=(M_ROWS // tm,),
        in_specs=[pl.BlockSpec((tm, Q_RANK), lambda i: (i, cq_block)),
                  pl.BlockSpec((1, Q_RANK), lambda i: (0, 0)),
                  pl.BlockSpec((Q_RANK, HEADS * HEAD_PAD), lambda i: (0, 0)),
                  pl.BlockSpec((tm, LANES), lambda i: (i, 0)),
                  pl.BlockSpec((tm, LANES), lambda i: (i, 0))],
        out_specs=pl.BlockSpec((HEADS, tm, HEAD_PAD), lambda i: (0, i, 0)),
        out_shape=jax.ShapeDtypeStruct((HEADS, M_ROWS, HEAD_PAD), BF16),
        compiler_params=_params(1, 40),
        name="q_proj",
    )(proj, q_norm_g.reshape(1, Q_RANK), wuq, cc, ss)


def _latkr_kernel(ckv_ref, h_ref, wkr_ref, g_ref, cc_ref, ss_ref, lat_ref, kr_out_ref, kr128_ref):
    x = ckv_ref[...]
    r = lax.rsqrt(jnp.mean(x * x, axis=-1, keepdims=True) + RMS_EPS)
    lat_ref[...] = x * r * g_ref[...]
    blk = lax.dot_general(h_ref[...], wkr_ref[...], (((1,), (1,)), ((), ())), preferred_element_type=F32)
    rot = _rope128(blk, cc_ref[...], ss_ref[...])
    kr_out_ref[...] = rot[:, :QK_ROPE]
    kr128_ref[...] = rot


def _lat_krope(proj, h, wkr, kv_norm_g, cc, ss, row0, rows):
    tr = 512
    off = row0 // tr
    ckv_block = (2 * GMLP_WIDTH + Q_RANK) // KV_RANK
    return pl.pallas_call(
        _latkr_kernel,
        grid=(rows // tr,),
        in_specs=[pl.BlockSpec((tr, KV_RANK), lambda i: (i + off, ckv_block)),
                  pl.BlockSpec((tr, D_MODEL), lambda i: (i + off, 0)),
                  pl.BlockSpec((LANES, D_MODEL), lambda i: (0, 0)),
                  pl.BlockSpec((1, KV_RANK), lambda i: (0, 0)),
                  pl.BlockSpec((tr, LANES), lambda i: (i + off, 0)),
                  pl.BlockSpec((tr, LANES), lambda i: (i + off, 0))],
        out_specs=[pl.BlockSpec((tr, KV_RANK), lambda i: (i, 0)),
                   pl.BlockSpec((tr, QK_ROPE), lambda i: (i, 0)),
                   pl.BlockSpec((tr, LANES), lambda i: (i, 0))],
        out_shape=[jax.ShapeDtypeStruct((rows, KV_RANK), F32),
                   jax.ShapeDtypeStruct((rows, QK_ROPE), F32),
                   jax.ShapeDtypeStruct((rows, LANES), F32)],
        compiler_params=_params(1, 32),
        name="lat_krope",
    )(proj, h, wkr, kv_norm_g.reshape(1, KV_RANK), cc, ss)


def _kv_t_kernel(lat_ref, kr_ref, w_ref, wvt_ref, k_ref, vt_ref):
    a = lat_ref[...].astype(BF16)
    kr = kr_ref[...].astype(BF16)
    for h in range(HEADS):
        kn = _dot(a, w_ref[:, h * HEAD_PAD:h * HEAD_PAD + QK_NOPE])
        k_ref[h, :, :LANES] = kn.astype(BF16)
        k_ref[h, :, LANES:] = kr
        vt = lax.dot_general(wvt_ref[h], a, (((1,), (1,)), ((), ())), preferred_element_type=F32)
        vt_ref[h, 0] = vt.astype(BF16)


def _kv_proj_t(lat, kr128, wukv, wvt, tr):
    rows = lat.shape[0]
    return pl.pallas_call(
        _kv_t_kernel,
        grid=(rows // tr,),
        in_specs=[pl.BlockSpec((tr, KV_RANK), lambda i: (i, 0)),
                  pl.BlockSpec((tr, LANES), lambda i: (i, 0)),
                  pl.BlockSpec((KV_RANK, HEADS * HEAD_PAD), lambda i: (0, 0)),
                  pl.BlockSpec((HEADS, HEAD_V, KV_RANK), lambda i: (0, 0, 0))],
        out_specs=[pl.BlockSpec((HEADS, tr, HEAD_PAD), lambda i: (0, i, 0)),
                   pl.BlockSpec((HEADS, 1, HEAD_V, tr), lambda i: (0, i, 0, 0))],
        out_shape=[jax.ShapeDtypeStruct((HEADS, rows, HEAD_PAD), BF16),
                   jax.ShapeDtypeStruct((HEADS, rows // tr, HEAD_V, tr), BF16)],
        compiler_params=_params(1, 48),
        name="kv_proj_t",
    )(lat, kr128, wukv, wvt)


ATTN_TILE = 1024


def _softmax_pv_t(s, s_max, vt, carry):
    m, l, acc = carry
    m_new = jnp.maximum(m, s_max)
    p = jnp.exp2(s - m_new)
    a = jnp.exp2(m - m_new)
    l = a * l + jnp.sum(p, axis=0, keepdims=True)
    acc = a * acc + jnp.dot(vt, p.astype(BF16), preferred_element_type=F32)
    return m_new, l, acc


def _chunk_mask_t(q_rel0, nk, nq):
    kc = lax.broadcasted_iota(jnp.int32, (nk, nq), 0) >> CHUNK_SHIFT
    qc = (q_rel0 + lax.broadcasted_iota(jnp.int32, (nk, nq), 1)) >> CHUNK_SHIFT
    return qc >= kc


def _attn_prompt_kernel(q_ref, k_ref, vt_ref, o_ref, s_even, s_odd, *, rb):
    qi = pl.program_id(1)
    n_rb = ATTN_TILE // rb

    def scores(j, s_ref):
        k = k_ref[0, pl.ds(pl.multiple_of(j * ATTN_TILE, ATTN_TILE), ATTN_TILE), :]
        s = lax.dot_general(k, q_ref[0], (((1,), (1,)), ((), ())), preferred_element_type=F32)
        s_ref[...] = s
        return jnp.max(s, axis=0, keepdims=True)

    def make_step(s_cur, s_next):
        def step(j, carry):
            stats, s_max = carry
            next_max = scores(j + 1, s_next)
            vt = vt_ref[0, j]
            stats = tuple(_softmax_pv_t(s_cur[:, r * rb:(r + 1) * rb], s_max[:, r * rb:(r + 1) * rb], vt, stats[r])
                          for r in range(n_rb))
            return stats, next_max
        return step

    step_even, step_odd = make_step(s_even, s_odd), make_step(s_odd, s_even)

    def finish(s_ref, stats):
        for r in range(n_rb):
            nk = (r + 1) * rb
            s_r = jnp.where(_chunk_mask_t(r * rb, nk, rb), s_ref[:nk, r * rb:(r + 1) * rb], NEG)
            _, l, acc = _softmax_pv_t(s_r, jnp.max(s_r, axis=0, keepdims=True), vt_ref[0, qi, :, :nk], stats[r])
            o_ref[r * rb:(r + 1) * rb, :] = (acc / l).T.astype(o_ref.dtype)

    init = tuple((jnp.full((1, rb), NEG, F32), jnp.zeros((1, rb), F32), jnp.zeros((HEAD_V, rb), F32))
                 for _ in range(n_rb))
    carry = (init, scores(0, s_even))
    carry = lax.fori_loop(0, qi >> 1, lambda p, c: step_odd(2 * p + 1, step_even(2 * p, c)), carry)
    odd = (qi & 1) == 1
    stats, _ = lax.cond(odd, lambda: step_even(qi - 1, carry), lambda: carry)
    lax.cond(odd, lambda: finish(s_odd, stats), lambda: finish(s_even, stats))


def _attn_prompt(q, k, vt):
    return pl.pallas_call(
        functools.partial(_attn_prompt_kernel, rb=512),
        grid=(HEADS, SEQ // ATTN_TILE),
        in_specs=[pl.BlockSpec((1, ATTN_TILE, HEAD_PAD), lambda h, i: (h, i, 0)),
                  pl.BlockSpec((1, SEQ, HEAD_PAD), lambda h, i: (h, 0, 0)),
                  pl.BlockSpec((1, SEQ // ATTN_TILE, HEAD_V, ATTN_TILE), lambda h, i: (h, 0, 0, 0))],
        out_specs=pl.BlockSpec((ATTN_TILE, HEAD_V), lambda h, i: (i, h)),
        out_shape=jax.ShapeDtypeStruct((M_ROWS, HEADS * HEAD_V), BF16),
        scratch_shapes=[pltpu.VMEM((ATTN_TILE, ATTN_TILE), F32), pltpu.VMEM((ATTN_TILE, ATTN_TILE), F32)],
        compiler_params=_params(2, 48),
        name="attn_prompt",
    )(q, k, vt)


def _attn_sample_kernel(q_ref, lat_ref, kr_ref, w_ref, o_prev_ref, o_ref, qa_scr):
    del o_prev_ref
    nt = (((1,), (1,)), ((), ()))
    lat = lat_ref[0].astype(BF16)
    keys = jnp.concatenate([lat, kr_ref[0].astype(BF16)], axis=1)
    for h in range(HEADS):
        rows = slice(h * DEC_SEQ, (h + 1) * DEC_SEQ)
        w_uk = w_ref[:, h * HEAD_PAD:h * HEAD_PAD + QK_NOPE]
        q_lat = lax.dot_general(q_ref[h, :, :LANES], w_uk, nt, preferred_element_type=F32)
        qa_scr[rows, :KV_RANK] = q_lat.astype(BF16)
        qa_scr[rows, KV_RANK:] = q_ref[h, :, LANES:]
    s = lax.dot_general(qa_scr[...], keys, nt, preferred_element_type=F32)
    shape = (HEADS * DEC_SEQ, KV_PAD)
    q_pos = PAST_LEN + (lax.broadcasted_iota(jnp.int32, shape, 0) & (DEC_SEQ - 1))
    k_pos = lax.broadcasted_iota(jnp.int32, shape, 1)
    visible = ((q_pos >> CHUNK_SHIFT) >= (k_pos >> CHUNK_SHIFT)) & (k_pos < PAST_LEN + DEC_SEQ)
    s = jnp.where(visible, s, NEG)
    m = jnp.max(s, axis=-1, keepdims=True)
    p = jnp.exp2(s - m)
    l = jnp.sum(p, axis=-1, keepdims=True)
    o_lat = (jnp.dot(p.astype(BF16), lat, preferred_element_type=F32) / l).astype(BF16)
    for h in range(HEADS):
        w_uv = w_ref[:, h * HEAD_PAD + QK_NOPE:(h + 1) * HEAD_PAD]
        o_h = _dot(o_lat[h * DEC_SEQ:(h + 1) * DEC_SEQ, :], w_uv)
        o_ref[:, h * HEAD_V:(h + 1) * HEAD_V] = o_h.astype(o_ref.dtype)


def _attn_sample(q, lat_all, kr_all, wukv, o_prev):
    q_off = SEQ // DEC_SEQ
    return pl.pallas_call(
        _attn_sample_kernel,
        grid=(DEC_BATCH,),
        in_specs=[pl.BlockSpec((HEADS, DEC_SEQ, HEAD_PAD), lambda b: (0, q_off + b, 0)),
                  pl.BlockSpec((1, KV_PAD, KV_RANK), lambda b: (b, 0, 0)),
                  pl.BlockSpec((1, KV_PAD, LANES), lambda b: (b, 0, 0)),
                  pl.BlockSpec((KV_RANK, HEADS * HEAD_PAD), lambda b: (0, 0)),
                  pl.BlockSpec(memory_space=pl.ANY)],
        out_specs=pl.BlockSpec((DEC_SEQ, HEADS * HEAD_V), lambda b: (q_off + b, 0)),
        out_shape=jax.ShapeDtypeStruct((M_ROWS, HEADS * HEAD_V), BF16),
        scratch_shapes=[pltpu.VMEM((HEADS * DEC_SEQ, KV_RANK + LANES), BF16)],
        input_output_aliases={4: 0},
        compiler_params=_params(1, 48),
        name="attn_sample",
    )(q, lat_all, kr_all, wukv, o_prev)


def _gmlp_kernel(u_ref, v_ref, g_ref, b_ref, w_ref, bs_ref, *rest, chunk, emit_vn):
    if emit_vn:
        _, a_ref, vn_ref = rest
    else:
        (a_ref,) = rest
    v = v_ref[...]
    mu = jnp.mean(v, axis=-1, keepdims=True)
    vc = v - mu
    var = jnp.mean(vc * vc, axis=-1, keepdims=True)
    vn = vc * lax.rsqrt(var + LN_EPS) * g_ref[...] + b_ref[...]
    if emit_vn:
        vn_ref[...] = vn
    vnb = vn.astype(BF16)
    n_chunks = v.shape[0] // chunk
    causal = (lax.broadcasted_iota(jnp.int32, (chunk, GMLP_CHUNK), 0)
              >= lax.broadcasted_iota(jnp.int32, (chunk, GMLP_CHUNK), 1))
    for g in range(GMLP_GROUPS):
        cols = slice(g * GROUP_DIM, (g + 1) * GROUP_DIM)
        w = jnp.where(causal, w_ref[g, :chunk, :], 0.0).astype(BF16)
        rhs = jnp.concatenate([vnb[c * chunk:(c + 1) * chunk, cols] for c in range(n_chunks)], axis=1)
        if chunk < GMLP_CHUNK:
            rhs = jnp.concatenate([rhs, jnp.zeros((GMLP_CHUNK - chunk, rhs.shape[1]), BF16)], axis=0)
        mixed = jnp.dot(w, rhs, preferred_element_type=F32) + bs_ref[g, :chunk, :]
        for c in range(n_chunks):
            rows = slice(c * chunk, (c + 1) * chunk)
            gate = mixed[:, c * GROUP_DIM:(c + 1) * GROUP_DIM]
            a_ref[rows, cols] = (u_ref[rows, cols] * gate).astype(a_ref.dtype)


def _gmlp(proj, ln_g, ln_b, w_s, b_s3, row0, rows, tr, chunk, a_prev=None):
    emit_vn = a_prev is not None
    off = row0 // tr
    vec = pl.BlockSpec((1, GMLP_WIDTH), lambda i: (0, 0))
    in_specs = [pl.BlockSpec((tr, GMLP_WIDTH), lambda i: (i + off, 0)),
                pl.BlockSpec((tr, GMLP_WIDTH), lambda i: (i + off, 1)),
                vec, vec,
                pl.BlockSpec((GMLP_GROUPS, GMLP_CHUNK, GMLP_CHUNK), lambda i: (0, 0, 0)),
                pl.BlockSpec((GMLP_GROUPS, GMLP_CHUNK, 1), lambda i: (0, 0, 0))]
    args = [proj, proj, ln_g.reshape(1, GMLP_WIDTH), ln_b.reshape(1, GMLP_WIDTH), w_s, b_s3]
    out_specs = [pl.BlockSpec((tr, GMLP_WIDTH), lambda i: (i + off, 0))]
    out_shape = [jax.ShapeDtypeStruct((M_ROWS, GMLP_WIDTH), BF16)]
    aliases = {}
    if emit_vn:
        in_specs.append(pl.BlockSpec(memory_space=pl.ANY))
        args.append(a_prev)
        aliases = {len(args) - 1: 0}
        out_specs.append(pl.BlockSpec((tr, GMLP_WIDTH), lambda i: (i, 0)))
        out_shape.append(jax.ShapeDtypeStruct((rows, GMLP_WIDTH), F32))
    return pl.pallas_call(
        functools.partial(_gmlp_kernel, chunk=chunk, emit_vn=emit_vn),
        grid=(rows // tr,),
        in_specs=in_specs,
        out_specs=out_specs,
        out_shape=out_shape,
        input_output_aliases=aliases,
        compiler_params=_params(1, 40),
        name="gmlp_gate_vn" if emit_vn else "gmlp_gate",
    )(*args)


def _swap_halves(w):
    half = w.shape[-1] // 2
    return jnp.concatenate([w[..., half:], w[..., :half]], axis=-1)


def _rope_tables():
    pos = jnp.concatenate([jnp.arange(SEQ, dtype=jnp.int32),
                           jnp.tile(PAST_LEN + jnp.arange(DEC_SEQ, dtype=jnp.int32), DEC_BATCH)])
    inv = 1.0 / (ROPE_THETA ** (jnp.arange(0, QK_ROPE, 2, dtype=F32) / QK_ROPE))
    ang = pos.astype(F32)[:, None] * inv[None, :]
    cos, sin = jnp.cos(ang), jnp.sin(ang)
    zero = jnp.zeros((M_ROWS, LANES - QK_ROPE), F32)
    return (jnp.concatenate([cos, cos, zero], axis=1), jnp.concatenate([-sin, sin, zero], axis=1))


def kernel(x_prompt, x_sample, cache_mla_latent, cache_mla_krope, c_prompt, c_sample, w_ada, b_ada, ffn1_w_gate_up, ffn1_w_down, ln1_g, ln1_b, w_in, gmlp_ln_g, gmlp_ln_b, gmlp_w_s, gmlp_b_s, mla_q_norm_g, mla_w_uq, mla_kv_norm_g, mla_w_ukv, w_out, ln2_g, ln2_b, ffn2_w_gate_up, ffn2_w_down, ln3_g, ln3_b):
    wukv = mla_w_ukv.astype(BF16)
    wvt = mla_w_ukv.reshape(KV_RANK, HEADS, HEAD_PAD)[:, :, QK_NOPE:].transpose(1, 2, 0).astype(BF16)
    w_in_t = w_in.T
    w_kr_t = w_in_t[IN_COLS:]
    wkr = jnp.concatenate([w_kr_t, w_kr_t[QK_ROPE // 2:], w_kr_t[:QK_ROPE // 2]], axis=0)
    uq = mla_w_uq.reshape(Q_RANK, HEADS, QK_NOPE + QK_ROPE)
    wuq = jnp.concatenate([uq, _swap_halves(uq[..., QK_NOPE:])], axis=-1).reshape(Q_RANK, HEADS * HEAD_PAD).astype(BF16)
    b_s3 = gmlp_b_s[:, :, None]
    cc, ss = _rope_tables()

    c16 = jnp.concatenate([c_prompt, c_sample, jnp.zeros((C_ROWS - 1 - DEC_BATCH, D_MODEL), F32)], axis=0)
    n_first = 3
    first = _segment_table(_modulation(c16, w_ada, b_ada, 0, n_first))
    SH1, SC1, G1 = ((first, i) for i in range(n_first))

    x0 = (x_prompt.reshape(SEQ, D_MODEL), x_sample.reshape(S_ROWS, D_MODEL))

    h1 = _modulate(x0, SC1, SH1)
    z1, mod_rest = _ffn_down(*_swiglu_up(h1, ffn1_w_gate_up, ffn1_w_down), x0, G1,
                             adaln=(c16, w_ada, b_ada, n_first, N_MOD - n_first))
    rest = _segment_table(mod_rest)
    SH2, SC2, G2, SH3, SC3, G3 = ((rest, i) for i in range(N_MOD - n_first))
    x1, h2 = _layernorm(z1, ln1_g, ln1_b, SC2, SH2)

    proj = _matmul_nt(h2, w_in_t, TM, IN_TILE, "in_proj", n_cols=IN_COLS)
    (a_mix,) = _gmlp(proj, gmlp_ln_g, gmlp_ln_b, gmlp_w_s, b_s3, 0, SEQ, 512, GMLP_CHUNK)
    a_mix, vn_s = _gmlp(proj, gmlp_ln_g, gmlp_ln_b, gmlp_w_s, b_s3, SEQ, S_ROWS, DEC_SEQ, DEC_SEQ, a_prev=a_mix)
    q = _q_proj(proj, mla_q_norm_g, wuq, cc, ss)
    lat_p, kr_p, kr128_p = _lat_krope(proj, h2, wkr, mla_kv_norm_g, cc, ss, 0, SEQ)
    lat_s, kr_s, kr128_s = _lat_krope(proj, h2, wkr, mla_kv_norm_g, cc, ss, SEQ, S_ROWS)

    k_p, vt_p = _kv_proj_t(lat_p, kr128_p, wukv, wvt, ATTN_TILE)
    o_mix = _attn_prompt(q, k_p, vt_p)

    pad_rows = KV_PAD - PAST_LEN - DEC_SEQ
    lat_all = jnp.concatenate([cache_mla_latent, lat_s.reshape(DEC_BATCH, DEC_SEQ, KV_RANK),
                               jnp.zeros((DEC_BATCH, pad_rows, KV_RANK), F32)], axis=1)
    cache_kr128 = jnp.pad(cache_mla_krope, ((0, 0), (0, 0), (0, LANES - QK_ROPE)))
    kr_all = jnp.concatenate([cache_kr128, kr128_s.reshape(DEC_BATCH, DEC_SEQ, LANES),
                              jnp.zeros((DEC_BATCH, pad_rows, LANES), F32)], axis=1)
    o_mix = _attn_sample(q, lat_all, kr_all, wukv, o_mix)

    z2 = _out_proj(a_mix, o_mix, w_out, x1, G2)
    x2, h3 = _layernorm(z2, ln2_g, ln2_b, SC3, SH3)

    (z3,) = _ffn_down(*_swiglu_up(h3, ffn2_w_gate_up, ffn2_w_down), (x2,), G3)
    (y_p,) = _layernorm(z3, ln3_g, ln3_b, row0=0, rows=SEQ)
    (y_s,) = _layernorm(z3, ln3_g, ln3_b, row0=SEQ, rows=S_ROWS)

    return (y_p.reshape(1, SEQ, D_MODEL),
            y_s.reshape(DEC_BATCH, DEC_SEQ, D_MODEL),
            lat_p.reshape(1, SEQ, KV_RANK),
            kr_p.reshape(1, SEQ, QK_ROPE),
            lat_s.reshape(DEC_BATCH, DEC_SEQ, KV_RANK),
            kr_s.reshape(DEC_BATCH, DEC_SEQ, QK_ROPE),
            vn_s.reshape(DEC_BATCH, DEC_SEQ, GMLP_WIDTH))
```

```python
import functools

import jax
import jax.numpy as jnp
from jax import lax
from jax.experimental import pallas as pl
from jax.experimental.pallas import tpu as pltpu

F32 = jnp.float32
BF16 = jnp.bfloat16

D_MODEL = 4096
SEQ = 8192
DEC_BATCH = 8
DEC_SEQ = 64
PAST_LEN = 2048
CHUNK = 64
CHUNK_SHIFT = 6
GMLP_CHUNK = 128
GROUP_DIM = 128
GMLP_WIDTH = D_MODEL // 2
GMLP_GROUPS = GMLP_WIDTH // GROUP_DIM
HEAD_V = 128
QK_NOPE = 128
QK_ROPE = 64
HEADS = (D_MODEL - GMLP_WIDTH) // HEAD_V
KV_RANK = 512
Q_RANK = D_MODEL // 4
D_FF = 256 * ((8 * D_MODEL // 3 + 255) // 256)
N_MOD = 9
ROPE_THETA = 10000.0
LN_EPS = 1e-5
RMS_EPS = 1e-6
ALPHA = 2.0 ** 0.25
Q_SCALE = (QK_NOPE + QK_ROPE) ** -0.5 * 1.4426950408889634
NEG = -1e30

LANES = 128
HEAD_PAD = 2 * LANES
S_ROWS = DEC_BATCH * DEC_SEQ
M_ROWS = SEQ + S_ROWS
SEG = DEC_SEQ
N_SEG = M_ROWS // SEG
TM = M_ROWS // 8
TR = 8 * SEG
IN_COLS = 2 * GMLP_WIDTH + Q_RANK + KV_RANK
IN_TILE = 512
KV_PAD = 17 * LANES
MIB = 2 ** 20


def _params(n_axes, vmem_mib):
    return pltpu.CompilerParams(dimension_semantics=("arbitrary",) * n_axes,
                                vmem_limit_bytes=vmem_mib * MIB)


C_ROWS = 16


def _adaln_tile(c_ref, w_ref, b_ref):
    c = c_ref[...]
    s = (c * jax.nn.sigmoid(c)).astype(BF16)
    return lax.dot_general(s, w_ref[...], (((1,), (0,)), ((), ())), preferred_element_type=F32) + b_ref[...]


def _adaln_specs(tn, comp0, block_of_step):
    per_comp = D_MODEL // tn

    def col(*idx):
        return comp0 * per_comp + block_of_step(*idx)

    in_specs = [pl.BlockSpec((C_ROWS, D_MODEL), lambda *idx: (0, 0)),
                pl.BlockSpec((D_MODEL, tn), lambda *idx: (0, col(*idx))),
                pl.BlockSpec((1, tn), lambda *idx: (0, col(*idx)))]
    out_spec = pl.BlockSpec((1, C_ROWS, tn),
                            lambda *idx: (block_of_step(*idx) // per_comp, 0, block_of_step(*idx) % per_comp))
    return in_specs, out_spec


def _mod_kernel(c_ref, w_ref, b_ref, o_ref):
    o_ref[0] = _adaln_tile(c_ref, w_ref, b_ref)


def _modulation(c16, w_ada, b_ada, comp0, n_comp):
    tn = 512
    in_specs, out_spec = _adaln_specs(tn, comp0, lambda j: j)
    return pl.pallas_call(
        _mod_kernel,
        grid=(n_comp * D_MODEL // tn,),
        in_specs=in_specs,
        out_specs=out_spec,
        out_shape=jax.ShapeDtypeStruct((n_comp, C_ROWS, D_MODEL), F32),
        compiler_params=_params(1, 40),
        name="adaln_mod",
    )(c16, w_ada, b_ada.reshape(1, -1))


def _segment_table(mod):
    n = mod.shape[0]
    return jnp.concatenate([jnp.broadcast_to(mod[:, 0:1], (n, SEQ // SEG, D_MODEL)), mod[:, 1:1 + DEC_BATCH]], axis=1)


def _seg_operand(entry, tile_rows, width, index_map):
    table, comp = entry
    view = table.reshape(table.shape[0], M_ROWS // tile_rows, tile_rows // SEG, D_MODEL)
    spec = pl.BlockSpec((1, 1, tile_rows // SEG, width), lambda *idx: (comp,) + tuple(index_map(*idx)))
    return view, spec


def _split_x_specs(tile_rows, width, col):
    n_prompt = SEQ // tile_rows
    return [pl.BlockSpec((tile_rows, width), lambda *idx: (jnp.minimum(idx[0], n_prompt - 1), col(*idx))),
            pl.BlockSpec((tile_rows, width), lambda *idx: (jnp.maximum(idx[0] - n_prompt, 0), col(*idx)))]


def _on_owner(x_refs, tile_rows, body):
    if len(x_refs) == 1:
        body(x_refs[0])
    else:
        i = pl.program_id(0)
        n_prompt = SEQ // tile_rows
        pl.when(i < n_prompt)(lambda: body(x_refs[0]))
        pl.when(i >= n_prompt)(lambda: body(x_refs[1]))


def _modulate_kernel(xp_ref, xs_ref, sc_ref, sh_ref, h_ref):
    def body(x_ref):
        for s in range(TR // SEG):
            rows = slice(s * SEG, (s + 1) * SEG)
            h = x_ref[rows, :] * (1.0 + sc_ref[0, 0, s:s + 1, :]) + sh_ref[0, 0, s:s + 1, :]
            h_ref[rows, :] = h.astype(h_ref.dtype)

    _on_owner((xp_ref, xs_ref), TR, body)


def _modulate(xs, sc, sh):
    sc_arr, sc_spec = _seg_operand(sc, TR, D_MODEL, lambda i: (i, 0, 0))
    sh_arr, sh_spec = _seg_operand(sh, TR, D_MODEL, lambda i: (i, 0, 0))
    return pl.pallas_call(
        _modulate_kernel,
        grid=(M_ROWS // TR,),
        in_specs=_split_x_specs(TR, D_MODEL, lambda i: 0) + [sc_spec, sh_spec],
        out_specs=pl.BlockSpec((TR, D_MODEL), lambda i: (i, 0)),
        out_shape=jax.ShapeDtypeStruct((M_ROWS, D_MODEL), BF16),
        compiler_params=_params(1, 48),
        name="modulate",
    )(*xs, sc_arr, sh_arr)


def _gated_residual(x_refs, tile_rows, acc, gate_ref, gate_scale, z_ref):
    if len(x_refs) == 2:
        from_prompt = pl.program_id(0) < SEQ // tile_rows
    for s in range(acc.shape[0] // SEG):
        rows = slice(s * SEG, (s + 1) * SEG)
        x = x_refs[0][rows, :]
        if len(x_refs) == 2:
            x = jnp.where(from_prompt, x, x_refs[1][rows, :])
        z_ref[rows, :] = ALPHA * x + (gate_scale * gate_ref[0, 0, s:s + 1, :]) * acc[rows, :]


def _ln_kernel(z_ref, g_ref, b_ref, *rest, with_h):
    if with_h:
        sc_ref, sh_ref, xo_ref, h_ref = rest
    else:
        (xo_ref,) = rest
    for s in range(TR // SEG):
        rows = slice(s * SEG, (s + 1) * SEG)
        z = z_ref[rows, :]
        mu = jnp.mean(z, axis=-1, keepdims=True)
        zc = z - mu
        var = jnp.mean(zc * zc, axis=-1, keepdims=True)
        xn = zc * lax.rsqrt(var + LN_EPS) * g_ref[...] + b_ref[...]
        xo_ref[rows, :] = xn
        if with_h:
            h = xn * (1.0 + sc_ref[0, 0, s:s + 1, :]) + sh_ref[0, 0, s:s + 1, :]
            h_ref[rows, :] = h.astype(h_ref.dtype)


def _layernorm(z, ln_g, ln_b, sc=None, sh=None, *, row0=0, rows=M_ROWS):
    with_h = sc is not None
    off = row0 // TR
    row_out = pl.BlockSpec((TR, D_MODEL), lambda i: (i, 0))
    vec = pl.BlockSpec((1, D_MODEL), lambda i: (0, 0))
    in_specs = [pl.BlockSpec((TR, D_MODEL), lambda i: (i + off, 0)), vec, vec]
    args = [z, ln_g.reshape(1, D_MODEL), ln_b.reshape(1, D_MODEL)]
    out_specs = [row_out]
    out_shape = [jax.ShapeDtypeStruct((rows, D_MODEL), F32)]
    if with_h:
        for entry in (sc, sh):
            arr, spec = _seg_operand(entry, TR, D_MODEL, lambda i: (i + off, 0, 0))
            in_specs.append(spec)
            args.append(arr)
        out_specs.append(row_out)
        out_shape.append(jax.ShapeDtypeStruct((rows, D_MODEL), BF16))
    return pl.pallas_call(
        functools.partial(_ln_kernel, with_h=with_h),
        grid=(rows // TR,),
        in_specs=in_specs,
        out_specs=out_specs,
        out_shape=out_shape,
        compiler_params=_params(1, 56),
        name="ln_mod" if with_h else "ln",
    )(*args)


def _dot(a, b):
    return lax.dot_general(a, b, (((1,), (0,)), ((), ())), preferred_element_type=F32)


def _gateup_kernel(h_ref, wg_ref, wu_ref, wd_ref, o_ref, wd_out_ref):
    h = h_ref[...]
    g = _dot(h, wg_ref[...])
    u = _dot(h, wu_ref[...])
    o_ref[...] = (g * jax.nn.sigmoid(g) * u).astype(o_ref.dtype)

    @pl.when(pl.program_id(0) == 0)
    def _():
        wd_out_ref[...] = wd_ref[...].astype(wd_out_ref.dtype)


def _swiglu_up(h, wgu, w_down):
    tm = 2 * TM
    tn = 256
    nj = D_FF // tn
    slab = pl.BlockSpec((tn, D_MODEL), lambda i, j: (jnp.where(i == 0, j, nj - 1), 0))
    return pl.pallas_call(
        _gateup_kernel,
        grid=(M_ROWS // tm, nj),
        in_specs=[pl.BlockSpec((tm, D_MODEL), lambda i, j: (i, 0), pipeline_mode=pl.Buffered(1)),
                  pl.BlockSpec((D_MODEL, tn), lambda i, j: (0, j)),
                  pl.BlockSpec((D_MODEL, tn), lambda i, j: (0, j + nj)),
                  slab],
        out_specs=[pl.BlockSpec((tm, tn), lambda i, j: (i, j)), slab],
        out_shape=[jax.ShapeDtypeStruct((M_ROWS, D_FF), BF16),
                   jax.ShapeDtypeStruct((D_FF, D_MODEL), BF16)],
        compiler_params=_params(2, 56),
        name="swiglu_up",
    )(h, wgu, wgu, w_down)


def _mm_nt_kernel(a_ref, bt_ref, o_ref):
    o_ref[...] = lax.dot_general(a_ref[...], bt_ref[...], (((1,), (1,)), ((), ())), preferred_element_type=F32)


def _matmul_nt(a, bt, tm, tn, name, n_cols):
    m, k = a.shape
    return pl.pallas_call(
        _mm_nt_kernel,
        grid=(m // tm, n_cols // tn),
        in_specs=[pl.BlockSpec((tm, k), lambda i, j: (i, 0)),
                  pl.BlockSpec((tn, k), lambda i, j: (j, 0))],
        out_specs=pl.BlockSpec((tm, tn), lambda i, j: (i, j)),
        out_shape=jax.ShapeDtypeStruct((m, n_cols), F32),
        compiler_params=_params(2, 48),
        name=name,
    )(a, bt)


def _down_kernel(*refs, n_x, tile_rows, gate_scale, with_adaln):
    a_ref, b_ref = refs[:2]
    x_refs = refs[2:2 + n_x]
    gate_ref = refs[2 + n_x]
    if with_adaln:
        c_ref, w_ref, bias_ref, z_ref, mod_ref = refs[3 + n_x:]
    else:
        (z_ref,) = refs[3 + n_x:]
    acc = _dot(a_ref[...], b_ref[...])
    _gated_residual(x_refs, tile_rows, acc, gate_ref, gate_scale, z_ref)
    if with_adaln:
        mod_ref[0] = _adaln_tile(c_ref, w_ref, bias_ref)


def _ffn_down(a, w_down_bf16, xs, gate, adaln=None):
    tm, tn = TR, 512
    m, k = a.shape
    n = w_down_bf16.shape[1]
    grid = (m // tm, n // tn)
    gate_arr, gate_spec = _seg_operand(gate, tm, tn, lambda i, j: (i, 0, j))
    x_specs = (_split_x_specs(tm, tn, lambda i, j: j) if len(xs) == 2
               else [pl.BlockSpec((tm, tn), lambda i, j: (i, j))])
    in_specs = [pl.BlockSpec((tm, k), lambda i, j: (i, 0)),
                pl.BlockSpec((k, tn), lambda i, j: (0, j))] + x_specs + [gate_spec]
    args = [a, w_down_bf16, *xs, gate_arr]
    out_specs = [pl.BlockSpec((tm, tn), lambda i, j: (i, j))]
    out_shape = [jax.ShapeDtypeStruct((m, n), F32)]
    if adaln is not None:
        c16, w_ada, b_ada, comp0, n_comp = adaln
        side_tn = 256
        n_side = n_comp * D_MODEL // side_tn
        assert n_side <= grid[0] * grid[1]
        side_in, side_out = _adaln_specs(side_tn, comp0, lambda i, j: jnp.minimum(i * grid[1] + j, n_side - 1))
        in_specs += side_in
        args += [c16, w_ada, b_ada.reshape(1, -1)]
        out_specs.append(side_out)
        out_shape.append(jax.ShapeDtypeStruct((n_comp, C_ROWS, D_MODEL), F32))
    return pl.pallas_call(
        functools.partial(_down_kernel, n_x=len(xs), tile_rows=tm, gate_scale=0.5, with_adaln=adaln is not None),
        grid=grid,
        in_specs=in_specs,
        out_specs=out_specs,
        out_shape=out_shape,
        compiler_params=_params(2, 60),
        name="ffn_down" if adaln is None else "ffn_down_adaln",
    )(*args)


def _out_proj_kernel(a1_ref, a2_ref, b_ref, x_ref, gate_ref, z_ref):
    k1 = a1_ref.shape[1]
    acc = _dot(a1_ref[...], b_ref[:k1, :])
    acc += _dot(a2_ref[...], b_ref[k1:, :])
    _gated_residual((x_ref,), TM, acc, gate_ref, 1.0, z_ref)


def _out_proj(a1, a2, b, x, gate):
    tm, tn = TM, 512
    m, k1 = a1.shape
    k2 = a2.shape[1]
    n = b.shape[1]
    gate_arr, gate_spec = _seg_operand(gate, tm, tn, lambda i, j: (i, 0, j))
    tile = pl.BlockSpec((tm, tn), lambda i, j: (i, j))
    return pl.pallas_call(
        _out_proj_kernel,
        grid=(m // tm, n // tn),
        in_specs=[pl.BlockSpec((tm, k1), lambda i, j: (i, 0)),
                  pl.BlockSpec((tm, k2), lambda i, j: (i, 0)),
                  pl.BlockSpec((k1 + k2, tn), lambda i, j: (0, j)),
                  tile, gate_spec],
        out_specs=tile,
        out_shape=jax.ShapeDtypeStruct((m, n), F32),
        compiler_params=_params(2, 56),
        name="out_proj",
    )(a1, a2, b, x, gate_arr)


def _rope128(blk, cc, ss):
    return blk * cc + pltpu.roll(blk, QK_ROPE, 1) * ss


def _q_kernel(cq_ref, g_ref, w_ref, cc_ref, ss_ref, o_ref):
    x = cq_ref[...]
    r = lax.rsqrt(jnp.mean(x * x, axis=-1, keepdims=True) + RMS_EPS)
    a = (x * r * g_ref[...]).astype(BF16)
    cc = cc_ref[...]
    ss = ss_ref[...]
    for h in range(HEADS):
        acc = _dot(a, w_ref[:, h * HEAD_PAD:(h + 1) * HEAD_PAD])
        o_ref[h, :, :LANES] = (acc[:, :LANES] * Q_SCALE).astype(o_ref.dtype)
        o_ref[h, :, LANES:] = (_rope128(acc[:, LANES:], cc, ss) * Q_SCALE).astype(o_ref.dtype)


def _q_proj(proj, q_norm_g, wuq, cc, ss):
    tm = TM // 2
    cq_block = (2 * GMLP_WIDTH) // Q_RANK
    return pl.pallas_call(
        _q_kernel,
        grid=(M_ROWS // tm,),
        in_specs=[pl.BlockSpec((tm, Q_RANK), lambda i: (i, cq_block)),
                  pl.BlockSpec((1, Q_RANK), lambda i: (0, 0)),
                  pl.BlockSpec((Q_RANK, HEADS * HEAD_PAD), lambda i: (0, 0)),
                  pl.BlockSpec((tm, LANES), lambda i: (i, 0)),
                  pl.BlockSpec((tm, LANES), lambda i: (i, 0))],
        out_specs=pl.BlockSpec((HEADS, tm, HEAD_PAD), lambda i: (0, i, 0)),
        out_shape=jax.ShapeDtypeStruct((HEADS, M_ROWS, HEAD_PAD), BF16),
        compiler_params=_params(1, 40),
        name="q_proj",
    )(proj, q_norm_g.reshape(1, Q_RANK), wuq, cc, ss)


def _latkr_kernel(ckv_ref, h_ref, wkr_ref, g_ref, cc_ref, ss_ref, lat_ref, kr_out_ref, kr128_ref):
    x = ckv_ref[...]
    r = lax.rsqrt(jnp.mean(x * x, axis=-1, keepdims=True) + RMS_EPS)
    lat_ref[...] = x * r * g_ref[...]
    blk = lax.dot_general(h_ref[...], wkr_ref[...], (((1,), (1,)), ((), ())), preferred_element_type=F32)
    rot = _rope128(blk, cc_ref[...], ss_ref[...])
    kr_out_ref[...] = rot[:, :QK_ROPE]
    kr128_ref[...] = rot


def _lat_krope(proj, h, wkr, kv_norm_g, cc, ss, row0, rows):
    tr = 512
    off = row0 // tr
    ckv_block = (2 * GMLP_WIDTH + Q_RANK) // KV_RANK
    return pl.pallas_call(
        _latkr_kernel,
        grid=(rows // tr,),
        in_specs=[pl.BlockSpec((tr, KV_RANK), lambda i: (i + off, ckv_block)),
                  pl.BlockSpec((tr, D_MODEL), lambda i: (i + off, 0)),
                  pl.BlockSpec((LANES, D_MODEL), lambda i: (0, 0)),
                  pl.BlockSpec((1, KV_RANK), lambda i: (0, 0)),
                  pl.BlockSpec((tr, LANES), lambda i: (i + off, 0)),
                  pl.BlockSpec((tr, LANES), lambda i: (i + off, 0))],
        out_specs=[pl.BlockSpec((tr, KV_RANK), lambda i: (i, 0)),
                   pl.BlockSpec((tr, QK_ROPE), lambda i: (i, 0)),
                   pl.BlockSpec((tr, LANES), lambda i: (i, 0))],
        out_shape=[jax.ShapeDtypeStruct((rows, KV_RANK), F32),
                   jax.ShapeDtypeStruct((rows, QK_ROPE), F32),
                   jax.ShapeDtypeStruct((rows, LANES), F32)],
        compiler_params=_params(1, 32),
        name="lat_krope",
    )(proj, h, wkr, kv_norm_g.reshape(1, KV_RANK), cc, ss)


def _kv_t_kernel(lat_ref, kr_ref, w_ref, wvt_ref, k_ref, vt_ref):
    a = lat_ref[...].astype(BF16)
    kr = kr_ref[...].astype(BF16)
    for h in range(HEADS):
        kn = _dot(a, w_ref[:, h * HEAD_PAD:h * HEAD_PAD + QK_NOPE])
        k_ref[h, :, :LANES] = kn.astype(BF16)
        k_ref[h, :, LANES:] = kr
        vt = lax.dot_general(wvt_ref[h], a, (((1,), (1,)), ((), ())), preferred_element_type=F32)
        vt_ref[h, 0] = vt.astype(BF16)


def _kv_proj_t(lat, kr128, wukv, wvt, tr):
    rows = lat.shape[0]
    return pl.pallas_call(
        _kv_t_kernel,
        grid=(rows // tr,),
        in_specs=[pl.BlockSpec((tr, KV_RANK), lambda i: (i, 0)),
                  pl.BlockSpec((tr, LANES), lambda i: (i, 0)),
                  pl.BlockSpec((KV_RANK, HEADS * HEAD_PAD), lambda i: (0, 0)),
                  pl.BlockSpec((HEADS, HEAD_V, KV_RANK), lambda i: (0, 0, 0))],
        out_specs=[pl.BlockSpec((HEADS, tr, HEAD_PAD), lambda i: (0, i, 0)),
                   pl.BlockSpec((HEADS, 1, HEAD_V, tr), lambda i: (0, i, 0, 0))],
        out_shape=[jax.ShapeDtypeStruct((HEADS, rows, HEAD_PAD), BF16),
                   jax.ShapeDtypeStruct((HEADS, rows // tr, HEAD_V, tr), BF16)],
        compiler_params=_params(1, 48),
        name="kv_proj_t",
    )(lat, kr128, wukv, wvt)


ATTN_TILE = 1024


def _softmax_pv_t(s, s_max, vt, carry):
    m, l, acc = carry
    m_new = jnp.maximum(m, s_max)
    p = jnp.exp2(s - m_new)
    a = jnp.exp2(m - m_new)
    l = a * l + jnp.sum(p, axis=0, keepdims=True)
    acc = a * acc + jnp.dot(vt, p.astype(BF16), preferred_element_type=F32)
    return m_new, l, acc


def _chunk_mask_t(q_rel0, nk, nq):
    kc = lax.broadcasted_iota(jnp.int32, (nk, nq), 0) >> CHUNK_SHIFT
    qc = (q_rel0 + lax.broadcasted_iota(jnp.int32, (nk, nq), 1)) >> CHUNK_SHIFT
    return qc >= kc


def _attn_prompt_kernel(q_ref, k_ref, vt_ref, o_ref, s_even, s_odd, *, rb, nh):
    qi = pl.program_id(1)
    n_rb = ATTN_TILE // rb
    heads = range(nh)

    def scores(j, s_ref):
        k0 = pl.multiple_of(j * ATTN_TILE, ATTN_TILE)
        maxima = []
        for hh in heads:
            s = lax.dot_general(k_ref[hh, pl.ds(k0, ATTN_TILE), :], q_ref[hh], (((1,), (1,)), ((), ())),
                                preferred_element_type=F32)
            s_ref[hh] = s
            maxima.append(jnp.max(s, axis=0, keepdims=True))
        return tuple(maxima)

    def make_step(s_cur, s_next):
        def step(j, carry):
            stats, s_max = carry
            next_max = scores(j + 1, s_next)
            stats = tuple(
                tuple(_softmax_pv_t(s_cur[hh, :, r * rb:(r + 1) * rb], s_max[hh][:, r * rb:(r + 1) * rb],
                                    vt_ref[hh, j], stats[hh][r]) for r in range(n_rb))
                for hh in heads)
            return stats, next_max
        return step

    step_even, step_odd = make_step(s_even, s_odd), make_step(s_odd, s_even)

    def finish(s_ref, stats):
        mask = _chunk_mask_t(0, rb, rb)
        for r in range(n_rb):
            nk = (r + 1) * rb
            cols = slice(r * rb, nk)
            for hh in heads:
                s_r = jnp.where(mask, s_ref[hh, r * rb:nk, cols], NEG)
                if r > 0:
                    s_r = jnp.concatenate([s_ref[hh, :r * rb, cols], s_r], axis=0)
                _, l, acc = _softmax_pv_t(s_r, jnp.max(s_r, axis=0, keepdims=True), vt_ref[hh, qi, :, :nk],
                                          stats[hh][r])
                o_ref[r * rb:(r + 1) * rb, hh * HEAD_V:(hh + 1) * HEAD_V] = (acc / l).T.astype(o_ref.dtype)

    init = tuple(tuple((jnp.full((1, rb), NEG, F32), jnp.zeros((1, rb), F32), jnp.zeros((HEAD_V, rb), F32))
                       for _ in range(n_rb)) for _ in heads)
    carry = (init, scores(0, s_even))
    carry = lax.fori_loop(0, qi >> 1, lambda p, c: step_odd(2 * p + 1, step_even(2 * p, c)), carry)
    odd = (qi & 1) == 1
    stats, _ = lax.cond(odd, lambda: step_even(qi - 1, carry), lambda: carry)
    lax.cond(odd, lambda: finish(s_odd, stats), lambda: finish(s_even, stats))


def _attn_prompt(q, k, vt):
    nh = 2
    return pl.pallas_call(
        functools.partial(_attn_prompt_kernel, rb=512, nh=nh),
        grid=(HEADS // nh, SEQ // ATTN_TILE),
        in_specs=[pl.BlockSpec((nh, ATTN_TILE, HEAD_PAD), lambda h, i: (h, i, 0)),
                  pl.BlockSpec((nh, SEQ, HEAD_PAD), lambda h, i: (h, 0, 0)),
                  pl.BlockSpec((nh, SEQ // ATTN_TILE, HEAD_V, ATTN_TILE), lambda h, i: (h, 0, 0, 0))],
        out_specs=pl.BlockSpec((ATTN_TILE, nh * HEAD_V), lambda h, i: (i, h)),
        out_shape=jax.ShapeDtypeStruct((M_ROWS, HEADS * HEAD_V), BF16),
        scratch_shapes=[pltpu.VMEM((nh, ATTN_TILE, ATTN_TILE), F32), pltpu.VMEM((nh, ATTN_TILE, ATTN_TILE), F32)],
        compiler_params=_params(2, 56),
        name="attn_prompt",
    )(q, k, vt)


def _attn_sample_kernel(q_ref, lat_ref, kr_ref, w_ref, o_prev_ref, o_ref, qa_scr):
    del o_prev_ref
    nt = (((1,), (1,)), ((), ()))
    lat = lat_ref[0].astype(BF16)
    keys = jnp.concatenate([lat, kr_ref[0].astype(BF16)], axis=1)
    for h in range(HEADS):
        rows = slice(h * DEC_SEQ, (h + 1) * DEC_SEQ)
        w_uk = w_ref[:, h * HEAD_PAD:h * HEAD_PAD + QK_NOPE]
        q_lat = lax.dot_general(q_ref[h, :, :LANES], w_uk, nt, preferred_element_type=F32)
        qa_scr[rows, :KV_RANK] = q_lat.astype(BF16)
        qa_scr[rows, KV_RANK:] = q_ref[h, :, LANES:]
    s = lax.dot_general(qa_scr[...], keys, nt, preferred_element_type=F32)
    shape = (HEADS * DEC_SEQ, KV_PAD)
    q_pos = PAST_LEN + (lax.broadcasted_iota(jnp.int32, shape, 0) & (DEC_SEQ - 1))
    k_pos = lax.broadcasted_iota(jnp.int32, shape, 1)
    visible = ((q_pos >> CHUNK_SHIFT) >= (k_pos >> CHUNK_SHIFT)) & (k_pos < PAST_LEN + DEC_SEQ)
    s = jnp.where(visible, s, NEG)
    m = jnp.max(s, axis=-1, keepdims=True)
    p = jnp.exp2(s - m)
    l = jnp.sum(p, axis=-1, keepdims=True)
    o_lat = (jnp.dot(p.astype(BF16), lat, preferred_element_type=F32) / l).astype(BF16)
    for h in range(HEADS):
        w_uv = w_ref[:, h * HEAD_PAD + QK_NOPE:(h + 1) * HEAD_PAD]
        o_h = _dot(o_lat[h * DEC_SEQ:(h + 1) * DEC_SEQ, :], w_uv)
        o_ref[:, h * HEAD_V:(h + 1) * HEAD_V] = o_h.astype(o_ref.dtype)


def _attn_sample(q, lat_all, kr_all, wukv, o_prev):
    q_off = SEQ // DEC_SEQ
    return pl.pallas_call(
        _attn_sample_kernel,
        grid=(DEC_BATCH,),
        in_specs=[pl.BlockSpec((HEADS, DEC_SEQ, HEAD_PAD), lambda b: (0, q_off + b, 0)),
                  pl.BlockSpec((1, KV_PAD, KV_RANK), lambda b: (b, 0, 0)),
                  pl.BlockSpec((1, KV_PAD, LANES), lambda b: (b, 0, 0)),
                  pl.BlockSpec((KV_RANK, HEADS * HEAD_PAD), lambda b: (0, 0)),
                  pl.BlockSpec(memory_space=pl.ANY)],
        out_specs=pl.BlockSpec((DEC_SEQ, HEADS * HEAD_V), lambda b: (q_off + b, 0)),
        out_shape=jax.ShapeDtypeStruct((M_ROWS, HEADS * HEAD_V), BF16),
        scratch_shapes=[pltpu.VMEM((HEADS * DEC_SEQ, KV_RANK + LANES), BF16)],
        input_output_aliases={4: 0},
        compiler_params=_params(1, 48),
        name="attn_sample",
    )(q, lat_all, kr_all, wukv, o_prev)


def _gmlp_kernel(u_ref, v_ref, g_ref, b_ref, w_ref, bs_ref, *rest, chunk, emit_vn):
    if emit_vn:
        _, a_ref, vn_ref = rest
    else:
        (a_ref,) = rest
    v = v_ref[...]
    mu = jnp.mean(v, axis=-1, keepdims=True)
    vc = v - mu
    var = jnp.mean(vc * vc, axis=-1, keepdims=True)
    vn = vc * lax.rsqrt(var + LN_EPS) * g_ref[...] + b_ref[...]
    if emit_vn:
        vn_ref[...] = vn
    vnb = vn.astype(BF16)
    n_chunks = v.shape[0] // chunk
    causal = (lax.broadcasted_iota(jnp.int32, (chunk, GMLP_CHUNK), 0)
              >= lax.broadcasted_iota(jnp.int32, (chunk, GMLP_CHUNK), 1))
    for g in range(GMLP_GROUPS):
        cols = slice(g * GROUP_DIM, (g + 1) * GROUP_DIM)
        w = jnp.where(causal, w_ref[g, :chunk, :], 0.0).astype(BF16)
        rhs = jnp.concatenate([vnb[c * chunk:(c + 1) * chunk, cols] for c in range(n_chunks)], axis=1)
        if chunk < GMLP_CHUNK:
            rhs = jnp.concatenate([rhs, jnp.zeros((GMLP_CHUNK - chunk, rhs.shape[1]), BF16)], axis=0)
        mixed = jnp.dot(w, rhs, preferred_element_type=F32) + bs_ref[g, :chunk, :]
        for c in range(n_chunks):
            rows = slice(c * chunk, (c + 1) * chunk)
            gate = mixed[:, c * GROUP_DIM:(c + 1) * GROUP_DIM]
            a_ref[rows, cols] = (u_ref[rows, cols] * gate).astype(a_ref.dtype)


def _gmlp(proj, ln_g, ln_b, w_s, b_s3, row0, rows, tr, chunk, a_prev=None):
    emit_vn = a_prev is not None
    off = row0 // tr
    vec = pl.BlockSpec((1, GMLP_WIDTH), lambda i: (0, 0))
    in_specs = [pl.BlockSpec((tr, GMLP_WIDTH), lambda i: (i + off, 0)),
                pl.BlockSpec((tr, GMLP_WIDTH), lambda i: (i + off, 1)),
                vec, vec,
                pl.BlockSpec((GMLP_GROUPS, GMLP_CHUNK, GMLP_CHUNK), lambda i: (0, 0, 0)),
                pl.BlockSpec((GMLP_GROUPS, GMLP_CHUNK, 1), lambda i: (0, 0, 0))]
    args = [proj, proj, ln_g.reshape(1, GMLP_WIDTH), ln_b.reshape(1, GMLP_WIDTH), w_s, b_s3]
    out_specs = [pl.BlockSpec((tr, GMLP_WIDTH), lambda i: (i + off, 0))]
    out_shape = [jax.ShapeDtypeStruct((M_ROWS, GMLP_WIDTH), BF16)]
    aliases = {}
    if emit_vn:
        in_specs.append(pl.BlockSpec(memory_space=pl.ANY))
        args.append(a_prev)
        aliases = {len(args) - 1: 0}
        out_specs.append(pl.BlockSpec((tr, GMLP_WIDTH), lambda i: (i, 0)))
        out_shape.append(jax.ShapeDtypeStruct((rows, GMLP_WIDTH), F32))
    return pl.pallas_call(
        functools.partial(_gmlp_kernel, chunk=chunk, emit_vn=emit_vn),
        grid=(rows // tr,),
        in_specs=in_specs,
        out_specs=out_specs,
        out_shape=out_shape,
        input_output_aliases=aliases,
        compiler_params=_params(1, 40),
        name="gmlp_gate_vn" if emit_vn else "gmlp_gate",
    )(*args)


def _swap_halves(w):
    half = w.shape[-1] // 2
    return jnp.concatenate([w[..., half:], w[..., :half]], axis=-1)


def _rope_tables():
    pos = jnp.concatenate([jnp.arange(SEQ, dtype=jnp.int32),
                           jnp.tile(PAST_LEN + jnp.arange(DEC_SEQ, dtype=jnp.int32), DEC_BATCH)])
    inv = 1.0 / (ROPE_THETA ** (jnp.arange(0, QK_ROPE, 2, dtype=F32) / QK_ROPE))
    ang = pos.astype(F32)[:, None] * inv[None, :]
    cos, sin = jnp.cos(ang), jnp.sin(ang)
    zero = jnp.zeros((M_ROWS, LANES - QK_ROPE), F32)
    return (jnp.concatenate([cos, cos, zero], axis=1), jnp.concatenate([-sin, sin, zero], axis=1))


def kernel(x_prompt, x_sample, cache_mla_latent, cache_mla_krope, c_prompt, c_sample, w_ada, b_ada, ffn1_w_gate_up, ffn1_w_down, ln1_g, ln1_b, w_in, gmlp_ln_g, gmlp_ln_b, gmlp_w_s, gmlp_b_s, mla_q_norm_g, mla_w_uq, mla_kv_norm_g, mla_w_ukv, w_out, ln2_g, ln2_b, ffn2_w_gate_up, ffn2_w_down, ln3_g, ln3_b):
    wukv = mla_w_ukv.astype(BF16)
    wvt = mla_w_ukv.reshape(KV_RANK, HEADS, HEAD_PAD)[:, :, QK_NOPE:].transpose(1, 2, 0).astype(BF16)
    w_in_t = w_in.T
    w_kr_t = w_in_t[IN_COLS:]
    wkr = jnp.concatenate([w_kr_t, w_kr_t[QK_ROPE // 2:], w_kr_t[:QK_ROPE // 2]], axis=0)
    uq = mla_w_uq.reshape(Q_RANK, HEADS, QK_NOPE + QK_ROPE)
    wuq = jnp.concatenate([uq, _swap_halves(uq[..., QK_NOPE:])], axis=-1).reshape(Q_RANK, HEADS * HEAD_PAD).astype(BF16)
    b_s3 = gmlp_b_s[:, :, None]
    cc, ss = _rope_tables()

    c16 = jnp.concatenate([c_prompt, c_sample, jnp.zeros((C_ROWS - 1 - DEC_BATCH, D_MODEL), F32)], axis=0)
    n_first = 3
    first = _segment_table(_modulation(c16, w_ada, b_ada, 0, n_first))
    SH1, SC1, G1 = ((first, i) for i in range(n_first))

    x0 = (x_prompt.reshape(SEQ, D_MODEL), x_sample.reshape(S_ROWS, D_MODEL))

    h1 = _modulate(x0, SC1, SH1)
    z1, mod_rest = _ffn_down(*_swiglu_up(h1, ffn1_w_gate_up, ffn1_w_down), x0, G1,
                             adaln=(c16, w_ada, b_ada, n_first, N_MOD - n_first))
    rest = _segment_table(mod_rest)
    SH2, SC2, G2, SH3, SC3, G3 = ((rest, i) for i in range(N_MOD - n_first))
    x1, h2 = _layernorm(z1, ln1_g, ln1_b, SC2, SH2)

    proj = _matmul_nt(h2, w_in_t, TM, IN_TILE, "in_proj", n_cols=IN_COLS)
    (a_mix,) = _gmlp(proj, gmlp_ln_g, gmlp_ln_b, gmlp_w_s, b_s3, 0, SEQ, 512, GMLP_CHUNK)
    a_mix, vn_s = _gmlp(proj, gmlp_ln_g, gmlp_ln_b, gmlp_w_s, b_s3, SEQ, S_ROWS, DEC_SEQ, DEC_SEQ, a_prev=a_mix)
    q = _q_proj(proj, mla_q_norm_g, wuq, cc, ss)
    lat_p, kr_p, kr128_p = _lat_krope(proj, h2, wkr, mla_kv_norm_g, cc, ss, 0, SEQ)
    lat_s, kr_s, kr128_s = _lat_krope(proj, h2, wkr, mla_kv_norm_g, cc, ss, SEQ, S_ROWS)

    k_p, vt_p = _kv_proj_t(lat_p, kr128_p, wukv, wvt, ATTN_TILE)
    o_mix = _attn_prompt(q, k_p, vt_p)

    pad_rows = KV_PAD - PAST_LEN - DEC_SEQ
    lat_all = jnp.concatenate([cache_mla_latent, lat_s.reshape(DEC_BATCH, DEC_SEQ, KV_RANK),
                               jnp.zeros((DEC_BATCH, pad_rows, KV_RANK), F32)], axis=1)
    cache_kr128 = jnp.pad(cache_mla_krope, ((0, 0), (0, 0), (0, LANES - QK_ROPE)))
    kr_all = jnp.concatenate([cache_kr128, kr128_s.reshape(DEC_BATCH, DEC_SEQ, LANES),
                              jnp.zeros((DEC_BATCH, pad_rows, LANES), F32)], axis=1)
    o_mix = _attn_sample(q, lat_all, kr_all, wukv, o_mix)

    z2 = _out_proj(a_mix, o_mix, w_out, x1, G2)
    x2, h3 = _layernorm(z2, ln2_g, ln2_b, SC3, SH3)

    (z3,) = _ffn_down(*_swiglu_up(h3, ffn2_w_gate_up, ffn2_w_down), (x2,), G3)
    (y_p,) = _layernorm(z3, ln3_g, ln3_b, row0=0, rows=SEQ)
    (y_s,) = _layernorm(z3, ln3_g, ln3_b, row0=SEQ, rows=S_ROWS)

    return (y_p.reshape(1, SEQ, D_MODEL),
            y_s.reshape(DEC_BATCH, DEC_SEQ, D_MODEL),
            lat_p.reshape(1, SEQ, KV_RANK),
            kr_p.reshape(1, SEQ, QK_ROPE),
            lat_s.reshape(DEC_BATCH, DEC_SEQ, KV_RANK),
            kr_s.reshape(DEC_BATCH, DEC_SEQ, QK_ROPE),
            vn_s.reshape(DEC_BATCH, DEC_SEQ, GMLP_WIDTH))
```

```python
import functools

import jax
import jax.numpy as jnp
from jax import lax
from jax.experimental import pallas as pl
from jax.experimental.pallas import tpu as pltpu

F32 = jnp.float32
BF16 = jnp.bfloat16

D_MODEL = 4096
SEQ = 8192
DEC_BATCH = 8
DEC_SEQ = 64
PAST_LEN = 2048
CHUNK = 64
CHUNK_SHIFT = 6
GMLP_CHUNK = 128
GROUP_DIM = 128
GMLP_WIDTH = D_MODEL // 2
GMLP_GROUPS = GMLP_WIDTH // GROUP_DIM
HEAD_V = 128
QK_NOPE = 128
QK_ROPE = 64
HEADS = (D_MODEL - GMLP_WIDTH) // HEAD_V
KV_RANK = 512
Q_RANK = D_MODEL // 4
D_FF = 256 * ((8 * D_MODEL // 3 + 255) // 256)
N_MOD = 9
ROPE_THETA = 10000.0
LN_EPS = 1e-5
RMS_EPS = 1e-6
ALPHA = 2.0 ** 0.25
Q_SCALE = (QK_NOPE + QK_ROPE) ** -0.5 * 1.4426950408889634
NEG = -1e30

LANES = 128
HEAD_PAD = 2 * LANES
S_ROWS = DEC_BATCH * DEC_SEQ
M_ROWS = SEQ + S_ROWS
SEG = DEC_SEQ
N_SEG = M_ROWS // SEG
TM = M_ROWS // 8
TR = 8 * SEG
IN_COLS = 2 * GMLP_WIDTH + Q_RANK + KV_RANK
IN_TILE = 512
KV_PAD = 17 * LANES
MIB = 2 ** 20


def _params(n_axes, vmem_mib):
    return pltpu.CompilerParams(dimension_semantics=("arbitrary",) * n_axes,
                                vmem_limit_bytes=vmem_mib * MIB)


C_ROWS = 16


def _adaln_tile(c_ref, w_ref, b_ref):
    c = c_ref[...]
    s = (c * jax.nn.sigmoid(c)).astype(BF16)
    return lax.dot_general(s, w_ref[...], (((1,), (0,)), ((), ())), preferred_element_type=F32) + b_ref[...]


def _adaln_specs(tn, comp0, block_of_step):
    per_comp = D_MODEL // tn

    def col(*idx):
        return comp0 * per_comp + block_of_step(*idx)

    in_specs = [pl.BlockSpec((C_ROWS, D_MODEL), lambda *idx: (0, 0)),
                pl.BlockSpec((D_MODEL, tn), lambda *idx: (0, col(*idx))),
                pl.BlockSpec((1, tn), lambda *idx: (0, col(*idx)))]
    out_spec = pl.BlockSpec((1, C_ROWS, tn),
                            lambda *idx: (block_of_step(*idx) // per_comp, 0, block_of_step(*idx) % per_comp))
    return in_specs, out_spec


def _mod_kernel(c_ref, w_ref, b_ref, o_ref):
    o_ref[0] = _adaln_tile(c_ref, w_ref, b_ref)


def _modulation(c16, w_ada, b_ada, comp0, n_comp):
    tn = 512
    in_specs, out_spec = _adaln_specs(tn, comp0, lambda j: j)
    return pl.pallas_call(
        _mod_kernel,
        grid=(n_comp * D_MODEL // tn,),
        in_specs=in_specs,
        out_specs=out_spec,
        out_shape=jax.ShapeDtypeStruct((n_comp, C_ROWS, D_MODEL), F32),
        compiler_params=_params(1, 40),
        name="adaln_mod",
    )(c16, w_ada, b_ada.reshape(1, -1))


def _segment_table(mod):
    n = mod.shape[0]
    return jnp.concatenate([jnp.broadcast_to(mod[:, 0:1], (n, SEQ // SEG, D_MODEL)), mod[:, 1:1 + DEC_BATCH]], axis=1)


def _seg_operand(entry, tile_rows, width, index_map):
    table, comp = entry
    view = table.reshape(table.shape[0], M_ROWS // tile_rows, tile_rows // SEG, D_MODEL)
    spec = pl.BlockSpec((1, 1, tile_rows // SEG, width), lambda *idx: (comp,) + tuple(index_map(*idx)))
    return view, spec


def _split_x_specs(tile_rows, width, col):
    n_prompt = SEQ // tile_rows
    return [pl.BlockSpec((tile_rows, width), lambda *idx: (jnp.minimum(idx[0], n_prompt - 1), col(*idx))),
            pl.BlockSpec((tile_rows, width), lambda *idx: (jnp.maximum(idx[0] - n_prompt, 0), col(*idx)))]


def _on_owner(x_refs, tile_rows, body):
    if len(x_refs) == 1:
        body(x_refs[0])
    else:
        i = pl.program_id(0)
        n_prompt = SEQ // tile_rows
        pl.when(i < n_prompt)(lambda: body(x_refs[0]))
        pl.when(i >= n_prompt)(lambda: body(x_refs[1]))


def _modulate_kernel(xp_ref, xs_ref, sc_ref, sh_ref, h_ref):
    def body(x_ref):
        for s in range(TR // SEG):
            rows = slice(s * SEG, (s + 1) * SEG)
            h = x_ref[rows, :] * (1.0 + sc_ref[0, 0, s:s + 1, :]) + sh_ref[0, 0, s:s + 1, :]
            h_ref[rows, :] = h.astype(h_ref.dtype)

    _on_owner((xp_ref, xs_ref), TR, body)


def _modulate(xs, sc, sh):
    sc_arr, sc_spec = _seg_operand(sc, TR, D_MODEL, lambda i: (i, 0, 0))
    sh_arr, sh_spec = _seg_operand(sh, TR, D_MODEL, lambda i: (i, 0, 0))
    return pl.pallas_call(
        _modulate_kernel,
        grid=(M_ROWS // TR,),
        in_specs=_split_x_specs(TR, D_MODEL, lambda i: 0) + [sc_spec, sh_spec],
        out_specs=pl.BlockSpec((TR, D_MODEL), lambda i: (i, 0)),
        out_shape=jax.ShapeDtypeStruct((M_ROWS, D_MODEL), BF16),
        compiler_params=_params(1, 48),
        name="modulate",
    )(*xs, sc_arr, sh_arr)


def _raw_residual(*xs):
    return ("raw",) + xs


def _ln_residual(z, mu, rstd, ln_g, ln_b):
    return ("ln", z, mu, rstd, ln_g.reshape(1, -1), ln_b.reshape(1, -1))


def _residual_operands(res, tm, tn):
    tile = pl.BlockSpec((tm, tn), lambda i, j: (i, j))
    if res[0] == "raw":
        xs = list(res[1:])
        return xs, (_split_x_specs(tm, tn, lambda i, j: j) if len(xs) == 2 else [tile])
    stat = pl.BlockSpec((tm, LANES), lambda i, j: (i, 0))
    vec = pl.BlockSpec((1, tn), lambda i, j: (0, j))
    return list(res[1:]), [tile, stat, stat, vec, vec]


def _residual_rows(kind, refs, tile_rows, rows):
    if kind == "ln":
        z_ref, mu_ref, rstd_ref, g_ref, b_ref = refs
        return (z_ref[rows, :] - mu_ref[rows, :1]) * rstd_ref[rows, :1] * g_ref[...] + b_ref[...]
    x = refs[0][rows, :]
    if len(refs) == 2:
        x = jnp.where(pl.program_id(0) < SEQ // tile_rows, x, refs[1][rows, :])
    return x


def _gated_residual(kind, x_refs, tile_rows, acc, gate_ref, gate_scale, z_ref):
    for s in range(acc.shape[0] // SEG):
        rows = slice(s * SEG, (s + 1) * SEG)
        x = _residual_rows(kind, x_refs, tile_rows, rows)
        z_ref[rows, :] = ALPHA * x + (gate_scale * gate_ref[0, 0, s:s + 1, :]) * acc[rows, :]


def _ln_kernel(z_ref, g_ref, b_ref, *rest, with_h):
    if with_h:
        sc_ref, sh_ref, h_ref, mu_ref, rstd_ref = rest
    else:
        (xo_ref,) = rest
    for s in range(TR // SEG):
        rows = slice(s * SEG, (s + 1) * SEG)
        z = z_ref[rows, :]
        mu = jnp.mean(z, axis=-1, keepdims=True)
        zc = z - mu
        var = jnp.mean(zc * zc, axis=-1, keepdims=True)
        rstd = lax.rsqrt(var + LN_EPS)
        xn = zc * rstd * g_ref[...] + b_ref[...]
        if with_h:
            h = xn * (1.0 + sc_ref[0, 0, s:s + 1, :]) + sh_ref[0, 0, s:s + 1, :]
            h_ref[rows, :] = h.astype(h_ref.dtype)
            mu_ref[rows, :] = jnp.broadcast_to(mu, (SEG, LANES))
            rstd_ref[rows, :] = jnp.broadcast_to(rstd, (SEG, LANES))
        else:
            xo_ref[rows, :] = xn


def _layernorm(z, ln_g, ln_b, sc=None, sh=None, *, row0=0, rows=M_ROWS):
    with_h = sc is not None
    off = row0 // TR
    vec = pl.BlockSpec((1, D_MODEL), lambda i: (0, 0))
    in_specs = [pl.BlockSpec((TR, D_MODEL), lambda i: (i + off, 0)), vec, vec]
    args = [z, ln_g.reshape(1, D_MODEL), ln_b.reshape(1, D_MODEL)]
    row_out = pl.BlockSpec((TR, D_MODEL), lambda i: (i, 0))
    if with_h:
        for entry in (sc, sh):
            arr, spec = _seg_operand(entry, TR, D_MODEL, lambda i: (i + off, 0, 0))
            in_specs.append(spec)
            args.append(arr)
        stat = pl.BlockSpec((TR, LANES), lambda i: (i, 0))
        out_specs = [row_out, stat, stat]
        out_shape = [jax.ShapeDtypeStruct((rows, D_MODEL), BF16),
                     jax.ShapeDtypeStruct((rows, LANES), F32), jax.ShapeDtypeStruct((rows, LANES), F32)]
    else:
        out_specs = [row_out]
        out_shape = [jax.ShapeDtypeStruct((rows, D_MODEL), F32)]
    return pl.pallas_call(
        functools.partial(_ln_kernel, with_h=with_h),
        grid=(rows // TR,),
        in_specs=in_specs,
        out_specs=out_specs,
        out_shape=out_shape,
        compiler_params=_params(1, 56),
        name="ln_mod" if with_h else "ln",
    )(*args)


def _dot(a, b):
    return lax.dot_general(a, b, (((1,), (0,)), ((), ())), preferred_element_type=F32)


def _gateup_kernel(h_ref, wg_ref, wu_ref, wd_ref, o_ref, wd_out_ref):
    h = h_ref[...]
    g = _dot(h, wg_ref[...])
    u = _dot(h, wu_ref[...])
    o_ref[...] = (g * jax.nn.sigmoid(g) * u).astype(o_ref.dtype)

    @pl.when(pl.program_id(0) == 0)
    def _():
        wd_out_ref[...] = wd_ref[...].astype(wd_out_ref.dtype)


def _swiglu_up(h, wgu, w_down):
    tm = 2 * TM
    tn = 256
    nj = D_FF // tn
    slab = pl.BlockSpec((tn, D_MODEL), lambda i, j: (jnp.where(i == 0, j, nj - 1), 0))
    return pl.pallas_call(
        _gateup_kernel,
        grid=(M_ROWS // tm, nj),
        in_specs=[pl.BlockSpec((tm, D_MODEL), lambda i, j: (i, 0), pipeline_mode=pl.Buffered(1)),
                  pl.BlockSpec((D_MODEL, tn), lambda i, j: (0, j)),
                  pl.BlockSpec((D_MODEL, tn), lambda i, j: (0, j + nj)),
                  slab],
        out_specs=[pl.BlockSpec((tm, tn), lambda i, j: (i, j)), slab],
        out_shape=[jax.ShapeDtypeStruct((M_ROWS, D_FF), BF16),
                   jax.ShapeDtypeStruct((D_FF, D_MODEL), BF16)],
        compiler_params=_params(2, 56),
        name="swiglu_up",
    )(h, wgu, wgu, w_down)


def _mm_nt_kernel(a_ref, bt_ref, o_ref):
    o_ref[...] = lax.dot_general(a_ref[...], bt_ref[...], (((1,), (1,)), ((), ())), preferred_element_type=F32)


def _matmul_nt(a, bt, tm, tn, name, n_cols):
    m, k = a.shape
    return pl.pallas_call(
        _mm_nt_kernel,
        grid=(m // tm, n_cols // tn),
        in_specs=[pl.BlockSpec((tm, k), lambda i, j: (i, 0)),
                  pl.BlockSpec((tn, k), lambda i, j: (j, 0))],
        out_specs=pl.BlockSpec((tm, tn), lambda i, j: (i, j)),
        out_shape=jax.ShapeDtypeStruct((m, n_cols), F32),
        compiler_params=_params(2, 48),
        name=name,
    )(a, bt)


def _down_kernel(*refs, res_kind, n_res, tile_rows, gate_scale, with_adaln):
    a_ref, b_ref = refs[:2]
    x_refs = refs[2:2 + n_res]
    gate_ref = refs[2 + n_res]
    if with_adaln:
        c_ref, w_ref, bias_ref, z_ref, mod_ref = refs[3 + n_res:]
    else:
        (z_ref,) = refs[3 + n_res:]
    acc = _dot(a_ref[...], b_ref[...])
    _gated_residual(res_kind, x_refs, tile_rows, acc, gate_ref, gate_scale, z_ref)
    if with_adaln:
        mod_ref[0] = _adaln_tile(c_ref, w_ref, bias_ref)


def _ffn_down(a, w_down_bf16, res, gate, adaln=None):
    tm, tn = TR, 512
    m, k = a.shape
    n = w_down_bf16.shape[1]
    grid = (m // tm, n // tn)
    gate_arr, gate_spec = _seg_operand(gate, tm, tn, lambda i, j: (i, 0, j))
    res_args, res_specs = _residual_operands(res, tm, tn)
    in_specs = [pl.BlockSpec((tm, k), lambda i, j: (i, 0)),
                pl.BlockSpec((k, tn), lambda i, j: (0, j))] + res_specs + [gate_spec]
    args = [a, w_down_bf16, *res_args, gate_arr]
    out_specs = [pl.BlockSpec((tm, tn), lambda i, j: (i, j))]
    out_shape = [jax.ShapeDtypeStruct((m, n), F32)]
    if adaln is not None:
        c16, w_ada, b_ada, comp0, n_comp = adaln
        side_tn = 256
        n_side = n_comp * D_MODEL // side_tn
        assert n_side <= grid[0] * grid[1]
        side_in, side_out = _adaln_specs(side_tn, comp0, lambda i, j: jnp.minimum(i * grid[1] + j, n_side - 1))
        in_specs += side_in
        args += [c16, w_ada, b_ada.reshape(1, -1)]
        out_specs.append(side_out)
        out_shape.append(jax.ShapeDtypeStruct((n_comp, C_ROWS, D_MODEL), F32))
    return pl.pallas_call(
        functools.partial(_down_kernel, res_kind=res[0], n_res=len(res_args), tile_rows=tm, gate_scale=0.5,
                          with_adaln=adaln is not None),
        grid=grid,
        in_specs=in_specs,
        out_specs=out_specs,
        out_shape=out_shape,
        compiler_params=_params(2, 60),
        name="ffn_down" if adaln is None else "ffn_down_adaln",
    )(*args)


def _out_proj_kernel(a1_ref, a2_ref, b_ref, *refs, res_kind):
    x_refs, gate_ref, z_ref = refs[:-2], refs[-2], refs[-1]
    k1 = a1_ref.shape[1]
    acc = _dot(a1_ref[...], b_ref[:k1, :])
    acc += _dot(a2_ref[...], b_ref[k1:, :])
    _gated_residual(res_kind, x_refs, TM, acc, gate_ref, 1.0, z_ref)


def _out_proj(a1, a2, b, res, gate):
    tm, tn = TM, 512
    m, k1 = a1.shape
    k2 = a2.shape[1]
    n = b.shape[1]
    gate_arr, gate_spec = _seg_operand(gate, tm, tn, lambda i, j: (i, 0, j))
    res_args, res_specs = _residual_operands(res, tm, tn)
    return pl.pallas_call(
        functools.partial(_out_proj_kernel, res_kind=res[0]),
        grid=(m // tm, n // tn),
        in_specs=[pl.BlockSpec((tm, k1), lambda i, j: (i, 0)),
                  pl.BlockSpec((tm, k2), lambda i, j: (i, 0)),
                  pl.BlockSpec((k1 + k2, tn), lambda i, j: (0, j))] + res_specs + [gate_spec],
        out_specs=pl.BlockSpec((tm, tn), lambda i, j: (i, j)),
        out_shape=jax.ShapeDtypeStruct((m, n), F32),
        compiler_params=_params(2, 56),
        name="out_proj",
    )(a1, a2, b, *res_args, gate_arr)


def _rope128(blk, cc, ss):
    return blk * cc + pltpu.roll(blk, QK_ROPE, 1) * ss


def _q_kernel(cq_ref, g_ref, w_ref, cc_ref, ss_ref, o_ref):
    x = cq_ref[...]
    r = lax.rsqrt(jnp.mean(x * x, axis=-1, keepdims=True) + RMS_EPS)
    a = (x * r * g_ref[...]).astype(BF16)
    cc = cc_ref[...]
    ss = ss_ref[...]
    for h in range(HEADS):
        acc = _dot(a, w_ref[:, h * HEAD_PAD:(h + 1) * HEAD_PAD])
        o_ref[h, :, :LANES] = (acc[:, :LANES] * Q_SCALE).astype(o_ref.dtype)
        o_ref[h, :, LANES:] = (_rope128(acc[:, LANES:], cc, ss) * Q_SCALE).astype(o_ref.dtype)


def _q_proj(proj, q_norm_g, wuq, cc, ss):
    tm = TM // 2
    cq_block = (2 * GMLP_WIDTH) // Q_RANK
    return pl.pallas_call(
        _q_kernel,
        grid=(M_ROWS // tm,),
        in_specs=[pl.BlockSpec((tm, Q_RANK), lambda i: (i, cq_block)),
                  pl.BlockSpec((1, Q_RANK), lambda i: (0, 0)),
                  pl.BlockSpec((Q_RANK, HEADS * HEAD_PAD), lambda i: (0, 0)),
                  pl.BlockSpec((tm, LANES), lambda i: (i, 0)),
                  pl.BlockSpec((tm, LANES), lambda i: (i, 0))],
        out_specs=pl.BlockSpec((HEADS, tm, HEAD_PAD), lambda i: (0, i, 0)),
        out_shape=jax.ShapeDtypeStruct((HEADS, M_ROWS, HEAD_PAD), BF16),
        compiler_params=_params(1, 40),
        name="q_proj",
    )(proj, q_norm_g.reshape(1, Q_RANK), wuq, cc, ss)


def _latkr_kernel(ckv_ref, h_ref, wkr_ref, g_ref, cc_ref, ss_ref, lat_ref, kr_out_ref, kr128_ref):
    x = ckv_ref[...]
    r = lax.rsqrt(jnp.mean(x * x, axis=-1, keepdims=True) + RMS_EPS)
    lat_ref[...] = x * r * g_ref[...]
    blk = lax.dot_general(h_ref[...], wkr_ref[...], (((1,), (1,)), ((), ())), preferred_element_type=F32)
    rot = _rope128(blk, cc_ref[...], ss_ref[...])
    kr_out_ref[...] = rot[:, :QK_ROPE]
    kr128_ref[...] = rot


def _lat_krope(proj, h, wkr, kv_norm_g, cc, ss, row0, rows):
    tr = 512
    off = row0 // tr
    ckv_block = (2 * GMLP_WIDTH + Q_RANK) // KV_RANK
    return pl.pallas_call(
        _latkr_kernel,
        grid=(rows // tr,),
        in_specs=[pl.BlockSpec((tr, KV_RANK), lambda i: (i + off, ckv_block)),
                  pl.BlockSpec((tr, D_MODEL), lambda i: (i + off, 0)),
                  pl.BlockSpec((LANES, D_MODEL), lambda i: (0, 0)),
                  pl.BlockSpec((1, KV_RANK), lambda i: (0, 0)),
                  pl.BlockSpec((tr, LANES), lambda i: (i + off, 0)),
                  pl.BlockSpec((tr, LANES), lambda i: (i + off, 0))],
        out_specs=[pl.BlockSpec((tr, KV_RANK), lambda i: (i, 0)),
                   pl.BlockSpec((tr, QK_ROPE), lambda i: (i, 0)),
                   pl.BlockSpec((tr, LANES), lambda i: (i, 0))],
        out_shape=[jax.ShapeDtypeStruct((rows, KV_RANK), F32),
                   jax.ShapeDtypeStruct((rows, QK_ROPE), F32),
                   jax.ShapeDtypeStruct((rows, LANES), F32)],
        compiler_params=_params(1, 32),
        name="lat_krope",
    )(proj, h, wkr, kv_norm_g.reshape(1, KV_RANK), cc, ss)


def _kv_t_kernel(lat_ref, kr_ref, w_ref, wvt_ref, k_ref, vt_ref):
    a = lat_ref[...].astype(BF16)
    kr = kr_ref[...].astype(BF16)
    for h in range(HEADS):
        kn = _dot(a, w_ref[:, h * HEAD_PAD:h * HEAD_PAD + QK_NOPE])
        k_ref[h, :, :LANES] = kn.astype(BF16)
        k_ref[h, :, LANES:] = kr
        vt = lax.dot_general(wvt_ref[h], a, (((1,), (1,)), ((), ())), preferred_element_type=F32)
        vt_ref[h, 0] = vt.astype(BF16)


def _kv_proj_t(lat, kr128, wukv, wvt, tr):
    rows = lat.shape[0]
    return pl.pallas_call(
        _kv_t_kernel,
        grid=(rows // tr,),
        in_specs=[pl.BlockSpec((tr, KV_RANK), lambda i: (i, 0)),
                  pl.BlockSpec((tr, LANES), lambda i: (i, 0)),
                  pl.BlockSpec((KV_RANK, HEADS * HEAD_PAD), lambda i: (0, 0)),
                  pl.BlockSpec((HEADS, HEAD_V, KV_RANK), lambda i: (0, 0, 0))],
        out_specs=[pl.BlockSpec((HEADS, tr, HEAD_PAD), lambda i: (0, i, 0)),
                   pl.BlockSpec((HEADS, 1, HEAD_V, tr), lambda i: (0, i, 0, 0))],
        out_shape=[jax.ShapeDtypeStruct((HEADS, rows, HEAD_PAD), BF16),
                   jax.ShapeDtypeStruct((HEADS, rows // tr, HEAD_V, tr), BF16)],
        compiler_params=_params(1, 48),
        name="kv_proj_t",
    )(lat, kr128, wukv, wvt)


ATTN_TILE = 1024


def _softmax_pv_t(s, s_max, vt, carry):
    m, l, acc = carry
    m_new = jnp.maximum(m, s_max)
    p = jnp.exp2(s - m_new)
    a = jnp.exp2(m - m_new)
    l = a * l + jnp.sum(p, axis=0, keepdims=True)
    acc = a * acc + jnp.dot(vt, p.astype(BF16), preferred_element_type=F32)
    return m_new, l, acc


def _chunk_mask_t(q_rel0, nk, nq):
    kc = lax.broadcasted_iota(jnp.int32, (nk, nq), 0) >> CHUNK_SHIFT
    qc = (q_rel0 + lax.broadcasted_iota(jnp.int32, (nk, nq), 1)) >> CHUNK_SHIFT
    return qc >= kc


def _attn_prompt_kernel(q_ref, k_ref, vt_ref, o_ref, s_even, s_odd, *, rb, nh):
    qi = pl.program_id(1)
    n_rb = ATTN_TILE // rb
    heads = range(nh)

    def scores(j, s_ref):
        k0 = pl.multiple_of(j * ATTN_TILE, ATTN_TILE)
        maxima = []
        for hh in heads:
            s = lax.dot_general(k_ref[hh, pl.ds(k0, ATTN_TILE), :], q_ref[hh], (((1,), (1,)), ((), ())),
                                preferred_element_type=F32)
            s_ref[hh] = s
            maxima.append(jnp.max(s, axis=0, keepdims=True))
        return tuple(maxima)

    def make_step(s_cur, s_next):
        def step(j, carry):
            stats, s_max = carry
            next_max = scores(j + 1, s_next)
            stats = tuple(
                tuple(_softmax_pv_t(s_cur[hh, :, r * rb:(r + 1) * rb], s_max[hh][:, r * rb:(r + 1) * rb],
                                    vt_ref[hh, j], stats[hh][r]) for r in range(n_rb))
                for hh in heads)
            return stats, next_max
        return step

    step_even, step_odd = make_step(s_even, s_odd), make_step(s_odd, s_even)

    def finish(s_ref, stats):
        mask = _chunk_mask_t(0, rb, rb)
        for r in range(n_rb):
            nk = (r + 1) * rb
            cols = slice(r * rb, nk)
            for hh in heads:
                s_r = jnp.where(mask, s_ref[hh, r * rb:nk, cols], NEG)
                if r > 0:
                    s_r = jnp.concatenate([s_ref[hh, :r * rb, cols], s_r], axis=0)
                _, l, acc = _softmax_pv_t(s_r, jnp.max(s_r, axis=0, keepdims=True), vt_ref[hh, qi, :, :nk],
                                          stats[hh][r])
                o_ref[r * rb:(r + 1) * rb, hh * HEAD_V:(hh + 1) * HEAD_V] = (acc / l).T.astype(o_ref.dtype)

    init = tuple(tuple((jnp.full((1, rb), NEG, F32), jnp.zeros((1, rb), F32), jnp.zeros((HEAD_V, rb), F32))
                       for _ in range(n_rb)) for _ in heads)
    carry = (init, scores(0, s_even))
    carry = lax.fori_loop(0, qi >> 1, lambda p, c: step_odd(2 * p + 1, step_even(2 * p, c)), carry)
    odd = (qi & 1) == 1
    stats, _ = lax.cond(odd, lambda: step_even(qi - 1, carry), lambda: carry)
    lax.cond(odd, lambda: finish(s_odd, stats), lambda: finish(s_even, stats))


def _attn_prompt(q, k, vt):
    nh = 2
    return pl.pallas_call(
        functools.partial(_attn_prompt_kernel, rb=512, nh=nh),
        grid=(HEADS // nh, SEQ // ATTN_TILE),
        in_specs=[pl.BlockSpec((nh, ATTN_TILE, HEAD_PAD), lambda h, i: (h, i, 0)),
                  pl.BlockSpec((nh, SEQ, HEAD_PAD), lambda h, i: (h, 0, 0)),
                  pl.BlockSpec((nh, SEQ // ATTN_TILE, HEAD_V, ATTN_TILE), lambda h, i: (h, 0, 0, 0))],
        out_specs=pl.BlockSpec((ATTN_TILE, nh * HEAD_V), lambda h, i: (i, h)),
        out_shape=jax.ShapeDtypeStruct((M_ROWS, HEADS * HEAD_V), BF16),
        scratch_shapes=[pltpu.VMEM((nh, ATTN_TILE, ATTN_TILE), F32), pltpu.VMEM((nh, ATTN_TILE, ATTN_TILE), F32)],
        compiler_params=_params(2, 56),
        name="attn_prompt",
    )(q, k, vt)


def _attn_sample_kernel(q_ref, cache_ref, new_ref, kr_ref, w_ref, o_prev_ref, o_ref, qa_scr):
    del o_prev_ref
    nt = (((1,), (1,)), ((), ()))
    pad = jnp.zeros((KV_PAD - PAST_LEN - DEC_SEQ, KV_RANK), F32)
    lat = jnp.concatenate([cache_ref[0], new_ref[...], pad], axis=0).astype(BF16)
    keys = jnp.concatenate([lat, kr_ref[0].astype(BF16)], axis=1)
    for h in range(HEADS):
        rows = slice(h * DEC_SEQ, (h + 1) * DEC_SEQ)
        w_uk = w_ref[:, h * HEAD_PAD:h * HEAD_PAD + QK_NOPE]
        q_lat = lax.dot_general(q_ref[h, :, :LANES], w_uk, nt, preferred_element_type=F32)
        qa_scr[rows, :KV_RANK] = q_lat.astype(BF16)
        qa_scr[rows, KV_RANK:] = q_ref[h, :, LANES:]
    s = lax.dot_general(qa_scr[...], keys, nt, preferred_element_type=F32)
    shape = (HEADS * DEC_SEQ, KV_PAD)
    q_pos = PAST_LEN + (lax.broadcasted_iota(jnp.int32, shape, 0) & (DEC_SEQ - 1))
    k_pos = lax.broadcasted_iota(jnp.int32, shape, 1)
    visible = ((q_pos >> CHUNK_SHIFT) >= (k_pos >> CHUNK_SHIFT)) & (k_pos < PAST_LEN + DEC_SEQ)
    s = jnp.where(visible, s, NEG)
    m = jnp.max(s, axis=-1, keepdims=True)
    p = jnp.exp2(s - m)
    l = jnp.sum(p, axis=-1, keepdims=True)
    o_lat = (jnp.dot(p.astype(BF16), lat, preferred_element_type=F32) / l).astype(BF16)
    for h in range(HEADS):
        w_uv = w_ref[:, h * HEAD_PAD + QK_NOPE:(h + 1) * HEAD_PAD]
        o_h = _dot(o_lat[h * DEC_SEQ:(h + 1) * DEC_SEQ, :], w_uv)
        o_ref[:, h * HEAD_V:(h + 1) * HEAD_V] = o_h.astype(o_ref.dtype)


def _attn_sample(q, cache_lat, lat_new, kr_all, wukv, o_prev):
    q_off = SEQ // DEC_SEQ
    return pl.pallas_call(
        _attn_sample_kernel,
        grid=(DEC_BATCH,),
        in_specs=[pl.BlockSpec((HEADS, DEC_SEQ, HEAD_PAD), lambda b: (0, q_off + b, 0)),
                  pl.BlockSpec((1, PAST_LEN, KV_RANK), lambda b: (b, 0, 0)),
                  pl.BlockSpec((DEC_SEQ, KV_RANK), lambda b: (b, 0)),
                  pl.BlockSpec((1, KV_PAD, LANES), lambda b: (b, 0, 0)),
                  pl.BlockSpec((KV_RANK, HEADS * HEAD_PAD), lambda b: (0, 0)),
                  pl.BlockSpec(memory_space=pl.ANY)],
        out_specs=pl.BlockSpec((DEC_SEQ, HEADS * HEAD_V), lambda b: (q_off + b, 0)),
        out_shape=jax.ShapeDtypeStruct((M_ROWS, HEADS * HEAD_V), BF16),
        scratch_shapes=[pltpu.VMEM((HEADS * DEC_SEQ, KV_RANK + LANES), BF16)],
        input_output_aliases={5: 0},
        compiler_params=_params(1, 48),
        name="attn_sample",
    )(q, cache_lat, lat_new, kr_all, wukv, o_prev)


def _gmlp_kernel(u_ref, v_ref, g_ref, b_ref, w_ref, bs_ref, *rest, chunk, emit_vn):
    if emit_vn:
        _, a_ref, vn_ref = rest
    else:
        (a_ref,) = rest
    v = v_ref[...]
    mu = jnp.mean(v, axis=-1, keepdims=True)
    vc = v - mu
    var = jnp.mean(vc * vc, axis=-1, keepdims=True)
    vn = vc * lax.rsqrt(var + LN_EPS) * g_ref[...] + b_ref[...]
    if emit_vn:
        vn_ref[...] = vn
    vnb = vn.astype(BF16)
    n_chunks = v.shape[0] // chunk
    causal = (lax.broadcasted_iota(jnp.int32, (chunk, GMLP_CHUNK), 0)
              >= lax.broadcasted_iota(jnp.int32, (chunk, GMLP_CHUNK), 1))
    for g in range(GMLP_GROUPS):
        cols = slice(g * GROUP_DIM, (g + 1) * GROUP_DIM)
        w = jnp.where(causal, w_ref[g, :chunk, :], 0.0).astype(BF16)
        rhs = jnp.concatenate([vnb[c * chunk:(c + 1) * chunk, cols] for c in range(n_chunks)], axis=1)
        if chunk < GMLP_CHUNK:
            rhs = jnp.concatenate([rhs, jnp.zeros((GMLP_CHUNK - chunk, rhs.shape[1]), BF16)], axis=0)
        mixed = jnp.dot(w, rhs, preferred_element_type=F32) + bs_ref[g, :chunk, :]
        for c in range(n_chunks):
            rows = slice(c * chunk, (c + 1) * chunk)
            gate = mixed[:, c * GROUP_DIM:(c + 1) * GROUP_DIM]
            a_ref[rows, cols] = (u_ref[rows, cols] * gate).astype(a_ref.dtype)


def _gmlp(proj, ln_g, ln_b, w_s, b_s3, row0, rows, tr, chunk, a_prev=None):
    emit_vn = a_prev is not None
    off = row0 // tr
    vec = pl.BlockSpec((1, GMLP_WIDTH), lambda i: (0, 0))
    in_specs = [pl.BlockSpec((tr, GMLP_WIDTH), lambda i: (i + off, 0)),
                pl.BlockSpec((tr, GMLP_WIDTH), lambda i: (i + off, 1)),
                vec, vec,
                pl.BlockSpec((GMLP_GROUPS, GMLP_CHUNK, GMLP_CHUNK), lambda i: (0, 0, 0)),
                pl.BlockSpec((GMLP_GROUPS, GMLP_CHUNK, 1), lambda i: (0, 0, 0))]
    args = [proj, proj, ln_g.reshape(1, GMLP_WIDTH), ln_b.reshape(1, GMLP_WIDTH), w_s, b_s3]
    out_specs = [pl.BlockSpec((tr, GMLP_WIDTH), lambda i: (i + off, 0))]
    out_shape = [jax.ShapeDtypeStruct((M_ROWS, GMLP_WIDTH), BF16)]
    aliases = {}
    if emit_vn:
        in_specs.append(pl.BlockSpec(memory_space=pl.ANY))
        args.append(a_prev)
        aliases = {len(args) - 1: 0}
        out_specs.append(pl.BlockSpec((tr, GMLP_WIDTH), lambda i: (i, 0)))
        out_shape.append(jax.ShapeDtypeStruct((rows, GMLP_WIDTH), F32))
    return pl.pallas_call(
        functools.partial(_gmlp_kernel, chunk=chunk, emit_vn=emit_vn),
        grid=(rows // tr,),
        in_specs=in_specs,
        out_specs=out_specs,
        out_shape=out_shape,
        input_output_aliases=aliases,
        compiler_params=_params(1, 40),
        name="gmlp_gate_vn" if emit_vn else "gmlp_gate",
    )(*args)


def _swap_halves(w):
    half = w.shape[-1] // 2
    return jnp.concatenate([w[..., half:], w[..., :half]], axis=-1)


def _rope_tables():
    pos = jnp.concatenate([jnp.arange(SEQ, dtype=jnp.int32),
                           jnp.tile(PAST_LEN + jnp.arange(DEC_SEQ, dtype=jnp.int32), DEC_BATCH)])
    inv = 1.0 / (ROPE_THETA ** (jnp.arange(0, QK_ROPE, 2, dtype=F32) / QK_ROPE))
    ang = pos.astype(F32)[:, None] * inv[None, :]
    cos, sin = jnp.cos(ang), jnp.sin(ang)
    zero = jnp.zeros((M_ROWS, LANES - QK_ROPE), F32)
    return (jnp.concatenate([cos, cos, zero], axis=1), jnp.concatenate([-sin, sin, zero], axis=1))


def kernel(x_prompt, x_sample, cache_mla_latent, cache_mla_krope, c_prompt, c_sample, w_ada, b_ada, ffn1_w_gate_up, ffn1_w_down, ln1_g, ln1_b, w_in, gmlp_ln_g, gmlp_ln_b, gmlp_w_s, gmlp_b_s, mla_q_norm_g, mla_w_uq, mla_kv_norm_g, mla_w_ukv, w_out, ln2_g, ln2_b, ffn2_w_gate_up, ffn2_w_down, ln3_g, ln3_b):
    wukv = mla_w_ukv.astype(BF16)
    wvt = mla_w_ukv.reshape(KV_RANK, HEADS, HEAD_PAD)[:, :, QK_NOPE:].transpose(1, 2, 0).astype(BF16)
    w_in_t = w_in.T
    w_kr_t = w_in_t[IN_COLS:]
    wkr = jnp.concatenate([w_kr_t, w_kr_t[QK_ROPE // 2:], w_kr_t[:QK_ROPE // 2]], axis=0)
    uq = mla_w_uq.reshape(Q_RANK, HEADS, QK_NOPE + QK_ROPE)
    wuq = jnp.concatenate([uq, _swap_halves(uq[..., QK_NOPE:])], axis=-1).reshape(Q_RANK, HEADS * HEAD_PAD).astype(BF16)
    b_s3 = gmlp_b_s[:, :, None]
    cc, ss = _rope_tables()

    c16 = jnp.concatenate([c_prompt, c_sample, jnp.zeros((C_ROWS - 1 - DEC_BATCH, D_MODEL), F32)], axis=0)
    n_first = 3
    first = _segment_table(_modulation(c16, w_ada, b_ada, 0, n_first))
    SH1, SC1, G1 = ((first, i) for i in range(n_first))

    x0 = (x_prompt.reshape(SEQ, D_MODEL), x_sample.reshape(S_ROWS, D_MODEL))

    h1 = _modulate(x0, SC1, SH1)
    z1, mod_rest = _ffn_down(*_swiglu_up(h1, ffn1_w_gate_up, ffn1_w_down), _raw_residual(*x0), G1,
                             adaln=(c16, w_ada, b_ada, n_first, N_MOD - n_first))
    rest = _segment_table(mod_rest)
    SH2, SC2, G2, SH3, SC3, G3 = ((rest, i) for i in range(N_MOD - n_first))
    h2, mu1, rstd1 = _layernorm(z1, ln1_g, ln1_b, SC2, SH2)

    proj = _matmul_nt(h2, w_in_t, TM, IN_TILE, "in_proj", n_cols=IN_COLS)
    (a_mix,) = _gmlp(proj, gmlp_ln_g, gmlp_ln_b, gmlp_w_s, b_s3, 0, SEQ, 512, GMLP_CHUNK)
    a_mix, vn_s = _gmlp(proj, gmlp_ln_g, gmlp_ln_b, gmlp_w_s, b_s3, SEQ, S_ROWS, DEC_SEQ, DEC_SEQ, a_prev=a_mix)
    q = _q_proj(proj, mla_q_norm_g, wuq, cc, ss)
    lat_p, kr_p, kr128_p = _lat_krope(proj, h2, wkr, mla_kv_norm_g, cc, ss, 0, SEQ)
    lat_s, kr_s, kr128_s = _lat_krope(proj, h2, wkr, mla_kv_norm_g, cc, ss, SEQ, S_ROWS)

    k_p, vt_p = _kv_proj_t(lat_p, kr128_p, wukv, wvt, ATTN_TILE)
    o_mix = _attn_prompt(q, k_p, vt_p)

    pad_rows = KV_PAD - PAST_LEN - DEC_SEQ
    cache_kr128 = jnp.pad(cache_mla_krope, ((0, 0), (0, 0), (0, LANES - QK_ROPE)))
    kr_all = jnp.concatenate([cache_kr128, kr128_s.reshape(DEC_BATCH, DEC_SEQ, LANES),
                              jnp.zeros((DEC_BATCH, pad_rows, LANES), F32)], axis=1)
    o_mix = _attn_sample(q, cache_mla_latent, lat_s, kr_all, wukv, o_mix)

    z2 = _out_proj(a_mix, o_mix, w_out, _ln_residual(z1, mu1, rstd1, ln1_g, ln1_b), G2)
    h3, mu2, rstd2 = _layernorm(z2, ln2_g, ln2_b, SC3, SH3)

    (z3,) = _ffn_down(*_swiglu_up(h3, ffn2_w_gate_up, ffn2_w_down), _ln_residual(z2, mu2, rstd2, ln2_g, ln2_b), G3)
    (y_p,) = _layernorm(z3, ln3_g, ln3_b, row0=0, rows=SEQ)
    (y_s,) = _layernorm(z3, ln3_g, ln3_b, row0=SEQ, rows=S_ROWS)

    return (y_p.reshape(1, SEQ, D_MODEL),
            y_s.reshape(DEC_BATCH, DEC_SEQ, D_MODEL),
            lat_p.reshape(1, SEQ, KV_RANK),
            kr_p.reshape(1, SEQ, QK_ROPE),
            lat_s.reshape(DEC_BATCH, DEC_SEQ, KV_RANK),
            kr_s.reshape(DEC_BATCH, DEC_SEQ, QK_ROPE),
            vn_s.reshape(DEC_BATCH, DEC_SEQ, GMLP_WIDTH))
```

```python
import functools

import jax
import jax.numpy as jnp
from jax import lax
from jax.experimental import pallas as pl
from jax.experimental.pallas import tpu as pltpu

F32 = jnp.float32
BF16 = jnp.bfloat16

D_MODEL = 4096
SEQ = 8192
DEC_BATCH = 8
DEC_SEQ = 64
PAST_LEN = 2048
CHUNK = 64
CHUNK_SHIFT = 6
GMLP_CHUNK = 128
GROUP_DIM = 128
GMLP_WIDTH = D_MODEL // 2
GMLP_GROUPS = GMLP_WIDTH // GROUP_DIM
HEAD_V = 128
QK_NOPE = 128
QK_ROPE = 64
HEADS = (D_MODEL - GMLP_WIDTH) // HEAD_V
KV_RANK = 512
Q_RANK = D_MODEL // 4
D_FF = 256 * ((8 * D_MODEL // 3 + 255) // 256)
N_MOD = 9
ROPE_THETA = 10000.0
LN_EPS = 1e-5
RMS_EPS = 1e-6
ALPHA = 2.0 ** 0.25
Q_SCALE = (QK_NOPE + QK_ROPE) ** -0.5 * 1.4426950408889634
NEG = -1e30

LANES = 128
HEAD_PAD = 2 * LANES
S_ROWS = DEC_BATCH * DEC_SEQ
M_ROWS = SEQ + S_ROWS
SEG = DEC_SEQ
N_SEG = M_ROWS // SEG
TM = M_ROWS // 8
TR = 8 * SEG
IN_COLS = 2 * GMLP_WIDTH + Q_RANK + KV_RANK
IN_TILE = 512
KV_PAD = 17 * LANES
MIB = 2 ** 20


def _params(n_axes, vmem_mib):
    return pltpu.CompilerParams(dimension_semantics=("arbitrary",) * n_axes,
                                vmem_limit_bytes=vmem_mib * MIB)


C_ROWS = 16


def _adaln_tile(c_ref, w_ref, b_ref):
    c = c_ref[...]
    s = (c * jax.nn.sigmoid(c)).astype(BF16)
    return lax.dot_general(s, w_ref[...], (((1,), (0,)), ((), ())), preferred_element_type=F32) + b_ref[...]


def _adaln_specs(tn, comp0, block_of_step):
    per_comp = D_MODEL // tn

    def col(*idx):
        return comp0 * per_comp + block_of_step(*idx)

    in_specs = [pl.BlockSpec((C_ROWS, D_MODEL), lambda *idx: (0, 0)),
                pl.BlockSpec((D_MODEL, tn), lambda *idx: (0, col(*idx))),
                pl.BlockSpec((1, tn), lambda *idx: (0, col(*idx)))]
    out_spec = pl.BlockSpec((1, C_ROWS, tn),
                            lambda *idx: (block_of_step(*idx) // per_comp, 0, block_of_step(*idx) % per_comp))
    return in_specs, out_spec


def _mod_kernel(c_ref, w_ref, b_ref, o_ref):
    o_ref[0] = _adaln_tile(c_ref, w_ref, b_ref)


def _modulation(c16, w_ada, b_ada, comp0, n_comp):
    tn = 512
    in_specs, out_spec = _adaln_specs(tn, comp0, lambda j: j)
    return pl.pallas_call(
        _mod_kernel,
        grid=(n_comp * D_MODEL // tn,),
        in_specs=in_specs,
        out_specs=out_spec,
        out_shape=jax.ShapeDtypeStruct((n_comp, C_ROWS, D_MODEL), F32),
        compiler_params=_params(1, 40),
        name="adaln_mod",
    )(c16, w_ada, b_ada.reshape(1, -1))


def _segment_table(mod):
    n = mod.shape[0]
    return jnp.concatenate([jnp.broadcast_to(mod[:, 0:1], (n, SEQ // SEG, D_MODEL)), mod[:, 1:1 + DEC_BATCH]], axis=1)


def _seg_operand(entry, tile_rows, width, index_map):
    table, comp = entry
    view = table.reshape(table.shape[0], M_ROWS // tile_rows, tile_rows // SEG, D_MODEL)
    spec = pl.BlockSpec((1, 1, tile_rows // SEG, width), lambda *idx: (comp,) + tuple(index_map(*idx)))
    return view, spec


def _split_x_specs(tile_rows, width, col):
    n_prompt = SEQ // tile_rows
    return [pl.BlockSpec((tile_rows, width), lambda *idx: (jnp.minimum(idx[0], n_prompt - 1), col(*idx))),
            pl.BlockSpec((tile_rows, width), lambda *idx: (jnp.maximum(idx[0] - n_prompt, 0), col(*idx)))]


def _on_owner(x_refs, tile_rows, body):
    if len(x_refs) == 1:
        body(x_refs[0])
    else:
        i = pl.program_id(0)
        n_prompt = SEQ // tile_rows
        pl.when(i < n_prompt)(lambda: body(x_refs[0]))
        pl.when(i >= n_prompt)(lambda: body(x_refs[1]))


def _modulate_kernel(xp_ref, xs_ref, sc_ref, sh_ref, h_ref):
    def body(x_ref):
        for s in range(TR // SEG):
            rows = slice(s * SEG, (s + 1) * SEG)
            h = x_ref[rows, :] * (1.0 + sc_ref[0, 0, s:s + 1, :]) + sh_ref[0, 0, s:s + 1, :]
            h_ref[rows, :] = h.astype(h_ref.dtype)

    _on_owner((xp_ref, xs_ref), TR, body)


def _modulate(xs, sc, sh):
    sc_arr, sc_spec = _seg_operand(sc, TR, D_MODEL, lambda i: (i, 0, 0))
    sh_arr, sh_spec = _seg_operand(sh, TR, D_MODEL, lambda i: (i, 0, 0))
    return pl.pallas_call(
        _modulate_kernel,
        grid=(M_ROWS // TR,),
        in_specs=_split_x_specs(TR, D_MODEL, lambda i: 0) + [sc_spec, sh_spec],
        out_specs=pl.BlockSpec((TR, D_MODEL), lambda i: (i, 0)),
        out_shape=jax.ShapeDtypeStruct((M_ROWS, D_MODEL), BF16),
        compiler_params=_params(1, 48),
        name="modulate",
    )(*xs, sc_arr, sh_arr)


def _raw_residual(*xs):
    return ("raw",) + xs


def _ln_residual(z, mu, rstd, ln_g, ln_b):
    return ("ln", z, mu, rstd, ln_g.reshape(1, -1), ln_b.reshape(1, -1))


def _residual_operands(res, tm, tn):
    tile = pl.BlockSpec((tm, tn), lambda i, j: (i, j))
    if res[0] == "raw":
        xs = list(res[1:])
        return xs, (_split_x_specs(tm, tn, lambda i, j: j) if len(xs) == 2 else [tile])
    stat = pl.BlockSpec((tm, LANES), lambda i, j: (i, 0))
    vec = pl.BlockSpec((1, tn), lambda i, j: (0, j))
    return list(res[1:]), [tile, stat, stat, vec, vec]


def _residual_rows(kind, refs, tile_rows, rows):
    if kind == "ln":
        z_ref, mu_ref, rstd_ref, g_ref, b_ref = refs
        return (z_ref[rows, :] - mu_ref[rows, :1]) * rstd_ref[rows, :1] * g_ref[...] + b_ref[...]
    x = refs[0][rows, :]
    if len(refs) == 2:
        x = jnp.where(pl.program_id(0) < SEQ // tile_rows, x, refs[1][rows, :])
    return x


def _gated_residual(kind, x_refs, tile_rows, acc, gate_ref, gate_scale, z_ref):
    for s in range(acc.shape[0] // SEG):
        rows = slice(s * SEG, (s + 1) * SEG)
        x = _residual_rows(kind, x_refs, tile_rows, rows)
        z_ref[rows, :] = ALPHA * x + (gate_scale * gate_ref[0, 0, s:s + 1, :]) * acc[rows, :]


def _ln_kernel(z_ref, g_ref, b_ref, *rest, with_h):
    if with_h:
        sc_ref, sh_ref, h_ref, mu_ref, rstd_ref = rest
    else:
        (xo_ref,) = rest
    for s in range(TR // SEG):
        rows = slice(s * SEG, (s + 1) * SEG)
        z = z_ref[rows, :]
        mu = jnp.mean(z, axis=-1, keepdims=True)
        zc = z - mu
        var = jnp.mean(zc * zc, axis=-1, keepdims=True)
        rstd = lax.rsqrt(var + LN_EPS)
        xn = zc * rstd * g_ref[...] + b_ref[...]
        if with_h:
            h = xn * (1.0 + sc_ref[0, 0, s:s + 1, :]) + sh_ref[0, 0, s:s + 1, :]
            h_ref[rows, :] = h.astype(h_ref.dtype)
            mu_ref[rows, :] = jnp.broadcast_to(mu, (SEG, LANES))
            rstd_ref[rows, :] = jnp.broadcast_to(rstd, (SEG, LANES))
        else:
            xo_ref[rows, :] = xn


def _layernorm(z, ln_g, ln_b, sc=None, sh=None, *, row0=0, rows=M_ROWS):
    with_h = sc is not None
    off = row0 // TR
    vec = pl.BlockSpec((1, D_MODEL), lambda i: (0, 0))
    in_specs = [pl.BlockSpec((TR, D_MODEL), lambda i: (i + off, 0)), vec, vec]
    args = [z, ln_g.reshape(1, D_MODEL), ln_b.reshape(1, D_MODEL)]
    row_out = pl.BlockSpec((TR, D_MODEL), lambda i: (i, 0))
    if with_h:
        for entry in (sc, sh):
            arr, spec = _seg_operand(entry, TR, D_MODEL, lambda i: (i + off, 0, 0))
            in_specs.append(spec)
            args.append(arr)
        stat = pl.BlockSpec((TR, LANES), lambda i: (i, 0))
        out_specs = [row_out, stat, stat]
        out_shape = [jax.ShapeDtypeStruct((rows, D_MODEL), BF16),
                     jax.ShapeDtypeStruct((rows, LANES), F32), jax.ShapeDtypeStruct((rows, LANES), F32)]
    else:
        out_specs = [row_out]
        out_shape = [jax.ShapeDtypeStruct((rows, D_MODEL), F32)]
    return pl.pallas_call(
        functools.partial(_ln_kernel, with_h=with_h),
        grid=(rows // TR,),
        in_specs=in_specs,
        out_specs=out_specs,
        out_shape=out_shape,
        compiler_params=_params(1, 56),
        name="ln_mod" if with_h else "ln",
    )(*args)


def _dot(a, b):
    return lax.dot_general(a, b, (((1,), (0,)), ((), ())), preferred_element_type=F32)


def _gateup_kernel(h_ref, wg_ref, wu_ref, wd_ref, o_ref, wd_out_ref):
    h = h_ref[...]
    g = _dot(h, wg_ref[...])
    u = _dot(h, wu_ref[...])
    o_ref[...] = (g * jax.nn.sigmoid(g) * u).astype(o_ref.dtype)

    @pl.when(pl.program_id(0) == 0)
    def _():
        wd_out_ref[...] = wd_ref[...].astype(wd_out_ref.dtype)


def _swiglu_up(h, wgu, w_down):
    tm = 2 * TM
    tn = 256
    nj = D_FF // tn
    slab = pl.BlockSpec((tn, D_MODEL), lambda i, j: (jnp.where(i == 0, j, nj - 1), 0))
    return pl.pallas_call(
        _gateup_kernel,
        grid=(M_ROWS // tm, nj),
        in_specs=[pl.BlockSpec((tm, D_MODEL), lambda i, j: (i, 0), pipeline_mode=pl.Buffered(1)),
                  pl.BlockSpec((D_MODEL, tn), lambda i, j: (0, j)),
                  pl.BlockSpec((D_MODEL, tn), lambda i, j: (0, j + nj)),
                  slab],
        out_specs=[pl.BlockSpec((tm, tn), lambda i, j: (i, j)), slab],
        out_shape=[jax.ShapeDtypeStruct((M_ROWS, D_FF), BF16),
                   jax.ShapeDtypeStruct((D_FF, D_MODEL), BF16)],
        compiler_params=_params(2, 56),
        name="swiglu_up",
    )(h, wgu, wgu, w_down)


def _mm_nt_kernel(a_ref, bt_ref, o_ref):
    o_ref[...] = lax.dot_general(a_ref[...], bt_ref[...], (((1,), (1,)), ((), ())), preferred_element_type=F32)


def _matmul_nt(a, bt, tm, tn, name, n_cols):
    m, k = a.shape

    def pipelined(a_hbm, bt_hbm, o_hbm):
        pltpu.emit_pipeline(
            _mm_nt_kernel,
            grid=(m // tm, n_cols // tn),
            in_specs=[pl.BlockSpec((tm, k), lambda i, j: (i, 0)),
                      pl.BlockSpec((tn, k), lambda i, j: (j, 0))],
            out_specs=[pl.BlockSpec((tm, tn), lambda i, j: (i, j))],
        )(a_hbm, bt_hbm, o_hbm)

    return pl.pallas_call(
        pipelined,
        in_specs=[pl.BlockSpec(memory_space=pl.ANY), pl.BlockSpec(memory_space=pl.ANY)],
        out_specs=pl.BlockSpec(memory_space=pl.ANY),
        out_shape=jax.ShapeDtypeStruct((m, n_cols), F32),
        compiler_params=_params(0, 48),
        name=name,
    )(a, bt)


def _down_kernel(*refs, res_kind, n_res, tile_rows, gate_scale, with_adaln):
    a_ref, b_ref = refs[:2]
    x_refs = refs[2:2 + n_res]
    gate_ref = refs[2 + n_res]
    if with_adaln:
        c_ref, w_ref, bias_ref, z_ref, mod_ref = refs[3 + n_res:]
    else:
        (z_ref,) = refs[3 + n_res:]
    acc = _dot(a_ref[...], b_ref[...])
    _gated_residual(res_kind, x_refs, tile_rows, acc, gate_ref, gate_scale, z_ref)
    if with_adaln:
        mod_ref[0] = _adaln_tile(c_ref, w_ref, bias_ref)


def _ffn_down(a, w_down_bf16, res, gate, adaln=None):
    tm, tn = TR, 512
    m, k = a.shape
    n = w_down_bf16.shape[1]
    grid = (m // tm, n // tn)
    gate_arr, gate_spec = _seg_operand(gate, tm, tn, lambda i, j: (i, 0, j))
    res_args, res_specs = _residual_operands(res, tm, tn)
    in_specs = [pl.BlockSpec((tm, k), lambda i, j: (i, 0)),
                pl.BlockSpec((k, tn), lambda i, j: (0, j))] + res_specs + [gate_spec]
    args = [a, w_down_bf16, *res_args, gate_arr]
    out_specs = [pl.BlockSpec((tm, tn), lambda i, j: (i, j))]
    out_shape = [jax.ShapeDtypeStruct((m, n), F32)]
    if adaln is not None:
        c16, w_ada, b_ada, comp0, n_comp = adaln
        side_tn = 256
        n_side = n_comp * D_MODEL // side_tn
        assert n_side <= grid[0] * grid[1]
        side_in, side_out = _adaln_specs(side_tn, comp0, lambda i, j: jnp.minimum(i * grid[1] + j, n_side - 1))
        in_specs += side_in
        args += [c16, w_ada, b_ada.reshape(1, -1)]
        out_specs.append(side_out)
        out_shape.append(jax.ShapeDtypeStruct((n_comp, C_ROWS, D_MODEL), F32))
    return pl.pallas_call(
        functools.partial(_down_kernel, res_kind=res[0], n_res=len(res_args), tile_rows=tm, gate_scale=0.5,
                          with_adaln=adaln is not None),
        grid=grid,
        in_specs=in_specs,
        out_specs=out_specs,
        out_shape=out_shape,
        compiler_params=_params(2, 60),
        name="ffn_down" if adaln is None else "ffn_down_adaln",
    )(*args)


def _out_proj_kernel(a1_ref, a2_ref, b_ref, *refs, res_kind):
    x_refs, gate_ref, z_ref = refs[:-2], refs[-2], refs[-1]
    k1 = a1_ref.shape[1]
    acc = _dot(a1_ref[...], b_ref[:k1, :])
    acc += _dot(a2_ref[...], b_ref[k1:, :])
    _gated_residual(res_kind, x_refs, TM, acc, gate_ref, 1.0, z_ref)


def _out_proj(a1, a2, b, res, gate):
    tm, tn = TM, 512
    m, k1 = a1.shape
    k2 = a2.shape[1]
    n = b.shape[1]
    gate_arr, gate_spec = _seg_operand(gate, tm, tn, lambda i, j: (i, 0, j))
    res_args, res_specs = _residual_operands(res, tm, tn)
    return pl.pallas_call(
        functools.partial(_out_proj_kernel, res_kind=res[0]),
        grid=(m // tm, n // tn),
        in_specs=[pl.BlockSpec((tm, k1), lambda i, j: (i, 0)),
                  pl.BlockSpec((tm, k2), lambda i, j: (i, 0)),
                  pl.BlockSpec((k1 + k2, tn), lambda i, j: (0, j))] + res_specs + [gate_spec],
        out_specs=pl.BlockSpec((tm, tn), lambda i, j: (i, j)),
        out_shape=jax.ShapeDtypeStruct((m, n), F32),
        compiler_params=_params(2, 56),
        name="out_proj",
    )(a1, a2, b, *res_args, gate_arr)


def _rope128(blk, cc, ss):
    return blk * cc + pltpu.roll(blk, QK_ROPE, 1) * ss


def _q_kernel(cq_ref, g_ref, w_ref, cc_ref, ss_ref, o_ref):
    x = cq_ref[...]
    r = lax.rsqrt(jnp.mean(x * x, axis=-1, keepdims=True) + RMS_EPS)
    a = (x * r * g_ref[...]).astype(BF16)
    cc = cc_ref[...]
    ss = ss_ref[...]
    for h in range(HEADS):
        acc = _dot(a, w_ref[:, h * HEAD_PAD:(h + 1) * HEAD_PAD])
        o_ref[h, :, :LANES] = (acc[:, :LANES] * Q_SCALE).astype(o_ref.dtype)
        o_ref[h, :, LANES:] = (_rope128(acc[:, LANES:], cc, ss) * Q_SCALE).astype(o_ref.dtype)


def _q_proj(proj, q_norm_g, wuq, cc, ss):
    tm = TM // 2
    cq_block = (2 * GMLP_WIDTH) // Q_RANK
    return pl.pallas_call(
        _q_kernel,
        grid=(M_ROWS // tm,),
        in_specs=[pl.BlockSpec((tm, Q_RANK), lambda i: (i, cq_block)),
                  pl.BlockSpec((1, Q_RANK), lambda i: (0, 0)),
                  pl.BlockSpec((Q_RANK, HEADS * HEAD_PAD), lambda i: (0, 0)),
                  pl.BlockSpec((tm, LANES), lambda i: (i, 0)),
                  pl.BlockSpec((tm, LANES), lambda i: (i, 0))],
        out_specs=pl.BlockSpec((HEADS, tm, HEAD_PAD), lambda i: (0, i, 0)),
        out_shape=jax.ShapeDtypeStruct((HEADS, M_ROWS, HEAD_PAD), BF16),
        compiler_params=_params(1, 40),
        name="q_proj",
    )(proj, q_norm_g.reshape(1, Q_RANK), wuq, cc, ss)


def _latkr_kernel(ckv_ref, h_ref, wkr_ref, g_ref, cc_ref, ss_ref, lat_ref, kr_out_ref, kr128_ref):
    x = ckv_ref[...]
    r = lax.rsqrt(jnp.mean(x * x, axis=-1, keepdims=True) + RMS_EPS)
    lat_ref[...] = x * r * g_ref[...]
    blk = lax.dot_general(h_ref[...], wkr_ref[...], (((1,), (1,)), ((), ())), preferred_element_type=F32)
    rot = _rope128(blk, cc_ref[...], ss_ref[...])
    kr_out_ref[...] = rot[:, :QK_ROPE]
    kr128_ref[...] = rot


def _lat_krope(proj, h, wkr, kv_norm_g, cc, ss, row0, rows):
    tr = 512
    off = row0 // tr
    ckv_block = (2 * GMLP_WIDTH + Q_RANK) // KV_RANK
    return pl.pallas_call(
        _latkr_kernel,
        grid=(rows // tr,),
        in_specs=[pl.BlockSpec((tr, KV_RANK), lambda i: (i + off, ckv_block)),
                  pl.BlockSpec((tr, D_MODEL), lambda i: (i + off, 0)),
                  pl.BlockSpec((LANES, D_MODEL), lambda i: (0, 0)),
                  pl.BlockSpec((1, KV_RANK), lambda i: (0, 0)),
                  pl.BlockSpec((tr, LANES), lambda i: (i + off, 0)),
                  pl.BlockSpec((tr, LANES), lambda i: (i + off, 0))],
        out_specs=[pl.BlockSpec((tr, KV_RANK), lambda i: (i, 0)),
                   pl.BlockSpec((tr, QK_ROPE), lambda i: (i, 0)),
                   pl.BlockSpec((tr, LANES), lambda i: (i, 0))],
        out_shape=[jax.ShapeDtypeStruct((rows, KV_RANK), F32),
                   jax.ShapeDtypeStruct((rows, QK_ROPE), F32),
                   jax.ShapeDtypeStruct((rows, LANES), F32)],
        compiler_params=_params(1, 32),
        name="lat_krope",
    )(proj, h, wkr, kv_norm_g.reshape(1, KV_RANK), cc, ss)


def _kv_t_kernel(lat_ref, kr_ref, w_ref, wvt_ref, k_ref, vt_ref):
    a = lat_ref[...].astype(BF16)
    kr = kr_ref[...].astype(BF16)
    for h in range(HEADS):
        kn = _dot(a, w_ref[:, h * HEAD_PAD:h * HEAD_PAD + QK_NOPE])
        k_ref[h, :, :LANES] = kn.astype(BF16)
        k_ref[h, :, LANES:] = kr
        vt = lax.dot_general(wvt_ref[h], a, (((1,), (1,)), ((), ())), preferred_element_type=F32)
        vt_ref[h, 0] = vt.astype(BF16)


def _kv_proj_t(lat, kr128, wukv, wvt, tr):
    rows = lat.shape[0]
    return pl.pallas_call(
        _kv_t_kernel,
        grid=(rows // tr,),
        in_specs=[pl.BlockSpec((tr, KV_RANK), lambda i: (i, 0)),
                  pl.BlockSpec((tr, LANES), lambda i: (i, 0)),
                  pl.BlockSpec((KV_RANK, HEADS * HEAD_PAD), lambda i: (0, 0)),
                  pl.BlockSpec((HEADS, HEAD_V, KV_RANK), lambda i: (0, 0, 0))],
        out_specs=[pl.BlockSpec((HEADS, tr, HEAD_PAD), lambda i: (0, i, 0)),
                   pl.BlockSpec((HEADS, 1, HEAD_V, tr), lambda i: (0, i, 0, 0))],
        out_shape=[jax.ShapeDtypeStruct((HEADS, rows, HEAD_PAD), BF16),
                   jax.ShapeDtypeStruct((HEADS, rows // tr, HEAD_V, tr), BF16)],
        compiler_params=_params(1, 48),
        name="kv_proj_t",
    )(lat, kr128, wukv, wvt)


ATTN_TILE = 1024


def _softmax_pv_t(s, s_max, vt, carry):
    m, l, acc = carry
    m_new = jnp.maximum(m, s_max)
    p = jnp.exp2(s - m_new)
    a = jnp.exp2(m - m_new)
    l = a * l + jnp.sum(p, axis=0, keepdims=True)
    acc = a * acc + jnp.dot(vt, p.astype(BF16), preferred_element_type=F32)
    return m_new, l, acc


def _chunk_mask_t(q_rel0, nk, nq):
    kc = lax.broadcasted_iota(jnp.int32, (nk, nq), 0) >> CHUNK_SHIFT
    qc = (q_rel0 + lax.broadcasted_iota(jnp.int32, (nk, nq), 1)) >> CHUNK_SHIFT
    return qc >= kc


def _attn_prompt_kernel(q_ref, k_ref, vt_ref, o_ref, s_even, s_odd, *, rb, nh):
    qi = pl.program_id(1)
    n_rb = ATTN_TILE // rb
    heads = range(nh)

    def scores(j, s_ref):
        k0 = pl.multiple_of(j * ATTN_TILE, ATTN_TILE)
        maxima = []
        for hh in heads:
            s = lax.dot_general(k_ref[hh, pl.ds(k0, ATTN_TILE), :], q_ref[hh], (((1,), (1,)), ((), ())),
                                preferred_element_type=F32)
            s_ref[hh] = s
            maxima.append(jnp.max(s, axis=0, keepdims=True))
        return tuple(maxima)

    def make_step(s_cur, s_next):
        def step(j, carry):
            stats, s_max = carry
            next_max = scores(j + 1, s_next)
            stats = tuple(
                tuple(_softmax_pv_t(s_cur[hh, :, r * rb:(r + 1) * rb], s_max[hh][:, r * rb:(r + 1) * rb],
                                    vt_ref[hh, j], stats[hh][r]) for r in range(n_rb))
                for hh in heads)
            return stats, next_max
        return step

    step_even, step_odd = make_step(s_even, s_odd), make_step(s_odd, s_even)

    def finish(s_ref, stats):
        mask = _chunk_mask_t(0, rb, rb)
        for r in range(n_rb):
            nk = (r + 1) * rb
            cols = slice(r * rb, nk)
            for hh in heads:
                s_r = jnp.where(mask, s_ref[hh, r * rb:nk, cols], NEG)
                if r > 0:
                    s_r = jnp.concatenate([s_ref[hh, :r * rb, cols], s_r], axis=0)
                _, l, acc = _softmax_pv_t(s_r, jnp.max(s_r, axis=0, keepdims=True), vt_ref[hh, qi, :, :nk],
                                          stats[hh][r])
                o_ref[r * rb:(r + 1) * rb, hh * HEAD_V:(hh + 1) * HEAD_V] = (acc / l).T.astype(o_ref.dtype)

    init = tuple(tuple((jnp.full((1, rb), NEG, F32), jnp.zeros((1, rb), F32), jnp.zeros((HEAD_V, rb), F32))
                       for _ in range(n_rb)) for _ in heads)
    carry = (init, scores(0, s_even))
    carry = lax.fori_loop(0, qi >> 1, lambda p, c: step_odd(2 * p + 1, step_even(2 * p, c)), carry)
    odd = (qi & 1) == 1
    stats, _ = lax.cond(odd, lambda: step_even(qi - 1, carry), lambda: carry)
    lax.cond(odd, lambda: finish(s_odd, stats), lambda: finish(s_even, stats))


def _attn_prompt(q, k, vt):
    nh = 2
    return pl.pallas_call(
        functools.partial(_attn_prompt_kernel, rb=256, nh=nh),
        grid=(HEADS // nh, SEQ // ATTN_TILE),
        in_specs=[pl.BlockSpec((nh, ATTN_TILE, HEAD_PAD), lambda h, i: (h, i, 0)),
                  pl.BlockSpec((nh, SEQ, HEAD_PAD), lambda h, i: (h, 0, 0)),
                  pl.BlockSpec((nh, SEQ // ATTN_TILE, HEAD_V, ATTN_TILE), lambda h, i: (h, 0, 0, 0))],
        out_specs=pl.BlockSpec((ATTN_TILE, nh * HEAD_V), lambda h, i: (i, h)),
        out_shape=jax.ShapeDtypeStruct((M_ROWS, HEADS * HEAD_V), BF16),
        scratch_shapes=[pltpu.VMEM((nh, ATTN_TILE, ATTN_TILE), F32), pltpu.VMEM((nh, ATTN_TILE, ATTN_TILE), F32)],
        compiler_params=_params(2, 56),
        name="attn_prompt",
    )(q, k, vt)


def _attn_sample_kernel(q_ref, cache_ref, new_ref, kr_ref, w_ref, o_prev_ref, o_ref, qa_scr):
    del o_prev_ref
    nt = (((1,), (1,)), ((), ()))
    pad = jnp.zeros((KV_PAD - PAST_LEN - DEC_SEQ, KV_RANK), F32)
    lat = jnp.concatenate([cache_ref[0], new_ref[...], pad], axis=0).astype(BF16)
    keys = jnp.concatenate([lat, kr_ref[0].astype(BF16)], axis=1)
    for h in range(HEADS):
        rows = slice(h * DEC_SEQ, (h + 1) * DEC_SEQ)
        w_uk = w_ref[:, h * HEAD_PAD:h * HEAD_PAD + QK_NOPE]
        q_lat = lax.dot_general(q_ref[h, :, :LANES], w_uk, nt, preferred_element_type=F32)
        qa_scr[rows, :KV_RANK] = q_lat.astype(BF16)
        qa_scr[rows, KV_RANK:] = q_ref[h, :, LANES:]
    s = lax.dot_general(qa_scr[...], keys, nt, preferred_element_type=F32)
    shape = (HEADS * DEC_SEQ, KV_PAD)
    q_pos = PAST_LEN + (lax.broadcasted_iota(jnp.int32, shape, 0) & (DEC_SEQ - 1))
    k_pos = lax.broadcasted_iota(jnp.int32, shape, 1)
    visible = ((q_pos >> CHUNK_SHIFT) >= (k_pos >> CHUNK_SHIFT)) & (k_pos < PAST_LEN + DEC_SEQ)
    s = jnp.where(visible, s, NEG)
    m = jnp.max(s, axis=-1, keepdims=True)
    p = jnp.exp2(s - m)
    l = jnp.sum(p, axis=-1, keepdims=True)
    o_lat = (jnp.dot(p.astype(BF16), lat, preferred_element_type=F32) / l).astype(BF16)
    for h in range(HEADS):
        w_uv = w_ref[:, h * HEAD_PAD + QK_NOPE:(h + 1) * HEAD_PAD]
        o_h = _dot(o_lat[h * DEC_SEQ:(h + 1) * DEC_SEQ, :], w_uv)
        o_ref[:, h * HEAD_V:(h + 1) * HEAD_V] = o_h.astype(o_ref.dtype)


def _attn_sample(q, cache_lat, lat_new, kr_all, wukv, o_prev):
    q_off = SEQ // DEC_SEQ
    return pl.pallas_call(
        _attn_sample_kernel,
        grid=(DEC_BATCH,),
        in_specs=[pl.BlockSpec((HEADS, DEC_SEQ, HEAD_PAD), lambda b: (0, q_off + b, 0)),
                  pl.BlockSpec((1, PAST_LEN, KV_RANK), lambda b: (b, 0, 0)),
                  pl.BlockSpec((DEC_SEQ, KV_RANK), lambda b: (b, 0)),
                  pl.BlockSpec((1, KV_PAD, LANES), lambda b: (b, 0, 0)),
                  pl.BlockSpec((KV_RANK, HEADS * HEAD_PAD), lambda b: (0, 0)),
                  pl.BlockSpec(memory_space=pl.ANY)],
        out_specs=pl.BlockSpec((DEC_SEQ, HEADS * HEAD_V), lambda b: (q_off + b, 0)),
        out_shape=jax.ShapeDtypeStruct((M_ROWS, HEADS * HEAD_V), BF16),
        scratch_shapes=[pltpu.VMEM((HEADS * DEC_SEQ, KV_RANK + LANES), BF16)],
        input_output_aliases={5: 0},
        compiler_params=_params(1, 48),
        name="attn_sample",
    )(q, cache_lat, lat_new, kr_all, wukv, o_prev)


def _gmlp_kernel(u_ref, v_ref, g_ref, b_ref, w_ref, bs_ref, *rest, chunk, emit_vn):
    if emit_vn:
        _, a_ref, vn_ref = rest
    else:
        (a_ref,) = rest
    v = v_ref[...]
    mu = jnp.mean(v, axis=-1, keepdims=True)
    vc = v - mu
    var = jnp.mean(vc * vc, axis=-1, keepdims=True)
    vn = vc * lax.rsqrt(var + LN_EPS) * g_ref[...] + b_ref[...]
    if emit_vn:
        vn_ref[...] = vn
    vnb = vn.astype(BF16)
    n_chunks = v.shape[0] // chunk
    causal = (lax.broadcasted_iota(jnp.int32, (chunk, GMLP_CHUNK), 0)
              >= lax.broadcasted_iota(jnp.int32, (chunk, GMLP_CHUNK), 1))
    for g in range(GMLP_GROUPS):
        cols = slice(g * GROUP_DIM, (g + 1) * GROUP_DIM)
        w = jnp.where(causal, w_ref[g, :chunk, :], 0.0).astype(BF16)
        rhs = jnp.concatenate([vnb[c * chunk:(c + 1) * chunk, cols] for c in range(n_chunks)], axis=1)
        if chunk < GMLP_CHUNK:
            rhs = jnp.concatenate([rhs, jnp.zeros((GMLP_CHUNK - chunk, rhs.shape[1]), BF16)], axis=0)
        mixed = jnp.dot(w, rhs, preferred_element_type=F32) + bs_ref[g, :chunk, :]
        for c in range(n_chunks):
            rows = slice(c * chunk, (c + 1) * chunk)
            gate = mixed[:, c * GROUP_DIM:(c + 1) * GROUP_DIM]
            a_ref[rows, cols] = (u_ref[rows, cols] * gate).astype(a_ref.dtype)


def _gmlp(proj, ln_g, ln_b, w_s, b_s3, row0, rows, tr, chunk, a_prev=None):
    emit_vn = a_prev is not None
    off = row0 // tr
    vec = pl.BlockSpec((1, GMLP_WIDTH), lambda i: (0, 0))
    in_specs = [pl.BlockSpec((tr, GMLP_WIDTH), lambda i: (i + off, 0)),
                pl.BlockSpec((tr, GMLP_WIDTH), lambda i: (i + off, 1)),
                vec, vec,
                pl.BlockSpec((GMLP_GROUPS, GMLP_CHUNK, GMLP_CHUNK), lambda i: (0, 0, 0)),
                pl.BlockSpec((GMLP_GROUPS, GMLP_CHUNK, 1), lambda i: (0, 0, 0))]
    args = [proj, proj, ln_g.reshape(1, GMLP_WIDTH), ln_b.reshape(1, GMLP_WIDTH), w_s, b_s3]
    out_specs = [pl.BlockSpec((tr, GMLP_WIDTH), lambda i: (i + off, 0))]
    out_shape = [jax.ShapeDtypeStruct((M_ROWS, GMLP_WIDTH), BF16)]
    aliases = {}
    if emit_vn:
        in_specs.append(pl.BlockSpec(memory_space=pl.ANY))
        args.append(a_prev)
        aliases = {len(args) - 1: 0}
        out_specs.append(pl.BlockSpec((tr, GMLP_WIDTH), lambda i: (i, 0)))
        out_shape.append(jax.ShapeDtypeStruct((rows, GMLP_WIDTH), F32))
    return pl.pallas_call(
        functools.partial(_gmlp_kernel, chunk=chunk, emit_vn=emit_vn),
        grid=(rows // tr,),
        in_specs=in_specs,
        out_specs=out_specs,
        out_shape=out_shape,
        input_output_aliases=aliases,
        compiler_params=_params(1, 40),
        name="gmlp_gate_vn" if emit_vn else "gmlp_gate",
    )(*args)


def _swap_halves(w):
    half = w.shape[-1] // 2
    return jnp.concatenate([w[..., half:], w[..., :half]], axis=-1)


def _rope_tables():
    pos = jnp.concatenate([jnp.arange(SEQ, dtype=jnp.int32),
                           jnp.tile(PAST_LEN + jnp.arange(DEC_SEQ, dtype=jnp.int32), DEC_BATCH)])
    inv = 1.0 / (ROPE_THETA ** (jnp.arange(0, QK_ROPE, 2, dtype=F32) / QK_ROPE))
    ang = pos.astype(F32)[:, None] * inv[None, :]
    cos, sin = jnp.cos(ang), jnp.sin(ang)
    zero = jnp.zeros((M_ROWS, LANES - QK_ROPE), F32)
    return (jnp.concatenate([cos, cos, zero], axis=1), jnp.concatenate([-sin, sin, zero], axis=1))


def kernel(x_prompt, x_sample, cache_mla_latent, cache_mla_krope, c_prompt, c_sample, w_ada, b_ada, ffn1_w_gate_up, ffn1_w_down, ln1_g, ln1_b, w_in, gmlp_ln_g, gmlp_ln_b, gmlp_w_s, gmlp_b_s, mla_q_norm_g, mla_w_uq, mla_kv_norm_g, mla_w_ukv, w_out, ln2_g, ln2_b, ffn2_w_gate_up, ffn2_w_down, ln3_g, ln3_b):
    wukv = mla_w_ukv.astype(BF16)
    wvt = mla_w_ukv.reshape(KV_RANK, HEADS, HEAD_PAD)[:, :, QK_NOPE:].transpose(1, 2, 0).astype(BF16)
    w_in_t = w_in.T
    w_kr_t = w_in_t[IN_COLS:]
    wkr = jnp.concatenate([w_kr_t, w_kr_t[QK_ROPE // 2:], w_kr_t[:QK_ROPE // 2]], axis=0)
    uq = mla_w_uq.reshape(Q_RANK, HEADS, QK_NOPE + QK_ROPE)
    wuq = jnp.concatenate([uq, _swap_halves(uq[..., QK_NOPE:])], axis=-1).reshape(Q_RANK, HEADS * HEAD_PAD).astype(BF16)
    b_s3 = gmlp_b_s[:, :, None]
    cc, ss = _rope_tables()

    c16 = jnp.concatenate([c_prompt, c_sample, jnp.zeros((C_ROWS - 1 - DEC_BATCH, D_MODEL), F32)], axis=0)
    n_first = 3
    first = _segment_table(_modulation(c16, w_ada, b_ada, 0, n_first))
    SH1, SC1, G1 = ((first, i) for i in range(n_first))

    x0 = (x_prompt.reshape(SEQ, D_MODEL), x_sample.reshape(S_ROWS, D_MODEL))

    h1 = _modulate(x0, SC1, SH1)
    z1, mod_rest = _ffn_down(*_swiglu_up(h1, ffn1_w_gate_up, ffn1_w_down), _raw_residual(*x0), G1,
                             adaln=(c16, w_ada, b_ada, n_first, N_MOD - n_first))
    rest = _segment_table(mod_rest)
    SH2, SC2, G2, SH3, SC3, G3 = ((rest, i) for i in range(N_MOD - n_first))
    h2, mu1, rstd1 = _layernorm(z1, ln1_g, ln1_b, SC2, SH2)

    proj = _matmul_nt(h2, w_in_t, TM, IN_TILE, "in_proj", n_cols=IN_COLS)
    (a_mix,) = _gmlp(proj, gmlp_ln_g, gmlp_ln_b, gmlp_w_s, b_s3, 0, SEQ, 512, GMLP_CHUNK)
    a_mix, vn_s = _gmlp(proj, gmlp_ln_g, gmlp_ln_b, gmlp_w_s, b_s3, SEQ, S_ROWS, DEC_SEQ, DEC_SEQ, a_prev=a_mix)
    q = _q_proj(proj, mla_q_norm_g, wuq, cc, ss)
    lat_p, kr_p, kr128_p = _lat_krope(proj, h2, wkr, mla_kv_norm_g, cc, ss, 0, SEQ)
    lat_s, kr_s, kr128_s = _lat_krope(proj, h2, wkr, mla_kv_norm_g, cc, ss, SEQ, S_ROWS)

    k_p, vt_p = _kv_proj_t(lat_p, kr128_p, wukv, wvt, ATTN_TILE)
    o_mix = _attn_prompt(q, k_p, vt_p)

    pad_rows = KV_PAD - PAST_LEN - DEC_SEQ
    cache_kr128 = jnp.pad(cache_mla_krope, ((0, 0), (0, 0), (0, LANES - QK_ROPE)))
    kr_all = jnp.concatenate([cache_kr128, kr128_s.reshape(DEC_BATCH, DEC_SEQ, LANES),
                              jnp.zeros((DEC_BATCH, pad_rows, LANES), F32)], axis=1)
    o_mix = _attn_sample(q, cache_mla_latent, lat_s, kr_all, wukv, o_mix)

    z2 = _out_proj(a_mix, o_mix, w_out, _ln_residual(z1, mu1, rstd1, ln1_g, ln1_b), G2)
    h3, mu2, rstd2 = _layernorm(z2, ln2_g, ln2_b, SC3, SH3)

    (z3,) = _ffn_down(*_swiglu_up(h3, ffn2_w_gate_up, ffn2_w_down), _ln_residual(z2, mu2, rstd2, ln2_g, ln2_b), G3)
    (y_p,) = _layernorm(z3, ln3_g, ln3_b, row0=0, rows=SEQ)
    (y_s,) = _layernorm(z3, ln3_g, ln3_b, row0=SEQ, rows=S_ROWS)

    return (y_p.reshape(1, SEQ, D_MODEL),
            y_s.reshape(DEC_BATCH, DEC_SEQ, D_MODEL),
            lat_p.reshape(1, SEQ, KV_RANK),
            kr_p.reshape(1, SEQ, QK_ROPE),
            lat_s.reshape(DEC_BATCH, DEC_SEQ, KV_RANK),
            kr_s.reshape(DEC_BATCH, DEC_SEQ, QK_ROPE),
            vn_s.reshape(DEC_BATCH, DEC_SEQ, GMLP_WIDTH))
```

```python
import functools

import jax
import jax.numpy as jnp
from jax import lax
from jax.experimental import pallas as pl
from jax.experimental.pallas import tpu as pltpu

F32 = jnp.float32
BF16 = jnp.bfloat16

D_MODEL = 4096
SEQ = 8192
DEC_BATCH = 8
DEC_SEQ = 64
PAST_LEN = 2048
CHUNK = 64
CHUNK_SHIFT = 6
GMLP_CHUNK = 128
GROUP_DIM = 128
GMLP_WIDTH = D_MODEL // 2
GMLP_GROUPS = GMLP_WIDTH // GROUP_DIM
HEAD_V = 128
QK_NOPE = 128
QK_ROPE = 64
HEADS = (D_MODEL - GMLP_WIDTH) // HEAD_V
KV_RANK = 512
Q_RANK = D_MODEL // 4
D_FF = 256 * ((8 * D_MODEL // 3 + 255) // 256)
N_MOD = 9
ROPE_THETA = 10000.0
LN_EPS = 1e-5
RMS_EPS = 1e-6
ALPHA = 2.0 ** 0.25
Q_SCALE = (QK_NOPE + QK_ROPE) ** -0.5 * 1.4426950408889634
NEG = -1e30

LANES = 128
HEAD_PAD = 2 * LANES
S_ROWS = DEC_BATCH * DEC_SEQ
M_ROWS = SEQ + S_ROWS
SEG = DEC_SEQ
N_SEG = M_ROWS // SEG
TM = M_ROWS // 8
TR = 8 * SEG
IN_COLS = 2 * GMLP_WIDTH + Q_RANK + KV_RANK
IN_TILE = 512
KV_PAD = 17 * LANES
MIB = 2 ** 20


def _params(n_axes, vmem_mib):
    return pltpu.CompilerParams(dimension_semantics=("arbitrary",) * n_axes,
                                vmem_limit_bytes=vmem_mib * MIB)


C_ROWS = 16


def _adaln_tile(c_ref, w_ref, b_ref):
    c = c_ref[...]
    s = (c * jax.nn.sigmoid(c)).astype(BF16)
    return lax.dot_general(s, w_ref[...], (((1,), (0,)), ((), ())), preferred_element_type=F32) + b_ref[...]


def _adaln_specs(tn, comp0, block_of_step):
    per_comp = D_MODEL // tn

    def col(*idx):
        return comp0 * per_comp + block_of_step(*idx)

    in_specs = [pl.BlockSpec((C_ROWS, D_MODEL), lambda *idx: (0, 0)),
                pl.BlockSpec((D_MODEL, tn), lambda *idx: (0, col(*idx))),
                pl.BlockSpec((1, tn), lambda *idx: (0, col(*idx)))]
    out_spec = pl.BlockSpec((1, C_ROWS, tn),
                            lambda *idx: (block_of_step(*idx) // per_comp, 0, block_of_step(*idx) % per_comp))
    return in_specs, out_spec


def _mod_kernel(c_ref, w_ref, b_ref, o_ref):
    o_ref[0] = _adaln_tile(c_ref, w_ref, b_ref)


def _modulation(c16, w_ada, b_ada, comp0, n_comp):
    tn = 512
    in_specs, out_spec = _adaln_specs(tn, comp0, lambda j: j)
    return pl.pallas_call(
        _mod_kernel,
        grid=(n_comp * D_MODEL // tn,),
        in_specs=in_specs,
        out_specs=out_spec,
        out_shape=jax.ShapeDtypeStruct((n_comp, C_ROWS, D_MODEL), F32),
        compiler_params=_params(1, 40),
        name="adaln_mod",
    )(c16, w_ada, b_ada.reshape(1, -1))


def _segment_table(mod):
    n = mod.shape[0]
    return jnp.concatenate([jnp.broadcast_to(mod[:, 0:1], (n, SEQ // SEG, D_MODEL)), mod[:, 1:1 + DEC_BATCH]], axis=1)


def _seg_operand(entry, tile_rows, width, index_map):
    table, comp = entry
    view = table.reshape(table.shape[0], M_ROWS // tile_rows, tile_rows // SEG, D_MODEL)
    spec = pl.BlockSpec((1, 1, tile_rows // SEG, width), lambda *idx: (comp,) + tuple(index_map(*idx)))
    return view, spec


def _split_x_specs(tile_rows, width, col):
    n_prompt = SEQ // tile_rows
    return [pl.BlockSpec((tile_rows, width), lambda *idx: (jnp.minimum(idx[0], n_prompt - 1), col(*idx))),
            pl.BlockSpec((tile_rows, width), lambda *idx: (jnp.maximum(idx[0] - n_prompt, 0), col(*idx)))]


def _on_owner(x_refs, tile_rows, body):
    if len(x_refs) == 1:
        body(x_refs[0])
    else:
        i = pl.program_id(0)
        n_prompt = SEQ // tile_rows
        pl.when(i < n_prompt)(lambda: body(x_refs[0]))
        pl.when(i >= n_prompt)(lambda: body(x_refs[1]))


def _modulate_kernel(xp_ref, xs_ref, sc_ref, sh_ref, h_ref):
    def body(x_ref):
        for s in range(TR // SEG):
            rows = slice(s * SEG, (s + 1) * SEG)
            h = x_ref[rows, :] * (1.0 + sc_ref[0, 0, s:s + 1, :]) + sh_ref[0, 0, s:s + 1, :]
            h_ref[rows, :] = h.astype(h_ref.dtype)

    _on_owner((xp_ref, xs_ref), TR, body)


def _modulate(xs, sc, sh):
    sc_arr, sc_spec = _seg_operand(sc, TR, D_MODEL, lambda i: (i, 0, 0))
    sh_arr, sh_spec = _seg_operand(sh, TR, D_MODEL, lambda i: (i, 0, 0))
    return pl.pallas_call(
        _modulate_kernel,
        grid=(M_ROWS // TR,),
        in_specs=_split_x_specs(TR, D_MODEL, lambda i: 0) + [sc_spec, sh_spec],
        out_specs=pl.BlockSpec((TR, D_MODEL), lambda i: (i, 0)),
        out_shape=jax.ShapeDtypeStruct((M_ROWS, D_MODEL), BF16),
        compiler_params=_params(1, 48),
        name="modulate",
    )(*xs, sc_arr, sh_arr)


def _raw_residual(*xs):
    return ("raw",) + xs


def _ln_residual(z, mu, rstd, ln_g, ln_b):
    return ("ln", z, mu, rstd, ln_g.reshape(1, -1), ln_b.reshape(1, -1))


def _residual_operands(res, tm, tn):
    tile = pl.BlockSpec((tm, tn), lambda i, j: (i, j))
    if res[0] == "raw":
        xs = list(res[1:])
        return xs, (_split_x_specs(tm, tn, lambda i, j: j) if len(xs) == 2 else [tile])
    stat = pl.BlockSpec((tm, LANES), lambda i, j: (i, 0))
    vec = pl.BlockSpec((1, tn), lambda i, j: (0, j))
    return list(res[1:]), [tile, stat, stat, vec, vec]


def _residual_rows(kind, refs, tile_rows, rows):
    if kind == "ln":
        z_ref, mu_ref, rstd_ref, g_ref, b_ref = refs
        return (z_ref[rows, :] - mu_ref[rows, :1]) * rstd_ref[rows, :1] * g_ref[...] + b_ref[...]
    x = refs[0][rows, :]
    if len(refs) == 2:
        x = jnp.where(pl.program_id(0) < SEQ // tile_rows, x, refs[1][rows, :])
    return x


def _gated_residual(kind, x_refs, tile_rows, acc, gate_ref, gate_scale, z_ref):
    for s in range(acc.shape[0] // SEG):
        rows = slice(s * SEG, (s + 1) * SEG)
        x = _residual_rows(kind, x_refs, tile_rows, rows)
        z_ref[rows, :] = ALPHA * x + (gate_scale * gate_ref[0, 0, s:s + 1, :]) * acc[rows, :]


def _ln_kernel(z_ref, g_ref, b_ref, *rest, with_h):
    if with_h:
        sc_ref, sh_ref, h_ref, mu_ref, rstd_ref = rest
    else:
        (xo_ref,) = rest
    for s in range(TR // SEG):
        rows = slice(s * SEG, (s + 1) * SEG)
        z = z_ref[rows, :]
        mu = jnp.mean(z, axis=-1, keepdims=True)
        zc = z - mu
        var = jnp.mean(zc * zc, axis=-1, keepdims=True)
        rstd = lax.rsqrt(var + LN_EPS)
        xn = zc * rstd * g_ref[...] + b_ref[...]
        if with_h:
            h = xn * (1.0 + sc_ref[0, 0, s:s + 1, :]) + sh_ref[0, 0, s:s + 1, :]
            h_ref[rows, :] = h.astype(h_ref.dtype)
            mu_ref[rows, :] = jnp.broadcast_to(mu, (SEG, LANES))
            rstd_ref[rows, :] = jnp.broadcast_to(rstd, (SEG, LANES))
        else:
            xo_ref[rows, :] = xn


def _layernorm(z, ln_g, ln_b, sc=None, sh=None, *, row0=0, rows=M_ROWS):
    with_h = sc is not None
    off = row0 // TR
    vec = pl.BlockSpec((1, D_MODEL), lambda i: (0, 0))
    in_specs = [pl.BlockSpec((TR, D_MODEL), lambda i: (i + off, 0)), vec, vec]
    args = [z, ln_g.reshape(1, D_MODEL), ln_b.reshape(1, D_MODEL)]
    row_out = pl.BlockSpec((TR, D_MODEL), lambda i: (i, 0))
    if with_h:
        for entry in (sc, sh):
            arr, spec = _seg_operand(entry, TR, D_MODEL, lambda i: (i + off, 0, 0))
            in_specs.append(spec)
            args.append(arr)
        stat = pl.BlockSpec((TR, LANES), lambda i: (i, 0))
        out_specs = [row_out, stat, stat]
        out_shape = [jax.ShapeDtypeStruct((rows, D_MODEL), BF16),
                     jax.ShapeDtypeStruct((rows, LANES), F32), jax.ShapeDtypeStruct((rows, LANES), F32)]
    else:
        out_specs = [row_out]
        out_shape = [jax.ShapeDtypeStruct((rows, D_MODEL), F32)]
    return pl.pallas_call(
        functools.partial(_ln_kernel, with_h=with_h),
        grid=(rows // TR,),
        in_specs=in_specs,
        out_specs=out_specs,
        out_shape=out_shape,
        compiler_params=_params(1, 56),
        name="ln_mod" if with_h else "ln",
    )(*args)


def _dot(a, b):
    return lax.dot_general(a, b, (((1,), (0,)), ((), ())), preferred_element_type=F32)


def _gateup_kernel(h_ref, wg_ref, wu_ref, wd_ref, o_ref, wd_out_ref):
    h = h_ref[...]
    g = _dot(h, wg_ref[...])
    u = _dot(h, wu_ref[...])
    o_ref[...] = (g * jax.nn.sigmoid(g) * u).astype(o_ref.dtype)

    @pl.when(pl.program_id(0) == 0)
    def _():
        wd_out_ref[...] = wd_ref[...].astype(wd_out_ref.dtype)


def _swiglu_up(h, wgu, w_down):
    tm = 2 * TM
    tn = 256
    nj = D_FF // tn
    slab = pl.BlockSpec((tn, D_MODEL), lambda i, j: (jnp.where(i == 0, j, nj - 1), 0))
    return pl.pallas_call(
        _gateup_kernel,
        grid=(M_ROWS // tm, nj),
        in_specs=[pl.BlockSpec((tm, D_MODEL), lambda i, j: (i, 0), pipeline_mode=pl.Buffered(1)),
                  pl.BlockSpec((D_MODEL, tn), lambda i, j: (0, j)),
                  pl.BlockSpec((D_MODEL, tn), lambda i, j: (0, j + nj)),
                  slab],
        out_specs=[pl.BlockSpec((tm, tn), lambda i, j: (i, j)), slab],
        out_shape=[jax.ShapeDtypeStruct((M_ROWS, D_FF), BF16),
                   jax.ShapeDtypeStruct((D_FF, D_MODEL), BF16)],
        compiler_params=_params(2, 56),
        name="swiglu_up",
    )(h, wgu, wgu, w_down)


def _mm_nt_kernel(a_ref, bt_ref, o_ref):
    o_ref[...] = lax.dot_general(a_ref[...], bt_ref[...], (((1,), (1,)), ((), ())), preferred_element_type=F32)


def _matmul_nt(a, bt, tm, tn, name, n_cols):
    m, k = a.shape
    return pl.pallas_call(
        _mm_nt_kernel,
        grid=(m // tm, n_cols // tn),
        in_specs=[pl.BlockSpec((tm, k), lambda i, j: (i, 0)),
                  pl.BlockSpec((tn, k), lambda i, j: (j, 0))],
        out_specs=pl.BlockSpec((tm, tn), lambda i, j: (i, j)),
        out_shape=jax.ShapeDtypeStruct((m, n_cols), F32),
        compiler_params=_params(2, 48),
        name=name,
    )(a, bt)


def _down_kernel(*refs, res_kind, n_res, tile_rows, gate_scale, with_adaln):
    a_ref, b_ref = refs[:2]
    x_refs = refs[2:2 + n_res]
    gate_ref = refs[2 + n_res]
    if with_adaln:
        c_ref, w_ref, bias_ref, z_ref, mod_ref = refs[3 + n_res:]
    else:
        (z_ref,) = refs[3 + n_res:]
    acc = _dot(a_ref[...], b_ref[...])
    _gated_residual(res_kind, x_refs, tile_rows, acc, gate_ref, gate_scale, z_ref)
    if with_adaln:
        mod_ref[0] = _adaln_tile(c_ref, w_ref, bias_ref)


def _ffn_down(a, w_down_bf16, res, gate, adaln=None):
    tm, tn = TR, 512
    m, k = a.shape
    n = w_down_bf16.shape[1]
    grid = (m // tm, n // tn)
    gate_arr, gate_spec = _seg_operand(gate, tm, tn, lambda i, j: (i, 0, j))
    res_args, res_specs = _residual_operands(res, tm, tn)
    in_specs = [pl.BlockSpec((tm, k), lambda i, j: (i, 0)),
                pl.BlockSpec((k, tn), lambda i, j: (0, j))] + res_specs + [gate_spec]
    args = [a, w_down_bf16, *res_args, gate_arr]
    out_specs = [pl.BlockSpec((tm, tn), lambda i, j: (i, j))]
    out_shape = [jax.ShapeDtypeStruct((m, n), F32)]
    if adaln is not None:
        c16, w_ada, b_ada, comp0, n_comp = adaln
        side_tn = 256
        n_side = n_comp * D_MODEL // side_tn
        assert n_side <= grid[0] * grid[1]
        side_in, side_out = _adaln_specs(side_tn, comp0, lambda i, j: jnp.minimum(i * grid[1] + j, n_side - 1))
        in_specs += side_in
        args += [c16, w_ada, b_ada.reshape(1, -1)]
        out_specs.append(side_out)
        out_shape.append(jax.ShapeDtypeStruct((n_comp, C_ROWS, D_MODEL), F32))
    return pl.pallas_call(
        functools.partial(_down_kernel, res_kind=res[0], n_res=len(res_args), tile_rows=tm, gate_scale=0.5,
                          with_adaln=adaln is not None),
        grid=grid,
        in_specs=in_specs,
        out_specs=out_specs,
        out_shape=out_shape,
        compiler_params=_params(2, 60),
        name="ffn_down" if adaln is None else "ffn_down_adaln",
    )(*args)


def _out_proj_kernel(a1_ref, a2_ref, b_ref, *refs, res_kind):
    x_refs, gate_ref, z_ref = refs[:-2], refs[-2], refs[-1]
    k1 = a1_ref.shape[1]
    acc = _dot(a1_ref[...], b_ref[:k1, :])
    acc += _dot(a2_ref[...], b_ref[k1:, :])
    _gated_residual(res_kind, x_refs, TM, acc, gate_ref, 1.0, z_ref)


def _out_proj(a1, a2, b, res, gate):
    tm, tn = TM, 512
    m, k1 = a1.shape
    k2 = a2.shape[1]
    n = b.shape[1]
    gate_arr, gate_spec = _seg_operand(gate, tm, tn, lambda i, j: (i, 0, j))
    res_args, res_specs = _residual_operands(res, tm, tn)
    return pl.pallas_call(
        functools.partial(_out_proj_kernel, res_kind=res[0]),
        grid=(m // tm, n // tn),
        in_specs=[pl.BlockSpec((tm, k1), lambda i, j: (i, 0)),
                  pl.BlockSpec((tm, k2), lambda i, j: (i, 0)),
                  pl.BlockSpec((k1 + k2, tn), lambda i, j: (0, j))] + res_specs + [gate_spec],
        out_specs=pl.BlockSpec((tm, tn), lambda i, j: (i, j)),
        out_shape=jax.ShapeDtypeStruct((m, n), F32),
        compiler_params=_params(2, 56),
        name="out_proj",
    )(a1, a2, b, *res_args, gate_arr)


def _rope128(blk, cc, ss):
    return blk * cc + pltpu.roll(blk, QK_ROPE, 1) * ss


def _q_kernel(cq_ref, g_ref, w_ref, cc_ref, ss_ref, o_ref):
    x = cq_ref[...]
    r = lax.rsqrt(jnp.mean(x * x, axis=-1, keepdims=True) + RMS_EPS)
    a = (x * r * g_ref[...]).astype(BF16)
    cc = cc_ref[...]
    ss = ss_ref[...]
    for h in range(HEADS):
        acc = _dot(a, w_ref[:, h * HEAD_PAD:(h + 1) * HEAD_PAD])
        o_ref[h, :, :LANES] = (acc[:, :LANES] * Q_SCALE).astype(o_ref.dtype)
        o_ref[h, :, LANES:] = (_rope128(acc[:, LANES:], cc, ss) * Q_SCALE).astype(o_ref.dtype)


def _q_proj(proj, q_norm_g, wuq, cc, ss):
    tm = TM // 2
    cq_block = (2 * GMLP_WIDTH) // Q_RANK
    return pl.pallas_call(
        _q_kernel,
        grid=(M_ROWS // tm,),
        in_specs=[pl.BlockSpec((tm, Q_RANK), lambda i: (i, cq_block)),
                  pl.BlockSpec((1, Q_RANK), lambda i: (0, 0)),
                  pl.BlockSpec((Q_RANK, HEADS * HEAD_PAD), lambda i: (0, 0)),
                  pl.BlockSpec((tm, LANES), lambda i: (i, 0)),
                  pl.BlockSpec((tm, LANES), lambda i: (i, 0))],
        out_specs=pl.BlockSpec((HEADS, tm, HEAD_PAD), lambda i: (0, i, 0)),
        out_shape=jax.ShapeDtypeStruct((HEADS, M_ROWS, HEAD_PAD), BF16),
        compiler_params=_params(1, 40),
        name="q_proj",
    )(proj, q_norm_g.reshape(1, Q_RANK), wuq, cc, ss)


def _latkr_kernel(ckv_ref, h_ref, wkr_ref, g_ref, cc_ref, ss_ref, lat_ref, kr_out_ref, kr128_ref):
    x = ckv_ref[...]
    r = lax.rsqrt(jnp.mean(x * x, axis=-1, keepdims=True) + RMS_EPS)
    lat_ref[...] = x * r * g_ref[...]
    blk = lax.dot_general(h_ref[...], wkr_ref[...], (((1,), (1,)), ((), ())), preferred_element_type=F32)
    rot = _rope128(blk, cc_ref[...], ss_ref[...])
    kr_out_ref[...] = rot[:, :QK_ROPE]
    kr128_ref[...] = rot


def _lat_krope(proj, h, wkr, kv_norm_g, cc, ss, row0, rows):
    tr = 512
    off = row0 // tr
    ckv_block = (2 * GMLP_WIDTH + Q_RANK) // KV_RANK
    return pl.pallas_call(
        _latkr_kernel,
        grid=(rows // tr,),
        in_specs=[pl.BlockSpec((tr, KV_RANK), lambda i: (i + off, ckv_block)),
                  pl.BlockSpec((tr, D_MODEL), lambda i: (i + off, 0)),
                  pl.BlockSpec((LANES, D_MODEL), lambda i: (0, 0)),
                  pl.BlockSpec((1, KV_RANK), lambda i: (0, 0)),
                  pl.BlockSpec((tr, LANES), lambda i: (i + off, 0)),
                  pl.BlockSpec((tr, LANES), lambda i: (i + off, 0))],
        out_specs=[pl.BlockSpec((tr, KV_RANK), lambda i: (i, 0)),
                   pl.BlockSpec((tr, QK_ROPE), lambda i: (i, 0)),
                   pl.BlockSpec((tr, LANES), lambda i: (i, 0))],
        out_shape=[jax.ShapeDtypeStruct((rows, KV_RANK), F32),
                   jax.ShapeDtypeStruct((rows, QK_ROPE), F32),
                   jax.ShapeDtypeStruct((rows, LANES), F32)],
        compiler_params=_params(1, 32),
        name="lat_krope",
    )(proj, h, wkr, kv_norm_g.reshape(1, KV_RANK), cc, ss)


def _kv_t_kernel(lat_ref, kr_ref, w_ref, wvt_ref, k_ref, vt_ref):
    a = lat_ref[...].astype(BF16)
    kr = kr_ref[...].astype(BF16)
    for h in range(HEADS):
        kn = _dot(a, w_ref[:, h * HEAD_PAD:h * HEAD_PAD + QK_NOPE])
        k_ref[h, :, :LANES] = kn.astype(BF16)
        k_ref[h, :, LANES:] = kr
        vt = lax.dot_general(wvt_ref[h], a, (((1,), (1,)), ((), ())), preferred_element_type=F32)
        vt_ref[h, 0] = vt.astype(BF16)


def _kv_proj_t(lat, kr128, wukv, wvt, tr):
    rows = lat.shape[0]
    return pl.pallas_call(
        _kv_t_kernel,
        grid=(rows // tr,),
        in_specs=[pl.BlockSpec((tr, KV_RANK), lambda i: (i, 0)),
                  pl.BlockSpec((tr, LANES), lambda i: (i, 0)),
                  pl.BlockSpec((KV_RANK, HEADS * HEAD_PAD), lambda i: (0, 0)),
                  pl.BlockSpec((HEADS, HEAD_V, KV_RANK), lambda i: (0, 0, 0))],
        out_specs=[pl.BlockSpec((HEADS, tr, HEAD_PAD), lambda i: (0, i, 0)),
                   pl.BlockSpec((HEADS, 1, HEAD_V, tr), lambda i: (0, i, 0, 0))],
        out_shape=[jax.ShapeDtypeStruct((HEADS, rows, HEAD_PAD), BF16),
                   jax.ShapeDtypeStruct((HEADS, rows // tr, HEAD_V, tr), BF16)],
        compiler_params=_params(1, 48),
        name="kv_proj_t",
    )(lat, kr128, wukv, wvt)


ATTN_TILE = 1024


def _softmax_pv_t(s, s_max, vt, carry):
    m, l, acc = carry
    m_new = jnp.maximum(m, s_max)
    p = jnp.exp2(s - m_new)
    a = jnp.exp2(m - m_new)
    l = a * l + jnp.sum(p, axis=0, keepdims=True)
    acc = a * acc + jnp.dot(vt, p.astype(BF16), preferred_element_type=F32)
    return m_new, l, acc


def _chunk_mask_t(q_rel0, nk, nq):
    kc = lax.broadcasted_iota(jnp.int32, (nk, nq), 0) >> CHUNK_SHIFT
    qc = (q_rel0 + lax.broadcasted_iota(jnp.int32, (nk, nq), 1)) >> CHUNK_SHIFT
    return qc >= kc


def _attn_prompt_kernel(q_ref, k_ref, vt_ref, o_ref, s_even, s_odd, *, rb, nh):
    qi = pl.program_id(1)
    n_rb = ATTN_TILE // rb
    heads = range(nh)

    def scores(j, s_ref):
        k0 = pl.multiple_of(j * ATTN_TILE, ATTN_TILE)
        maxima = []
        for hh in heads:
            s = lax.dot_general(k_ref[hh, pl.ds(k0, ATTN_TILE), :], q_ref[hh], (((1,), (1,)), ((), ())),
                                preferred_element_type=F32)
            s_ref[hh] = s
            maxima.append(jnp.max(s, axis=0, keepdims=True))
        return tuple(maxima)

    def make_step(s_cur, s_next):
        def step(j, carry):
            stats, s_max = carry
            next_max = scores(j + 1, s_next)
            stats = tuple(
                tuple(_softmax_pv_t(s_cur[hh, :, r * rb:(r + 1) * rb], s_max[hh][:, r * rb:(r + 1) * rb],
                                    vt_ref[hh, j], stats[hh][r]) for r in range(n_rb))
                for hh in heads)
            return stats, next_max
        return step

    step_even, step_odd = make_step(s_even, s_odd), make_step(s_odd, s_even)

    def finish(s_ref, stats):
        mask = _chunk_mask_t(0, rb, rb)
        for r in range(n_rb):
            nk = (r + 1) * rb
            cols = slice(r * rb, nk)
            for hh in heads:
                s_r = jnp.where(mask, s_ref[hh, r * rb:nk, cols], NEG)
                if r > 0:
                    s_r = jnp.concatenate([s_ref[hh, :r * rb, cols], s_r], axis=0)
                _, l, acc = _softmax_pv_t(s_r, jnp.max(s_r, axis=0, keepdims=True), vt_ref[hh, qi, :, :nk],
                                          stats[hh][r])
                o_ref[r * rb:(r + 1) * rb, hh * HEAD_V:(hh + 1) * HEAD_V] = (acc / l).T.astype(o_ref.dtype)

    @pl.when(qi < SEQ // ATTN_TILE)
    def _():
        init = tuple(tuple((jnp.full((1, rb), NEG, F32), jnp.zeros((1, rb), F32), jnp.zeros((HEAD_V, rb), F32))
                           for _ in range(n_rb)) for _ in heads)
        carry = (init, scores(0, s_even))
        carry = lax.fori_loop(0, qi >> 1, lambda p, c: step_odd(2 * p + 1, step_even(2 * p, c)), carry)
        odd = (qi & 1) == 1
        stats, _ = lax.cond(odd, lambda: step_even(qi - 1, carry), lambda: carry)
        lax.cond(odd, lambda: finish(s_odd, stats), lambda: finish(s_even, stats))

    @pl.when(qi == SEQ // ATTN_TILE)
    def _():
        o_ref[...] = jnp.zeros(o_ref.shape, o_ref.dtype)


def _attn_prompt(q, k, vt):
    nh = 2
    return pl.pallas_call(
        functools.partial(_attn_prompt_kernel, rb=256, nh=nh),
        grid=(HEADS // nh, SEQ // ATTN_TILE + 1),
        in_specs=[pl.BlockSpec((nh, ATTN_TILE, HEAD_PAD), lambda h, i: (h, jnp.minimum(i, SEQ // ATTN_TILE - 1), 0)),
                  pl.BlockSpec((nh, SEQ, HEAD_PAD), lambda h, i: (h, 0, 0)),
                  pl.BlockSpec((nh, SEQ // ATTN_TILE, HEAD_V, ATTN_TILE), lambda h, i: (h, 0, 0, 0))],
        out_specs=pl.BlockSpec((ATTN_TILE, nh * HEAD_V), lambda h, i: (i, h)),
        out_shape=jax.ShapeDtypeStruct((M_ROWS, HEADS * HEAD_V), BF16),
        scratch_shapes=[pltpu.VMEM((nh, ATTN_TILE, ATTN_TILE), F32), pltpu.VMEM((nh, ATTN_TILE, ATTN_TILE), F32)],
        compiler_params=_params(2, 56),
        name="attn_prompt",
    )(q, k, vt)


def _attn_sample_kernel(q_ref, cache_ref, new_ref, kr_ref, w_ref, o_prev_ref, o_ref, qa_scr):
    del o_prev_ref
    nt = (((1,), (1,)), ((), ()))
    pad = jnp.zeros((KV_PAD - PAST_LEN - DEC_SEQ, KV_RANK), F32)
    lat = jnp.concatenate([cache_ref[0], new_ref[...], pad], axis=0).astype(BF16)
    keys = jnp.concatenate([lat, kr_ref[0].astype(BF16)], axis=1)
    for h in range(HEADS):
        rows = slice(h * DEC_SEQ, (h + 1) * DEC_SEQ)
        w_uk = w_ref[:, h * HEAD_PAD:h * HEAD_PAD + QK_NOPE]
        q_lat = lax.dot_general(q_ref[h, :, :LANES], w_uk, nt, preferred_element_type=F32)
        qa_scr[rows, :KV_RANK] = q_lat.astype(BF16)
        qa_scr[rows, KV_RANK:] = q_ref[h, :, LANES:]
    s = lax.dot_general(qa_scr[...], keys, nt, preferred_element_type=F32)
    shape = (HEADS * DEC_SEQ, KV_PAD)
    q_pos = PAST_LEN + (lax.broadcasted_iota(jnp.int32, shape, 0) & (DEC_SEQ - 1))
    k_pos = lax.broadcasted_iota(jnp.int32, shape, 1)
    visible = ((q_pos >> CHUNK_SHIFT) >= (k_pos >> CHUNK_SHIFT)) & (k_pos < PAST_LEN + DEC_SEQ)
    s = jnp.where(visible, s, NEG)
    m = jnp.max(s, axis=-1, keepdims=True)
    p = jnp.exp2(s - m)
    l = jnp.sum(p, axis=-1, keepdims=True)
    o_lat = (jnp.dot(p.astype(BF16), lat, preferred_element_type=F32) / l).astype(BF16)
    for h in range(HEADS):
        w_uv = w_ref[:, h * HEAD_PAD + QK_NOPE:(h + 1) * HEAD_PAD]
        o_h = _dot(o_lat[h * DEC_SEQ:(h + 1) * DEC_SEQ, :], w_uv)
        o_ref[:, h * HEAD_V:(h + 1) * HEAD_V] = o_h.astype(o_ref.dtype)


def _attn_sample(q, cache_lat, lat_new, kr_all, wukv, o_prev):
    q_off = SEQ // DEC_SEQ
    return pl.pallas_call(
        _attn_sample_kernel,
        grid=(DEC_BATCH,),
        in_specs=[pl.BlockSpec((HEADS, DEC_SEQ, HEAD_PAD), lambda b: (0, q_off + b, 0)),
                  pl.BlockSpec((1, PAST_LEN, KV_RANK), lambda b: (b, 0, 0)),
                  pl.BlockSpec((DEC_SEQ, KV_RANK), lambda b: (b, 0)),
                  pl.BlockSpec((1, KV_PAD, LANES), lambda b: (b, 0, 0)),
                  pl.BlockSpec((KV_RANK, HEADS * HEAD_PAD), lambda b: (0, 0)),
                  pl.BlockSpec(memory_space=pl.ANY)],
        out_specs=pl.BlockSpec((DEC_SEQ, HEADS * HEAD_V), lambda b: (q_off + b, 0)),
        out_shape=jax.ShapeDtypeStruct((M_ROWS, HEADS * HEAD_V), BF16),
        scratch_shapes=[pltpu.VMEM((HEADS * DEC_SEQ, KV_RANK + LANES), BF16)],
        input_output_aliases={5: 0},
        compiler_params=_params(1, 48),
        name="attn_sample",
    )(q, cache_lat, lat_new, kr_all, wukv, o_prev)


def _gmlp_kernel(u_ref, v_ref, g_ref, b_ref, w_ref, bs_ref, *rest, chunk, emit_vn, n_own):
    if emit_vn:
        _, a_ref, vn_ref = rest
        _gmlp_tile(u_ref, v_ref, g_ref, b_ref, w_ref, bs_ref, a_ref, vn_ref, chunk)
    else:
        (a_ref,) = rest
        i = pl.program_id(0)
        pl.when(i < n_own)(lambda: _gmlp_tile(u_ref, v_ref, g_ref, b_ref, w_ref, bs_ref, a_ref, None, chunk))

        @pl.when(i >= n_own)
        def _():
            a_ref[...] = jnp.zeros(a_ref.shape, a_ref.dtype)


def _gmlp_tile(u_ref, v_ref, g_ref, b_ref, w_ref, bs_ref, a_ref, vn_ref, chunk):
    emit_vn = vn_ref is not None
    v = v_ref[...]
    mu = jnp.mean(v, axis=-1, keepdims=True)
    vc = v - mu
    var = jnp.mean(vc * vc, axis=-1, keepdims=True)
    vn = vc * lax.rsqrt(var + LN_EPS) * g_ref[...] + b_ref[...]
    if emit_vn:
        vn_ref[...] = vn
    vnb = vn.astype(BF16)
    n_chunks = v.shape[0] // chunk
    causal = (lax.broadcasted_iota(jnp.int32, (chunk, GMLP_CHUNK), 0)
              >= lax.broadcasted_iota(jnp.int32, (chunk, GMLP_CHUNK), 1))
    for g in range(GMLP_GROUPS):
        cols = slice(g * GROUP_DIM, (g + 1) * GROUP_DIM)
        w = jnp.where(causal, w_ref[g, :chunk, :], 0.0).astype(BF16)
        rhs = jnp.concatenate([vnb[c * chunk:(c + 1) * chunk, cols] for c in range(n_chunks)], axis=1)
        if chunk < GMLP_CHUNK:
            rhs = jnp.concatenate([rhs, jnp.zeros((GMLP_CHUNK - chunk, rhs.shape[1]), BF16)], axis=0)
        mixed = jnp.dot(w, rhs, preferred_element_type=F32) + bs_ref[g, :chunk, :]
        for c in range(n_chunks):
            rows = slice(c * chunk, (c + 1) * chunk)
            gate = mixed[:, c * GROUP_DIM:(c + 1) * GROUP_DIM]
            a_ref[rows, cols] = (u_ref[rows, cols] * gate).astype(a_ref.dtype)


def _gmlp(proj, ln_g, ln_b, w_s, b_s3, row0, rows, tr, chunk, a_prev=None):
    emit_vn = a_prev is not None
    off = row0 // tr
    n_own = rows // tr
    n_steps = n_own if emit_vn else M_ROWS // tr
    vec = pl.BlockSpec((1, GMLP_WIDTH), lambda i: (0, 0))
    in_specs = [pl.BlockSpec((tr, GMLP_WIDTH), lambda i: (i + off, 0)),
                pl.BlockSpec((tr, GMLP_WIDTH), lambda i: (i + off, 1)),
                vec, vec,
                pl.BlockSpec((GMLP_GROUPS, GMLP_CHUNK, GMLP_CHUNK), lambda i: (0, 0, 0)),
                pl.BlockSpec((GMLP_GROUPS, GMLP_CHUNK, 1), lambda i: (0, 0, 0))]
    args = [proj, proj, ln_g.reshape(1, GMLP_WIDTH), ln_b.reshape(1, GMLP_WIDTH), w_s, b_s3]
    out_specs = [pl.BlockSpec((tr, GMLP_WIDTH), lambda i: (i + off, 0))]
    out_shape = [jax.ShapeDtypeStruct((M_ROWS, GMLP_WIDTH), BF16)]
    aliases = {}
    if emit_vn:
        in_specs.append(pl.BlockSpec(memory_space=pl.ANY))
        args.append(a_prev)
        aliases = {len(args) - 1: 0}
        out_specs.append(pl.BlockSpec((tr, GMLP_WIDTH), lambda i: (i, 0)))
        out_shape.append(jax.ShapeDtypeStruct((rows, GMLP_WIDTH), F32))
    return pl.pallas_call(
        functools.partial(_gmlp_kernel, chunk=chunk, emit_vn=emit_vn, n_own=n_own),
        grid=(n_steps,),
        in_specs=in_specs,
        out_specs=out_specs,
        out_shape=out_shape,
        input_output_aliases=aliases,
        compiler_params=_params(1, 40),
        name="gmlp_gate_vn" if emit_vn else "gmlp_gate",
    )(*args)


def _swap_halves(w):
    half = w.shape[-1] // 2
    return jnp.concatenate([w[..., half:], w[..., :half]], axis=-1)


def _rope_tables():
    pos = jnp.concatenate([jnp.arange(SEQ, dtype=jnp.int32),
                           jnp.tile(PAST_LEN + jnp.arange(DEC_SEQ, dtype=jnp.int32), DEC_BATCH)])
    inv = 1.0 / (ROPE_THETA ** (jnp.arange(0, QK_ROPE, 2, dtype=F32) / QK_ROPE))
    ang = pos.astype(F32)[:, None] * inv[None, :]
    cos, sin = jnp.cos(ang), jnp.sin(ang)
    zero = jnp.zeros((M_ROWS, LANES - QK_ROPE), F32)
    return (jnp.concatenate([cos, cos, zero], axis=1), jnp.concatenate([-sin, sin, zero], axis=1))


def kernel(x_prompt, x_sample, cache_mla_latent, cache_mla_krope, c_prompt, c_sample, w_ada, b_ada, ffn1_w_gate_up, ffn1_w_down, ln1_g, ln1_b, w_in, gmlp_ln_g, gmlp_ln_b, gmlp_w_s, gmlp_b_s, mla_q_norm_g, mla_w_uq, mla_kv_norm_g, mla_w_ukv, w_out, ln2_g, ln2_b, ffn2_w_gate_up, ffn2_w_down, ln3_g, ln3_b):
    wukv = mla_w_ukv.astype(BF16)
    wvt = mla_w_ukv.reshape(KV_RANK, HEADS, HEAD_PAD)[:, :, QK_NOPE:].transpose(1, 2, 0).astype(BF16)
    w_in_t = w_in.T
    w_kr_t = w_in_t[IN_COLS:]
    wkr = jnp.concatenate([w_kr_t, w_kr_t[QK_ROPE // 2:], w_kr_t[:QK_ROPE // 2]], axis=0)
    uq = mla_w_uq.reshape(Q_RANK, HEADS, QK_NOPE + QK_ROPE)
    wuq = jnp.concatenate([uq, _swap_halves(uq[..., QK_NOPE:])], axis=-1).reshape(Q_RANK, HEADS * HEAD_PAD).astype(BF16)
    b_s3 = gmlp_b_s[:, :, None]
    cc, ss = _rope_tables()

    c16 = jnp.concatenate([c_prompt, c_sample, jnp.zeros((C_ROWS - 1 - DEC_BATCH, D_MODEL), F32)], axis=0)
    n_first = 3
    first = _segment_table(_modulation(c16, w_ada, b_ada, 0, n_first))
    SH1, SC1, G1 = ((first, i) for i in range(n_first))

    x0 = (x_prompt.reshape(SEQ, D_MODEL), x_sample.reshape(S_ROWS, D_MODEL))

    h1 = _modulate(x0, SC1, SH1)
    z1, mod_rest = _ffn_down(*_swiglu_up(h1, ffn1_w_gate_up, ffn1_w_down), _raw_residual(*x0), G1,
                             adaln=(c16, w_ada, b_ada, n_first, N_MOD - n_first))
    rest = _segment_table(mod_rest)
    SH2, SC2, G2, SH3, SC3, G3 = ((rest, i) for i in range(N_MOD - n_first))
    h2, mu1, rstd1 = _layernorm(z1, ln1_g, ln1_b, SC2, SH2)

    proj = _matmul_nt(h2, w_in_t, TM, IN_TILE, "in_proj", n_cols=IN_COLS)
    (a_mix,) = _gmlp(proj, gmlp_ln_g, gmlp_ln_b, gmlp_w_s, b_s3, 0, SEQ, 512, GMLP_CHUNK)
    a_mix, vn_s = _gmlp(proj, gmlp_ln_g, gmlp_ln_b, gmlp_w_s, b_s3, SEQ, S_ROWS, DEC_SEQ, DEC_SEQ, a_prev=a_mix)
    q = _q_proj(proj, mla_q_norm_g, wuq, cc, ss)
    lat_p, kr_p, kr128_p = _lat_krope(proj, h2, wkr, mla_kv_norm_g, cc, ss, 0, SEQ)
    lat_s, kr_s, kr128_s = _lat_krope(proj, h2, wkr, mla_kv_norm_g, cc, ss, SEQ, S_ROWS)

    k_p, vt_p = _kv_proj_t(lat_p, kr128_p, wukv, wvt, ATTN_TILE)
    o_mix = _attn_prompt(q, k_p, vt_p)

    pad_rows = KV_PAD - PAST_LEN - DEC_SEQ
    cache_kr128 = jnp.pad(cache_mla_krope, ((0, 0), (0, 0), (0, LANES - QK_ROPE)))
    kr_all = jnp.concatenate([cache_kr128, kr128_s.reshape(DEC_BATCH, DEC_SEQ, LANES),
                              jnp.zeros((DEC_BATCH, pad_rows, LANES), F32)], axis=1)
    o_mix = _attn_sample(q, cache_mla_latent, lat_s, kr_all, wukv, o_mix)

    z2 = _out_proj(a_mix, o_mix, w_out, _ln_residual(z1, mu1, rstd1, ln1_g, ln1_b), G2)
    h3, mu2, rstd2 = _layernorm(z2, ln2_g, ln2_b, SC3, SH3)

    (z3,) = _ffn_down(*_swiglu_up(h3, ffn2_w_gate_up, ffn2_w_down), _ln_residual(z2, mu2, rstd2, ln2_g, ln2_b), G3)
    (y_p,) = _layernorm(z3, ln3_g, ln3_b, row0=0, rows=SEQ)
    (y_s,) = _layernorm(z3, ln3_g, ln3_b, row0=SEQ, rows=S_ROWS)

    return (y_p.reshape(1, SEQ, D_MODEL),
            y_s.reshape(DEC_BATCH, DEC_SEQ, D_MODEL),
            lat_p.reshape(1, SEQ, KV_RANK),
            kr_p.reshape(1, SEQ, QK_ROPE),
            lat_s.reshape(DEC_BATCH, DEC_SEQ, KV_RANK),
            kr_s.reshape(DEC_BATCH, DEC_SEQ, QK_ROPE),
            vn_s.reshape(DEC_BATCH, DEC_SEQ, GMLP_WIDTH))
```

```python
import functools

import jax
import jax.numpy as jnp
from jax import lax
from jax.experimental import pallas as pl
from jax.experimental.pallas import tpu as pltpu

F32 = jnp.float32
BF16 = jnp.bfloat16

D_MODEL = 4096
SEQ = 8192
DEC_BATCH = 8
DEC_SEQ = 64
PAST_LEN = 2048
CHUNK = 64
CHUNK_SHIFT = 6
GMLP_CHUNK = 128
GROUP_DIM = 128
GMLP_WIDTH = D_MODEL // 2
GMLP_GROUPS = GMLP_WIDTH // GROUP_DIM
HEAD_V = 128
QK_NOPE = 128
QK_ROPE = 64
HEADS = (D_MODEL - GMLP_WIDTH) // HEAD_V
KV_RANK = 512
Q_RANK = D_MODEL // 4
D_FF = 256 * ((8 * D_MODEL // 3 + 255) // 256)
N_MOD = 9
ROPE_THETA = 10000.0
LN_EPS = 1e-5
RMS_EPS = 1e-6
ALPHA = 2.0 ** 0.25
Q_SCALE = (QK_NOPE + QK_ROPE) ** -0.5 * 1.4426950408889634
NEG = -1e30

LANES = 128
HEAD_PAD = 2 * LANES
S_ROWS = DEC_BATCH * DEC_SEQ
M_ROWS = SEQ + S_ROWS
SEG = DEC_SEQ
N_SEG = M_ROWS // SEG
TM = M_ROWS // 8
TR = 8 * SEG
IN_COLS = 2 * GMLP_WIDTH + Q_RANK + KV_RANK
IN_TILE = 512
KV_PAD = 17 * LANES
MIB = 2 ** 20


def _params(n_axes, vmem_mib):
    return pltpu.CompilerParams(dimension_semantics=("arbitrary",) * n_axes,
                                vmem_limit_bytes=vmem_mib * MIB)


C_ROWS = 16


def _adaln_tile(c_ref, w_ref, b_ref):
    c = c_ref[...]
    s = (c * jax.nn.sigmoid(c)).astype(BF16)
    return lax.dot_general(s, w_ref[...], (((1,), (0,)), ((), ())), preferred_element_type=F32) + b_ref[...]


def _adaln_specs(tn, comp0, block_of_step):
    per_comp = D_MODEL // tn

    def col(*idx):
        return comp0 * per_comp + block_of_step(*idx)

    in_specs = [pl.BlockSpec((C_ROWS, D_MODEL), lambda *idx: (0, 0)),
                pl.BlockSpec((D_MODEL, tn), lambda *idx: (0, col(*idx))),
                pl.BlockSpec((1, tn), lambda *idx: (0, col(*idx)))]
    out_spec = pl.BlockSpec((1, C_ROWS, tn),
                            lambda *idx: (block_of_step(*idx) // per_comp, 0, block_of_step(*idx) % per_comp))
    return in_specs, out_spec


def _mod_kernel(c_ref, w_ref, b_ref, o_ref):
    o_ref[0] = _adaln_tile(c_ref, w_ref, b_ref)


def _modulation(c16, w_ada, b_ada, comp0, n_comp):
    tn = 512
    in_specs, out_spec = _adaln_specs(tn, comp0, lambda j: j)
    return pl.pallas_call(
        _mod_kernel,
        grid=(n_comp * D_MODEL // tn,),
        in_specs=in_specs,
        out_specs=out_spec,
        out_shape=jax.ShapeDtypeStruct((n_comp, C_ROWS, D_MODEL), F32),
        compiler_params=_params(1, 40),
        name="adaln_mod",
    )(c16, w_ada, b_ada.reshape(1, -1))


def _segment_table(mod):
    n = mod.shape[0]
    return jnp.concatenate([jnp.broadcast_to(mod[:, 0:1], (n, SEQ // SEG, D_MODEL)), mod[:, 1:1 + DEC_BATCH]], axis=1)


def _seg_operand(entry, tile_rows, width, index_map):
    table, comp = entry
    view = table.reshape(table.shape[0], M_ROWS // tile_rows, tile_rows // SEG, D_MODEL)
    spec = pl.BlockSpec((1, 1, tile_rows // SEG, width), lambda *idx: (comp,) + tuple(index_map(*idx)))
    return view, spec


def _split_x_specs(tile_rows, width, col):
    n_prompt = SEQ // tile_rows
    return [pl.BlockSpec((tile_rows, width), lambda *idx: (jnp.minimum(idx[0], n_prompt - 1), col(*idx))),
            pl.BlockSpec((tile_rows, width), lambda *idx: (jnp.maximum(idx[0] - n_prompt, 0), col(*idx)))]


def _on_owner(x_refs, tile_rows, body):
    if len(x_refs) == 1:
        body(x_refs[0])
    else:
        i = pl.program_id(0)
        n_prompt = SEQ // tile_rows
        pl.when(i < n_prompt)(lambda: body(x_refs[0]))
        pl.when(i >= n_prompt)(lambda: body(x_refs[1]))


def _modulate_kernel(xp_ref, xs_ref, sc_ref, sh_ref, h_ref):
    def body(x_ref):
        for s in range(TR // SEG):
            rows = slice(s * SEG, (s + 1) * SEG)
            h = x_ref[rows, :] * (1.0 + sc_ref[0, 0, s:s + 1, :]) + sh_ref[0, 0, s:s + 1, :]
            h_ref[rows, :] = h.astype(h_ref.dtype)

    _on_owner((xp_ref, xs_ref), TR, body)


def _modulate(xs, sc, sh):
    sc_arr, sc_spec = _seg_operand(sc, TR, D_MODEL, lambda i: (i, 0, 0))
    sh_arr, sh_spec = _seg_operand(sh, TR, D_MODEL, lambda i: (i, 0, 0))
    return pl.pallas_call(
        _modulate_kernel,
        grid=(M_ROWS // TR,),
        in_specs=_split_x_specs(TR, D_MODEL, lambda i: 0) + [sc_spec, sh_spec],
        out_specs=pl.BlockSpec((TR, D_MODEL), lambda i: (i, 0)),
        out_shape=jax.ShapeDtypeStruct((M_ROWS, D_MODEL), BF16),
        compiler_params=_params(1, 48),
        name="modulate",
    )(*xs, sc_arr, sh_arr)


def _raw_residual(*xs):
    return ("raw",) + xs


def _ln_residual(z, mu, rstd, ln_g, ln_b):
    return ("ln", z, mu, rstd, ln_g.reshape(1, -1), ln_b.reshape(1, -1))


def _residual_operands(res, tm, tn):
    tile = pl.BlockSpec((tm, tn), lambda i, j: (i, j))
    if res[0] == "raw":
        xs = list(res[1:])
        return xs, (_split_x_specs(tm, tn, lambda i, j: j) if len(xs) == 2 else [tile])
    stat = pl.BlockSpec((tm, LANES), lambda i, j: (i, 0))
    vec = pl.BlockSpec((1, tn), lambda i, j: (0, j))
    return list(res[1:]), [tile, stat, stat, vec, vec]


def _residual_rows(kind, refs, tile_rows, rows):
    if kind == "ln":
        z_ref, mu_ref, rstd_ref, g_ref, b_ref = refs
        return (z_ref[rows, :] - mu_ref[rows, :1]) * rstd_ref[rows, :1] * g_ref[...] + b_ref[...]
    x = refs[0][rows, :]
    if len(refs) == 2:
        x = jnp.where(pl.program_id(0) < SEQ // tile_rows, x, refs[1][rows, :])
    return x


def _gated_residual(kind, x_refs, tile_rows, acc, gate_ref, gate_scale, z_ref):
    for s in range(acc.shape[0] // SEG):
        rows = slice(s * SEG, (s + 1) * SEG)
        x = _residual_rows(kind, x_refs, tile_rows, rows)
        z_ref[rows, :] = ALPHA * x + (gate_scale * gate_ref[0, 0, s:s + 1, :]) * acc[rows, :]


def _ln_kernel(z_ref, g_ref, b_ref, *rest, with_h):
    if with_h:
        sc_ref, sh_ref, h_ref, mu_ref, rstd_ref = rest
    else:
        (xo_ref,) = rest
    for s in range(TR // SEG):
        rows = slice(s * SEG, (s + 1) * SEG)
        z = z_ref[rows, :]
        mu = jnp.mean(z, axis=-1, keepdims=True)
        zc = z - mu
        var = jnp.mean(zc * zc, axis=-1, keepdims=True)
        rstd = lax.rsqrt(var + LN_EPS)
        xn = zc * rstd * g_ref[...] + b_ref[...]
        if with_h:
            h = xn * (1.0 + sc_ref[0, 0, s:s + 1, :]) + sh_ref[0, 0, s:s + 1, :]
            h_ref[rows, :] = h.astype(h_ref.dtype)
            mu_ref[rows, :] = jnp.broadcast_to(mu, (SEG, LANES))
            rstd_ref[rows, :] = jnp.broadcast_to(rstd, (SEG, LANES))
        else:
            xo_ref[rows, :] = xn


def _layernorm(z, ln_g, ln_b, sc=None, sh=None, *, row0=0, rows=M_ROWS):
    with_h = sc is not None
    off = row0 // TR
    vec = pl.BlockSpec((1, D_MODEL), lambda i: (0, 0))
    in_specs = [pl.BlockSpec((TR, D_MODEL), lambda i: (i + off, 0)), vec, vec]
    args = [z, ln_g.reshape(1, D_MODEL), ln_b.reshape(1, D_MODEL)]
    row_out = pl.BlockSpec((TR, D_MODEL), lambda i: (i, 0))
    if with_h:
        for entry in (sc, sh):
            arr, spec = _seg_operand(entry, TR, D_MODEL, lambda i: (i + off, 0, 0))
            in_specs.append(spec)
            args.append(arr)
        stat = pl.BlockSpec((TR, LANES), lambda i: (i, 0))
        out_specs = [row_out, stat, stat]
        out_shape = [jax.ShapeDtypeStruct((rows, D_MODEL), BF16),
                     jax.ShapeDtypeStruct((rows, LANES), F32), jax.ShapeDtypeStruct((rows, LANES), F32)]
    else:
        out_specs = [row_out]
        out_shape = [jax.ShapeDtypeStruct((rows, D_MODEL), F32)]
    return pl.pallas_call(
        functools.partial(_ln_kernel, with_h=with_h),
        grid=(rows // TR,),
        in_specs=in_specs,
        out_specs=out_specs,
        out_shape=out_shape,
        compiler_params=_params(1, 56),
        name="ln_mod" if with_h else "ln",
    )(*args)


def _dot(a, b):
    return lax.dot_general(a, b, (((1,), (0,)), ((), ())), preferred_element_type=F32)


def _gateup_kernel(h_ref, wg_ref, wu_ref, wd_ref, o_ref, wd_out_ref):
    h = h_ref[...]
    g = _dot(h, wg_ref[...])
    u = _dot(h, wu_ref[...])
    o_ref[...] = (g * jax.nn.sigmoid(g) * u).astype(o_ref.dtype)

    @pl.when(pl.program_id(0) == 0)
    def _():
        wd_out_ref[...] = wd_ref[...].astype(wd_out_ref.dtype)


def _swiglu_up(h, wgu, w_down):
    tm = 2 * TM
    tn = 256
    nj = D_FF // tn
    slab = pl.BlockSpec((tn, D_MODEL), lambda i, j: (jnp.where(i == 0, j, nj - 1), 0))
    return pl.pallas_call(
        _gateup_kernel,
        grid=(M_ROWS // tm, nj),
        in_specs=[pl.BlockSpec((tm, D_MODEL), lambda i, j: (i, 0), pipeline_mode=pl.Buffered(1)),
                  pl.BlockSpec((D_MODEL, tn), lambda i, j: (0, j)),
                  pl.BlockSpec((D_MODEL, tn), lambda i, j: (0, j + nj)),
                  slab],
        out_specs=[pl.BlockSpec((tm, tn), lambda i, j: (i, j)), slab],
        out_shape=[jax.ShapeDtypeStruct((M_ROWS, D_FF), BF16),
                   jax.ShapeDtypeStruct((D_FF, D_MODEL), BF16)],
        compiler_params=_params(2, 56),
        name="swiglu_up",
    )(h, wgu, wgu, w_down)


def _mm_nt_kernel(a_ref, bt_ref, o_ref):
    o_ref[...] = lax.dot_general(a_ref[...], bt_ref[...], (((1,), (1,)), ((), ())), preferred_element_type=F32)


def _matmul_nt(a, bt, tm, tn, name, n_cols):
    m, k = a.shape
    return pl.pallas_call(
        _mm_nt_kernel,
        grid=(m // tm, n_cols // tn),
        in_specs=[pl.BlockSpec((tm, k), lambda i, j: (i, 0)),
                  pl.BlockSpec((tn, k), lambda i, j: (j, 0))],
        out_specs=pl.BlockSpec((tm, tn), lambda i, j: (i, j)),
        out_shape=jax.ShapeDtypeStruct((m, n_cols), F32),
        compiler_params=_params(2, 48),
        name=name,
    )(a, bt)


def _down_kernel(*refs, res_kind, n_res, tile_rows, gate_scale, with_adaln):
    a_ref, b_ref = refs[:2]
    x_refs = refs[2:2 + n_res]
    gate_ref = refs[2 + n_res]
    if with_adaln:
        c_ref, w_ref, bias_ref, z_ref, mod_ref = refs[3 + n_res:]
    else:
        (z_ref,) = refs[3 + n_res:]
    acc = _dot(a_ref[...], b_ref[...])
    _gated_residual(res_kind, x_refs, tile_rows, acc, gate_ref, gate_scale, z_ref)
    if with_adaln:
        mod_ref[0] = _adaln_tile(c_ref, w_ref, bias_ref)


def _ffn_down(a, w_down_bf16, res, gate, adaln=None):
    tm, tn = TR, 512
    m, k = a.shape
    n = w_down_bf16.shape[1]
    grid = (m // tm, n // tn)
    gate_arr, gate_spec = _seg_operand(gate, tm, tn, lambda i, j: (i, 0, j))
    res_args, res_specs = _residual_operands(res, tm, tn)
    in_specs = [pl.BlockSpec((tm, k), lambda i, j: (i, 0)),
                pl.BlockSpec((k, tn), lambda i, j: (0, j))] + res_specs + [gate_spec]
    args = [a, w_down_bf16, *res_args, gate_arr]
    out_specs = [pl.BlockSpec((tm, tn), lambda i, j: (i, j))]
    out_shape = [jax.ShapeDtypeStruct((m, n), F32)]
    if adaln is not None:
        c16, w_ada, b_ada, comp0, n_comp = adaln
        side_tn = 256
        n_side = n_comp * D_MODEL // side_tn
        assert n_side <= grid[0] * grid[1]
        side_in, side_out = _adaln_specs(side_tn, comp0, lambda i, j: jnp.minimum(i * grid[1] + j, n_side - 1))
        in_specs += side_in
        args += [c16, w_ada, b_ada.reshape(1, -1)]
        out_specs.append(side_out)
        out_shape.append(jax.ShapeDtypeStruct((n_comp, C_ROWS, D_MODEL), F32))
    return pl.pallas_call(
        functools.partial(_down_kernel, res_kind=res[0], n_res=len(res_args), tile_rows=tm, gate_scale=0.5,
                          with_adaln=adaln is not None),
        grid=grid,
        in_specs=in_specs,
        out_specs=out_specs,
        out_shape=out_shape,
        compiler_params=_params(2, 60),
        name="ffn_down" if adaln is None else "ffn_down_adaln",
    )(*args)


def _out_proj_kernel(a1_ref, a2_ref, b_ref, *refs, res_kind):
    x_refs, gate_ref, z_ref = refs[:-2], refs[-2], refs[-1]
    k1 = a1_ref.shape[1]
    acc = _dot(a1_ref[...], b_ref[:k1, :])
    acc += _dot(a2_ref[...], b_ref[k1:, :])
    _gated_residual(res_kind, x_refs, TM, acc, gate_ref, 1.0, z_ref)


def _out_proj(a1, a2, b, res, gate):
    tm, tn = TM, 512
    m, k1 = a1.shape
    k2 = a2.shape[1]
    n = b.shape[1]
    gate_arr, gate_spec = _seg_operand(gate, tm, tn, lambda i, j: (i, 0, j))
    res_args, res_specs = _residual_operands(res, tm, tn)
    return pl.pallas_call(
        functools.partial(_out_proj_kernel, res_kind=res[0]),
        grid=(m // tm, n // tn),
        in_specs=[pl.BlockSpec((tm, k1), lambda i, j: (i, 0)),
                  pl.BlockSpec((tm, k2), lambda i, j: (i, 0)),
                  pl.BlockSpec((k1 + k2, tn), lambda i, j: (0, j))] + res_specs + [gate_spec],
        out_specs=pl.BlockSpec((tm, tn), lambda i, j: (i, j)),
        out_shape=jax.ShapeDtypeStruct((m, n), F32),
        compiler_params=_params(2, 56),
        name="out_proj",
    )(a1, a2, b, *res_args, gate_arr)


def _rope128(blk, cc, ss):
    return blk * cc + pltpu.roll(blk, QK_ROPE, 1) * ss


def _q_kernel(cq_ref, g_ref, w_ref, cc_ref, ss_ref, o_ref):
    x = cq_ref[...]
    r = lax.rsqrt(jnp.mean(x * x, axis=-1, keepdims=True) + RMS_EPS)
    a = (x * r * g_ref[...]).astype(BF16)
    cc = cc_ref[...]
    ss = ss_ref[...]
    for h in range(HEADS):
        acc = _dot(a, w_ref[:, h * HEAD_PAD:(h + 1) * HEAD_PAD])
        o_ref[h, :, :LANES] = (acc[:, :LANES] * Q_SCALE).astype(o_ref.dtype)
        o_ref[h, :, LANES:] = (_rope128(acc[:, LANES:], cc, ss) * Q_SCALE).astype(o_ref.dtype)


def _q_proj(proj, q_norm_g, wuq, cc, ss):
    tm = TM // 2
    cq_block = (2 * GMLP_WIDTH) // Q_RANK
    return pl.pallas_call(
        _q_kernel,
        grid=(M_ROWS // tm,),
        in_specs=[pl.BlockSpec((tm, Q_RANK), lambda i: (i, cq_block)),
                  pl.BlockSpec((1, Q_RANK), lambda i: (0, 0)),
                  pl.BlockSpec((Q_RANK, HEADS * HEAD_PAD), lambda i: (0, 0)),
                  pl.BlockSpec((tm, LANES), lambda i: (i, 0)),
                  pl.BlockSpec((tm, LANES), lambda i: (i, 0))],
        out_specs=pl.BlockSpec((HEADS, tm, HEAD_PAD), lambda i: (0, i, 0)),
        out_shape=jax.ShapeDtypeStruct((HEADS, M_ROWS, HEAD_PAD), BF16),
        compiler_params=_params(1, 40),
        name="q_proj",
    )(proj, q_norm_g.reshape(1, Q_RANK), wuq, cc, ss)


def _latkr_kernel(ckv_ref, h_ref, wkr_ref, g_ref, cc_ref, ss_ref, lat_ref, kr_out_ref, kr128_ref):
    x = ckv_ref[...]
    r = lax.rsqrt(jnp.mean(x * x, axis=-1, keepdims=True) + RMS_EPS)
    lat_ref[...] = x * r * g_ref[...]
    blk = lax.dot_general(h_ref[...], wkr_ref[...], (((1,), (1,)), ((), ())), preferred_element_type=F32)
    rot = _rope128(blk, cc_ref[...], ss_ref[...])
    kr_out_ref[...] = rot[:, :QK_ROPE]
    kr128_ref[...] = rot


def _lat_krope(proj, h, wkr, kv_norm_g, cc, ss, row0, rows):
    tr = 512
    off = row0 // tr
    ckv_block = (2 * GMLP_WIDTH + Q_RANK) // KV_RANK
    return pl.pallas_call(
        _latkr_kernel,
        grid=(rows // tr,),
        in_specs=[pl.BlockSpec((tr, KV_RANK), lambda i: (i + off, ckv_block)),
                  pl.BlockSpec((tr, D_MODEL), lambda i: (i + off, 0)),
                  pl.BlockSpec((LANES, D_MODEL), lambda i: (0, 0)),
                  pl.BlockSpec((1, KV_RANK), lambda i: (0, 0)),
                  pl.BlockSpec((tr, LANES), lambda i: (i + off, 0)),
                  pl.BlockSpec((tr, LANES), lambda i: (i + off, 0))],
        out_specs=[pl.BlockSpec((tr, KV_RANK), lambda i: (i, 0)),
                   pl.BlockSpec((tr, QK_ROPE), lambda i: (i, 0)),
                   pl.BlockSpec((tr, LANES), lambda i: (i, 0))],
        out_shape=[jax.ShapeDtypeStruct((rows, KV_RANK), F32),
                   jax.ShapeDtypeStruct((rows, QK_ROPE), F32),
                   jax.ShapeDtypeStruct((rows, LANES), F32)],
        compiler_params=_params(1, 32),
        name="lat_krope",
    )(proj, h, wkr, kv_norm_g.reshape(1, KV_RANK), cc, ss)


def _kv_t_kernel(lat_ref, kr_ref, w_ref, wvt_ref, k_ref, vt_ref):
    a = lat_ref[...].astype(BF16)
    kr = kr_ref[...].astype(BF16)
    for h in range(HEADS):
        kn = _dot(a, w_ref[:, h * HEAD_PAD:h * HEAD_PAD + QK_NOPE])
        k_ref[h, :, :LANES] = kn.astype(BF16)
        k_ref[h, :, LANES:] = kr
        vt = lax.dot_general(wvt_ref[h], a, (((1,), (1,)), ((), ())), preferred_element_type=F32)
        vt_ref[h, 0] = vt.astype(BF16)


def _kv_proj_t(lat, kr128, wukv, wvt, tr):
    rows = lat.shape[0]
    return pl.pallas_call(
        _kv_t_kernel,
        grid=(rows // tr,),
        in_specs=[pl.BlockSpec((tr, KV_RANK), lambda i: (i, 0)),
                  pl.BlockSpec((tr, LANES), lambda i: (i, 0)),
                  pl.BlockSpec((KV_RANK, HEADS * HEAD_PAD), lambda i: (0, 0)),
                  pl.BlockSpec((HEADS, HEAD_V, KV_RANK), lambda i: (0, 0, 0))],
        out_specs=[pl.BlockSpec((HEADS, tr, HEAD_PAD), lambda i: (0, i, 0)),
                   pl.BlockSpec((HEADS, 1, HEAD_V, tr), lambda i: (0, i, 0, 0))],
        out_shape=[jax.ShapeDtypeStruct((HEADS, rows, HEAD_PAD), BF16),
                   jax.ShapeDtypeStruct((HEADS, rows // tr, HEAD_V, tr), BF16)],
        compiler_params=_params(1, 48),
        name="kv_proj_t",
    )(lat, kr128, wukv, wvt)


ATTN_TILE = 1024


def _softmax_pv_t(s, s_max, vt, carry):
    m, l, acc = carry
    m_new = jnp.maximum(m, s_max)
    p = jnp.exp2(s - m_new)
    a = jnp.exp2(m - m_new)
    l = a * l + jnp.sum(p, axis=0, keepdims=True)
    acc = a * acc + jnp.dot(vt, p.astype(BF16), preferred_element_type=F32)
    return m_new, l, acc


def _chunk_mask_t(q_rel0, nk, nq):
    kc = lax.broadcasted_iota(jnp.int32, (nk, nq), 0) >> CHUNK_SHIFT
    qc = (q_rel0 + lax.broadcasted_iota(jnp.int32, (nk, nq), 1)) >> CHUNK_SHIFT
    return qc >= kc


def _attn_prompt_kernel(q_ref, k_ref, vt_ref, o_init_ref, o_ref, s_even, s_odd, *, rb, nh):
    del o_init_ref
    qi = pl.program_id(1)
    n_rb = ATTN_TILE // rb
    heads = range(nh)

    def scores(j, s_ref):
        k0 = pl.multiple_of(j * ATTN_TILE, ATTN_TILE)
        maxima = []
        for hh in heads:
            s = lax.dot_general(k_ref[hh, pl.ds(k0, ATTN_TILE), :], q_ref[hh], (((1,), (1,)), ((), ())),
                                preferred_element_type=F32)
            s_ref[hh] = s
            maxima.append(jnp.max(s, axis=0, keepdims=True))
        return tuple(maxima)

    def make_step(s_cur, s_next):
        def step(j, carry):
            stats, s_max = carry
            next_max = scores(j + 1, s_next)
            stats = tuple(
                tuple(_softmax_pv_t(s_cur[hh, :, r * rb:(r + 1) * rb], s_max[hh][:, r * rb:(r + 1) * rb],
                                    vt_ref[hh, j], stats[hh][r]) for r in range(n_rb))
                for hh in heads)
            return stats, next_max
        return step

    step_even, step_odd = make_step(s_even, s_odd), make_step(s_odd, s_even)

    def finish(s_ref, stats):
        mask = _chunk_mask_t(0, rb, rb)
        for r in range(n_rb):
            nk = (r + 1) * rb
            cols = slice(r * rb, nk)
            for hh in heads:
                s_r = jnp.where(mask, s_ref[hh, r * rb:nk, cols], NEG)
                if r > 0:
                    s_r = jnp.concatenate([s_ref[hh, :r * rb, cols], s_r], axis=0)
                _, l, acc = _softmax_pv_t(s_r, jnp.max(s_r, axis=0, keepdims=True), vt_ref[hh, qi, :, :nk],
                                          stats[hh][r])
                o_ref[r * rb:(r + 1) * rb, hh * HEAD_V:(hh + 1) * HEAD_V] = (acc / l).T.astype(o_ref.dtype)

    init = tuple(tuple((jnp.full((1, rb), NEG, F32), jnp.zeros((1, rb), F32), jnp.zeros((HEAD_V, rb), F32))
                       for _ in range(n_rb)) for _ in heads)
    carry = (init, scores(0, s_even))
    carry = lax.fori_loop(0, qi >> 1, lambda p, c: step_odd(2 * p + 1, step_even(2 * p, c)), carry)
    odd = (qi & 1) == 1
    stats, _ = lax.cond(odd, lambda: step_even(qi - 1, carry), lambda: carry)
    lax.cond(odd, lambda: finish(s_odd, stats), lambda: finish(s_even, stats))


def _attn_prompt(q, k, vt, o_init):
    nh = 2
    return pl.pallas_call(
        functools.partial(_attn_prompt_kernel, rb=256, nh=nh),
        grid=(HEADS // nh, SEQ // ATTN_TILE),
        in_specs=[pl.BlockSpec((nh, ATTN_TILE, HEAD_PAD), lambda h, i: (h, i, 0)),
                  pl.BlockSpec((nh, SEQ, HEAD_PAD), lambda h, i: (h, 0, 0)),
                  pl.BlockSpec((nh, SEQ // ATTN_TILE, HEAD_V, ATTN_TILE), lambda h, i: (h, 0, 0, 0)),
                  pl.BlockSpec(memory_space=pl.ANY)],
        out_specs=pl.BlockSpec((ATTN_TILE, nh * HEAD_V), lambda h, i: (i, h)),
        out_shape=jax.ShapeDtypeStruct((M_ROWS, HEADS * HEAD_V), BF16),
        scratch_shapes=[pltpu.VMEM((nh, ATTN_TILE, ATTN_TILE), F32), pltpu.VMEM((nh, ATTN_TILE, ATTN_TILE), F32)],
        input_output_aliases={3: 0},
        compiler_params=_params(2, 56),
        name="attn_prompt",
    )(q, k, vt, o_init)


def _attn_sample_kernel(q_ref, cache_ref, new_ref, kr_ref, w_ref, o_prev_ref, o_ref, qa_scr):
    del o_prev_ref
    nt = (((1,), (1,)), ((), ()))
    pad = jnp.zeros((KV_PAD - PAST_LEN - DEC_SEQ, KV_RANK), F32)
    lat = jnp.concatenate([cache_ref[0], new_ref[...], pad], axis=0).astype(BF16)
    keys = jnp.concatenate([lat, kr_ref[0].astype(BF16)], axis=1)
    for h in range(HEADS):
        rows = slice(h * DEC_SEQ, (h + 1) * DEC_SEQ)
        w_uk = w_ref[:, h * HEAD_PAD:h * HEAD_PAD + QK_NOPE]
        q_lat = lax.dot_general(q_ref[h, :, :LANES], w_uk, nt, preferred_element_type=F32)
        qa_scr[rows, :KV_RANK] = q_lat.astype(BF16)
        qa_scr[rows, KV_RANK:] = q_ref[h, :, LANES:]
    s = lax.dot_general(qa_scr[...], keys, nt, preferred_element_type=F32)
    shape = (HEADS * DEC_SEQ, KV_PAD)
    q_pos = PAST_LEN + (lax.broadcasted_iota(jnp.int32, shape, 0) & (DEC_SEQ - 1))
    k_pos = lax.broadcasted_iota(jnp.int32, shape, 1)
    visible = ((q_pos >> CHUNK_SHIFT) >= (k_pos >> CHUNK_SHIFT)) & (k_pos < PAST_LEN + DEC_SEQ)
    s = jnp.where(visible, s, NEG)
    m = jnp.max(s, axis=-1, keepdims=True)
    p = jnp.exp2(s - m)
    l = jnp.sum(p, axis=-1, keepdims=True)
    o_lat = (jnp.dot(p.astype(BF16), lat, preferred_element_type=F32) / l).astype(BF16)
    for h in range(HEADS):
        w_uv = w_ref[:, h * HEAD_PAD + QK_NOPE:(h + 1) * HEAD_PAD]
        o_h = _dot(o_lat[h * DEC_SEQ:(h + 1) * DEC_SEQ, :], w_uv)
        o_ref[:, h * HEAD_V:(h + 1) * HEAD_V] = o_h.astype(o_ref.dtype)


def _attn_sample(q, cache_lat, lat_new, kr_all, wukv, o_prev):
    q_off = SEQ // DEC_SEQ
    return pl.pallas_call(
        _attn_sample_kernel,
        grid=(DEC_BATCH,),
        in_specs=[pl.BlockSpec((HEADS, DEC_SEQ, HEAD_PAD), lambda b: (0, q_off + b, 0)),
                  pl.BlockSpec((1, PAST_LEN, KV_RANK), lambda b: (b, 0, 0)),
                  pl.BlockSpec((DEC_SEQ, KV_RANK), lambda b: (b, 0)),
                  pl.BlockSpec((1, KV_PAD, LANES), lambda b: (b, 0, 0)),
                  pl.BlockSpec((KV_RANK, HEADS * HEAD_PAD), lambda b: (0, 0)),
                  pl.BlockSpec(memory_space=pl.ANY)],
        out_specs=pl.BlockSpec((DEC_SEQ, HEADS * HEAD_V), lambda b: (q_off + b, 0)),
        out_shape=jax.ShapeDtypeStruct((M_ROWS, HEADS * HEAD_V), BF16),
        scratch_shapes=[pltpu.VMEM((HEADS * DEC_SEQ, KV_RANK + LANES), BF16)],
        input_output_aliases={5: 0},
        compiler_params=_params(1, 48),
        name="attn_sample",
    )(q, cache_lat, lat_new, kr_all, wukv, o_prev)


def _gmlp_kernel(u_ref, v_ref, g_ref, b_ref, w_ref, bs_ref, *rest, chunk, emit_vn, n_own):
    if emit_vn:
        _, a_ref, vn_ref = rest
        _gmlp_tile(u_ref, v_ref, g_ref, b_ref, w_ref, bs_ref, a_ref, vn_ref, chunk)
    else:
        (a_ref,) = rest
        i = pl.program_id(0)
        pl.when(i < n_own)(lambda: _gmlp_tile(u_ref, v_ref, g_ref, b_ref, w_ref, bs_ref, a_ref, None, chunk))

        @pl.when(i >= n_own)
        def _():
            a_ref[...] = jnp.zeros(a_ref.shape, a_ref.dtype)


def _gmlp_tile(u_ref, v_ref, g_ref, b_ref, w_ref, bs_ref, a_ref, vn_ref, chunk):
    emit_vn = vn_ref is not None
    v = v_ref[...]
    mu = jnp.mean(v, axis=-1, keepdims=True)
    vc = v - mu
    var = jnp.mean(vc * vc, axis=-1, keepdims=True)
    vn = vc * lax.rsqrt(var + LN_EPS) * g_ref[...] + b_ref[...]
    if emit_vn:
        vn_ref[...] = vn
    vnb = vn.astype(BF16)
    n_chunks = v.shape[0] // chunk
    causal = (lax.broadcasted_iota(jnp.int32, (chunk, GMLP_CHUNK), 0)
              >= lax.broadcasted_iota(jnp.int32, (chunk, GMLP_CHUNK), 1))
    for g in range(GMLP_GROUPS):
        cols = slice(g * GROUP_DIM, (g + 1) * GROUP_DIM)
        w = jnp.where(causal, w_ref[g, :chunk, :], 0.0).astype(BF16)
        rhs = jnp.concatenate([vnb[c * chunk:(c + 1) * chunk, cols] for c in range(n_chunks)], axis=1)
        if chunk < GMLP_CHUNK:
            rhs = jnp.concatenate([rhs, jnp.zeros((GMLP_CHUNK - chunk, rhs.shape[1]), BF16)], axis=0)
        mixed = jnp.dot(w, rhs, preferred_element_type=F32) + bs_ref[g, :chunk, :]
        for c in range(n_chunks):
            rows = slice(c * chunk, (c + 1) * chunk)
            gate = mixed[:, c * GROUP_DIM:(c + 1) * GROUP_DIM]
            a_ref[rows, cols] = (u_ref[rows, cols] * gate).astype(a_ref.dtype)


def _gmlp(proj, ln_g, ln_b, w_s, b_s3, row0, rows, tr, chunk, a_prev=None):
    emit_vn = a_prev is not None
    off = row0 // tr
    n_own = rows // tr
    n_steps = n_own if emit_vn else M_ROWS // tr
    vec = pl.BlockSpec((1, GMLP_WIDTH), lambda i: (0, 0))
    in_specs = [pl.BlockSpec((tr, GMLP_WIDTH), lambda i: (i + off, 0)),
                pl.BlockSpec((tr, GMLP_WIDTH), lambda i: (i + off, 1)),
                vec, vec,
                pl.BlockSpec((GMLP_GROUPS, GMLP_CHUNK, GMLP_CHUNK), lambda i: (0, 0, 0)),
                pl.BlockSpec((GMLP_GROUPS, GMLP_CHUNK, 1), lambda i: (0, 0, 0))]
    args = [proj, proj, ln_g.reshape(1, GMLP_WIDTH), ln_b.reshape(1, GMLP_WIDTH), w_s, b_s3]
    out_specs = [pl.BlockSpec((tr, GMLP_WIDTH), lambda i: (i + off, 0))]
    out_shape = [jax.ShapeDtypeStruct((M_ROWS, GMLP_WIDTH), BF16)]
    aliases = {}
    if emit_vn:
        in_specs.append(pl.BlockSpec(memory_space=pl.ANY))
        args.append(a_prev)
        aliases = {len(args) - 1: 0}
        out_specs.append(pl.BlockSpec((tr, GMLP_WIDTH), lambda i: (i, 0)))
        out_shape.append(jax.ShapeDtypeStruct((rows, GMLP_WIDTH), F32))
    return pl.pallas_call(
        functools.partial(_gmlp_kernel, chunk=chunk, emit_vn=emit_vn, n_own=n_own),
        grid=(n_steps,),
        in_specs=in_specs,
        out_specs=out_specs,
        out_shape=out_shape,
        input_output_aliases=aliases,
        compiler_params=_params(1, 40),
        name="gmlp_gate_vn" if emit_vn else "gmlp_gate",
    )(*args)


def _swap_halves(w):
    half = w.shape[-1] // 2
    return jnp.concatenate([w[..., half:], w[..., :half]], axis=-1)


def _rope_tables():
    pos = jnp.concatenate([jnp.arange(SEQ, dtype=jnp.int32),
                           jnp.tile(PAST_LEN + jnp.arange(DEC_SEQ, dtype=jnp.int32), DEC_BATCH)])
    inv = 1.0 / (ROPE_THETA ** (jnp.arange(0, QK_ROPE, 2, dtype=F32) / QK_ROPE))
    ang = pos.astype(F32)[:, None] * inv[None, :]
    cos, sin = jnp.cos(ang), jnp.sin(ang)
    zero = jnp.zeros((M_ROWS, LANES - QK_ROPE), F32)
    return (jnp.concatenate([cos, cos, zero], axis=1), jnp.concatenate([-sin, sin, zero], axis=1))


def kernel(x_prompt, x_sample, cache_mla_latent, cache_mla_krope, c_prompt, c_sample, w_ada, b_ada, ffn1_w_gate_up, ffn1_w_down, ln1_g, ln1_b, w_in, gmlp_ln_g, gmlp_ln_b, gmlp_w_s, gmlp_b_s, mla_q_norm_g, mla_w_uq, mla_kv_norm_g, mla_w_ukv, w_out, ln2_g, ln2_b, ffn2_w_gate_up, ffn2_w_down, ln3_g, ln3_b):
    wukv = mla_w_ukv.astype(BF16)
    wvt = mla_w_ukv.reshape(KV_RANK, HEADS, HEAD_PAD)[:, :, QK_NOPE:].transpose(1, 2, 0).astype(BF16)
    w_in_t = w_in.T
    w_kr_t = w_in_t[IN_COLS:]
    wkr = jnp.concatenate([w_kr_t, w_kr_t[QK_ROPE // 2:], w_kr_t[:QK_ROPE // 2]], axis=0)
    uq = mla_w_uq.reshape(Q_RANK, HEADS, QK_NOPE + QK_ROPE)
    wuq = jnp.concatenate([uq, _swap_halves(uq[..., QK_NOPE:])], axis=-1).reshape(Q_RANK, HEADS * HEAD_PAD).astype(BF16)
    b_s3 = gmlp_b_s[:, :, None]
    cc, ss = _rope_tables()

    c16 = jnp.concatenate([c_prompt, c_sample, jnp.zeros((C_ROWS - 1 - DEC_BATCH, D_MODEL), F32)], axis=0)
    n_first = 3
    first = _segment_table(_modulation(c16, w_ada, b_ada, 0, n_first))
    SH1, SC1, G1 = ((first, i) for i in range(n_first))

    x0 = (x_prompt.reshape(SEQ, D_MODEL), x_sample.reshape(S_ROWS, D_MODEL))

    h1 = _modulate(x0, SC1, SH1)
    z1, mod_rest = _ffn_down(*_swiglu_up(h1, ffn1_w_gate_up, ffn1_w_down), _raw_residual(*x0), G1,
                             adaln=(c16, w_ada, b_ada, n_first, N_MOD - n_first))
    rest = _segment_table(mod_rest)
    SH2, SC2, G2, SH3, SC3, G3 = ((rest, i) for i in range(N_MOD - n_first))
    h2, mu1, rstd1 = _layernorm(z1, ln1_g, ln1_b, SC2, SH2)

    proj = _matmul_nt(h2, w_in_t, TM, IN_TILE, "in_proj", n_cols=IN_COLS)
    (a_mix,) = _gmlp(proj, gmlp_ln_g, gmlp_ln_b, gmlp_w_s, b_s3, 0, SEQ, 512, GMLP_CHUNK)
    a_mix, vn_s = _gmlp(proj, gmlp_ln_g, gmlp_ln_b, gmlp_w_s, b_s3, SEQ, S_ROWS, DEC_SEQ, DEC_SEQ, a_prev=a_mix)
    q = _q_proj(proj, mla_q_norm_g, wuq, cc, ss)
    lat_p, kr_p, kr128_p = _lat_krope(proj, h2, wkr, mla_kv_norm_g, cc, ss, 0, SEQ)
    lat_s, kr_s, kr128_s = _lat_krope(proj, h2, wkr, mla_kv_norm_g, cc, ss, SEQ, S_ROWS)

    k_p, vt_p = _kv_proj_t(lat_p, kr128_p, wukv, wvt, ATTN_TILE)
    o_mix = _attn_prompt(q, k_p, vt_p, jnp.zeros((M_ROWS, HEADS * HEAD_V), BF16))

    pad_rows = KV_PAD - PAST_LEN - DEC_SEQ
    cache_kr128 = jnp.pad(cache_mla_krope, ((0, 0), (0, 0), (0, LANES - QK_ROPE)))
    kr_all = jnp.concatenate([cache_kr128, kr128_s.reshape(DEC_BATCH, DEC_SEQ, LANES),
                              jnp.zeros((DEC_BATCH, pad_rows, LANES), F32)], axis=1)
    o_mix = _attn_sample(q, cache_mla_latent, lat_s, kr_all, wukv, o_mix)

    z2 = _out_proj(a_mix, o_mix, w_out, _ln_residual(z1, mu1, rstd1, ln1_g, ln1_b), G2)
    h3, mu2, rstd2 = _layernorm(z2, ln2_g, ln2_b, SC3, SH3)

    (z3,) = _ffn_down(*_swiglu_up(h3, ffn2_w_gate_up, ffn2_w_down), _ln_residual(z2, mu2, rstd2, ln2_g, ln2_b), G3)
    (y_p,) = _layernorm(z3, ln3_g, ln3_b, row0=0, rows=SEQ)
    (y_s,) = _layernorm(z3, ln3_g, ln3_b, row0=SEQ, rows=S_ROWS)

    return (y_p.reshape(1, SEQ, D_MODEL),
            y_s.reshape(DEC_BATCH, DEC_SEQ, D_MODEL),
            lat_p.reshape(1, SEQ, KV_RANK),
            kr_p.reshape(1, SEQ, QK_ROPE),
            lat_s.reshape(DEC_BATCH, DEC_SEQ, KV_RANK),
            kr_s.reshape(DEC_BATCH, DEC_SEQ, QK_ROPE),
            vn_s.reshape(DEC_BATCH, DEC_SEQ, GMLP_WIDTH))
```

```python
import functools

import jax
import jax.numpy as jnp
from jax import lax
from jax.experimental import pallas as pl
from jax.experimental.pallas import tpu as pltpu

F32 = jnp.float32
BF16 = jnp.bfloat16

D_MODEL = 4096
SEQ = 8192
DEC_BATCH = 8
DEC_SEQ = 64
PAST_LEN = 2048
CHUNK = 64
CHUNK_SHIFT = CHUNK.bit_length() - 1
GMLP_CHUNK = 128
GROUP_DIM = 128
GMLP_WIDTH = D_MODEL // 2
GMLP_GROUPS = GMLP_WIDTH // GROUP_DIM
HEAD_V = 128
QK_NOPE = 128
QK_ROPE = 64
HEADS = (D_MODEL - GMLP_WIDTH) // HEAD_V
KV_RANK = 512
Q_RANK = D_MODEL // 4
D_FF = 256 * ((8 * D_MODEL // 3 + 255) // 256)
N_MOD = 9
ROPE_THETA = 10000.0
LN_EPS = 1e-5
RMS_EPS = 1e-6
ALPHA = 2.0 ** 0.25
Q_SCALE = (QK_NOPE + QK_ROPE) ** -0.5 * 1.4426950408889634
NEG = -1e30

LANES = 128
HEAD_PAD = 2 * LANES
S_ROWS = DEC_BATCH * DEC_SEQ
M_ROWS = SEQ + S_ROWS
SEG = DEC_SEQ
TM = M_ROWS // 8
TR = 8 * SEG
IN_COLS = 2 * GMLP_WIDTH + Q_RANK + KV_RANK
IN_TILE = 512
KV_PAD = -(-(PAST_LEN + DEC_SEQ) // LANES) * LANES
MIB = 2 ** 20


def _params(n_axes, vmem_mib):
    return pltpu.CompilerParams(dimension_semantics=("arbitrary",) * n_axes,
                                vmem_limit_bytes=vmem_mib * MIB)


C_ROWS = 16


def _adaln_tile(c_ref, w_ref, b_ref):
    c = c_ref[...]
    s = (c * jax.nn.sigmoid(c)).astype(BF16)
    return lax.dot_general(s, w_ref[...], (((1,), (0,)), ((), ())), preferred_element_type=F32) + b_ref[...]


def _adaln_specs(tn, comp0, block_of_step):
    per_comp = D_MODEL // tn

    def col(*idx):
        return comp0 * per_comp + block_of_step(*idx)

    in_specs = [pl.BlockSpec((C_ROWS, D_MODEL), lambda *idx: (0, 0)),
                pl.BlockSpec((D_MODEL, tn), lambda *idx: (0, col(*idx))),
                pl.BlockSpec((1, tn), lambda *idx: (0, col(*idx)))]
    out_spec = pl.BlockSpec((1, C_ROWS, tn),
                            lambda *idx: (block_of_step(*idx) // per_comp, 0, block_of_step(*idx) % per_comp))
    return in_specs, out_spec


def _mod_kernel(c_ref, w_ref, b_ref, o_ref):
    o_ref[0] = _adaln_tile(c_ref, w_ref, b_ref)


def _modulation(c16, w_ada, b_ada, comp0, n_comp):
    tn = 512
    in_specs, out_spec = _adaln_specs(tn, comp0, lambda j: j)
    return pl.pallas_call(
        _mod_kernel,
        grid=(n_comp * D_MODEL // tn,),
        in_specs=in_specs,
        out_specs=out_spec,
        out_shape=jax.ShapeDtypeStruct((n_comp, C_ROWS, D_MODEL), F32),
        compiler_params=_params(1, 40),
        name="adaln_mod",
    )(c16, w_ada, b_ada.reshape(1, -1))


def _segment_table(mod):
    n = mod.shape[0]
    return jnp.concatenate([jnp.broadcast_to(mod[:, 0:1], (n, SEQ // SEG, D_MODEL)), mod[:, 1:1 + DEC_BATCH]], axis=1)


def _seg_operand(entry, tile_rows, width, index_map):
    table, comp = entry
    view = table.reshape(table.shape[0], M_ROWS // tile_rows, tile_rows // SEG, D_MODEL)
    spec = pl.BlockSpec((1, 1, tile_rows // SEG, width), lambda *idx: (comp,) + tuple(index_map(*idx)))
    return view, spec


def _split_x_specs(tile_rows, width, col):
    n_prompt = SEQ // tile_rows
    return [pl.BlockSpec((tile_rows, width), lambda *idx: (jnp.minimum(idx[0], n_prompt - 1), col(*idx))),
            pl.BlockSpec((tile_rows, width), lambda *idx: (jnp.maximum(idx[0] - n_prompt, 0), col(*idx)))]


def _on_owner(x_refs, tile_rows, body):
    if len(x_refs) == 1:
        body(x_refs[0])
    else:
        i = pl.program_id(0)
        n_prompt = SEQ // tile_rows
        pl.when(i < n_prompt)(lambda: body(x_refs[0]))
        pl.when(i >= n_prompt)(lambda: body(x_refs[1]))


def _modulate_kernel(xp_ref, xs_ref, sc_ref, sh_ref, h_ref):
    def body(x_ref):
        for s in range(TR // SEG):
            rows = slice(s * SEG, (s + 1) * SEG)
            h = x_ref[rows, :] * (1.0 + sc_ref[0, 0, s:s + 1, :]) + sh_ref[0, 0, s:s + 1, :]
            h_ref[rows, :] = h.astype(h_ref.dtype)

    _on_owner((xp_ref, xs_ref), TR, body)


def _modulate(xs, sc, sh):
    sc_arr, sc_spec = _seg_operand(sc, TR, D_MODEL, lambda i: (i, 0, 0))
    sh_arr, sh_spec = _seg_operand(sh, TR, D_MODEL, lambda i: (i, 0, 0))
    return pl.pallas_call(
        _modulate_kernel,
        grid=(M_ROWS // TR,),
        in_specs=_split_x_specs(TR, D_MODEL, lambda i: 0) + [sc_spec, sh_spec],
        out_specs=pl.BlockSpec((TR, D_MODEL), lambda i: (i, 0)),
        out_shape=jax.ShapeDtypeStruct((M_ROWS, D_MODEL), BF16),
        compiler_params=_params(1, 48),
        name="modulate",
    )(*xs, sc_arr, sh_arr)


def _raw_residual(*xs):
    return ("raw",) + xs


def _ln_residual(z, mu, rstd, ln_g, ln_b):
    return ("ln", z, mu, rstd, ln_g.reshape(1, -1), ln_b.reshape(1, -1))


def _residual_operands(res, tm, tn):
    tile = pl.BlockSpec((tm, tn), lambda i, j: (i, j))
    if res[0] == "raw":
        xs = list(res[1:])
        return xs, (_split_x_specs(tm, tn, lambda i, j: j) if len(xs) == 2 else [tile])
    stat = pl.BlockSpec((tm, LANES), lambda i, j: (i, 0))
    vec = pl.BlockSpec((1, tn), lambda i, j: (0, j))
    return list(res[1:]), [tile, stat, stat, vec, vec]


def _residual_rows(kind, refs, tile_rows, rows):
    if kind == "ln":
        z_ref, mu_ref, rstd_ref, g_ref, b_ref = refs
        return (z_ref[rows, :] - mu_ref[rows, :1]) * rstd_ref[rows, :1] * g_ref[...] + b_ref[...]
    x = refs[0][rows, :]
    if len(refs) == 2:
        x = jnp.where(pl.program_id(0) < SEQ // tile_rows, x, refs[1][rows, :])
    return x


def _gated_residual(kind, x_refs, tile_rows, acc, gate_ref, gate_scale, z_ref):
    for s in range(acc.shape[0] // SEG):
        rows = slice(s * SEG, (s + 1) * SEG)
        x = _residual_rows(kind, x_refs, tile_rows, rows)
        z_ref[rows, :] = ALPHA * x + (gate_scale * gate_ref[0, 0, s:s + 1, :]) * acc[rows, :]


def _ln_kernel(z_ref, g_ref, b_ref, *rest, with_h):
    if with_h:
        sc_ref, sh_ref, h_ref, mu_ref, rstd_ref = rest
    else:
        (xo_ref,) = rest
    for s in range(TR // SEG):
        rows = slice(s * SEG, (s + 1) * SEG)
        z = z_ref[rows, :]
        mu = jnp.mean(z, axis=-1, keepdims=True)
        zc = z - mu
        var = jnp.mean(zc * zc, axis=-1, keepdims=True)
        rstd = lax.rsqrt(var + LN_EPS)
        xn = zc * rstd * g_ref[...] + b_ref[...]
        if with_h:
            h = xn * (1.0 + sc_ref[0, 0, s:s + 1, :]) + sh_ref[0, 0, s:s + 1, :]
            h_ref[rows, :] = h.astype(h_ref.dtype)
            mu_ref[rows, :] = jnp.broadcast_to(mu, (SEG, LANES))
            rstd_ref[rows, :] = jnp.broadcast_to(rstd, (SEG, LANES))
        else:
            xo_ref[rows, :] = xn


def _layernorm(z, ln_g, ln_b, sc=None, sh=None, *, row0=0, rows=M_ROWS):
    with_h = sc is not None
    off = row0 // TR
    vec = pl.BlockSpec((1, D_MODEL), lambda i: (0, 0))
    in_specs = [pl.BlockSpec((TR, D_MODEL), lambda i: (i + off, 0)), vec, vec]
    args = [z, ln_g.reshape(1, D_MODEL), ln_b.reshape(1, D_MODEL)]
    row_out = pl.BlockSpec((TR, D_MODEL), lambda i: (i, 0))
    if with_h:
        for entry in (sc, sh):
            arr, spec = _seg_operand(entry, TR, D_MODEL, lambda i: (i + off, 0, 0))
            in_specs.append(spec)
            args.append(arr)
        stat = pl.BlockSpec((TR, LANES), lambda i: (i, 0))
        out_specs = [row_out, stat, stat]
        out_shape = [jax.ShapeDtypeStruct((rows, D_MODEL), BF16),
                     jax.ShapeDtypeStruct((rows, LANES), F32), jax.ShapeDtypeStruct((rows, LANES), F32)]
    else:
        out_specs = [row_out]
        out_shape = [jax.ShapeDtypeStruct((rows, D_MODEL), F32)]
    return pl.pallas_call(
        functools.partial(_ln_kernel, with_h=with_h),
        grid=(rows // TR,),
        in_specs=in_specs,
        out_specs=out_specs,
        out_shape=out_shape,
        compiler_params=_params(1, 56),
        name="ln_mod" if with_h else "ln",
    )(*args)


def _dot(a, b):
    return lax.dot_general(a, b, (((1,), (0,)), ((), ())), preferred_element_type=F32)


def _gateup_kernel(h_ref, wg_ref, wu_ref, wd_ref, o_ref, wd_out_ref):
    h = h_ref[...]
    g = _dot(h, wg_ref[...])
    u = _dot(h, wu_ref[...])
    o_ref[...] = (g * jax.nn.sigmoid(g) * u).astype(o_ref.dtype)

    @pl.when(pl.program_id(0) == 0)
    def _():
        wd_out_ref[...] = wd_ref[...].astype(wd_out_ref.dtype)


def _swiglu_up(h, wgu, w_down):
    tm = 2 * TM
    tn = 256
    nj = D_FF // tn
    slab = pl.BlockSpec((tn, D_MODEL), lambda i, j: (jnp.where(i == 0, j, nj - 1), 0))
    return pl.pallas_call(
        _gateup_kernel,
        grid=(M_ROWS // tm, nj),
        in_specs=[pl.BlockSpec((tm, D_MODEL), lambda i, j: (i, 0), pipeline_mode=pl.Buffered(1)),
                  pl.BlockSpec((D_MODEL, tn), lambda i, j: (0, j)),
                  pl.BlockSpec((D_MODEL, tn), lambda i, j: (0, j + nj)),
                  slab],
        out_specs=[pl.BlockSpec((tm, tn), lambda i, j: (i, j)), slab],
        out_shape=[jax.ShapeDtypeStruct((M_ROWS, D_FF), BF16),
                   jax.ShapeDtypeStruct((D_FF, D_MODEL), BF16)],
        compiler_params=_params(2, 56),
        name="swiglu_up",
    )(h, wgu, wgu, w_down)


def _mm_nt_kernel(a_ref, bt_ref, o_ref):
    o_ref[...] = lax.dot_general(a_ref[...], bt_ref[...], (((1,), (1,)), ((), ())), preferred_element_type=F32)


def _matmul_nt(a, bt, tm, tn, name, n_cols):
    m, k = a.shape
    return pl.pallas_call(
        _mm_nt_kernel,
        grid=(m // tm, n_cols // tn),
        in_specs=[pl.BlockSpec((tm, k), lambda i, j: (i, 0)),
                  pl.BlockSpec((tn, k), lambda i, j: (j, 0))],
        out_specs=pl.BlockSpec((tm, tn), lambda i, j: (i, j)),
        out_shape=jax.ShapeDtypeStruct((m, n_cols), F32),
        compiler_params=_params(2, 48),
        name=name,
    )(a, bt)


def _down_kernel(*refs, res_kind, n_res, tile_rows, gate_scale, with_adaln):
    a_ref, b_ref = refs[:2]
    x_refs = refs[2:2 + n_res]
    gate_ref = refs[2 + n_res]
    if with_adaln:
        c_ref, w_ref, bias_ref, z_ref, mod_ref = refs[3 + n_res:]
    else:
        (z_ref,) = refs[3 + n_res:]
    acc = _dot(a_ref[...], b_ref[...])
    _gated_residual(res_kind, x_refs, tile_rows, acc, gate_ref, gate_scale, z_ref)
    if with_adaln:
        mod_ref[0] = _adaln_tile(c_ref, w_ref, bias_ref)


def _ffn_down(a, w_down_bf16, res, gate, adaln=None):
    tm, tn = TR, 512
    m, k = a.shape
    n = w_down_bf16.shape[1]
    grid = (m // tm, n // tn)
    gate_arr, gate_spec = _seg_operand(gate, tm, tn, lambda i, j: (i, 0, j))
    res_args, res_specs = _residual_operands(res, tm, tn)
    in_specs = [pl.BlockSpec((tm, k), lambda i, j: (i, 0)),
                pl.BlockSpec((k, tn), lambda i, j: (0, j))] + res_specs + [gate_spec]
    args = [a, w_down_bf16, *res_args, gate_arr]
    out_specs = [pl.BlockSpec((tm, tn), lambda i, j: (i, j))]
    out_shape = [jax.ShapeDtypeStruct((m, n), F32)]
    if adaln is not None:
        c16, w_ada, b_ada, comp0, n_comp = adaln
        side_tn = 256
        n_side = n_comp * D_MODEL // side_tn
        assert n_side <= grid[0] * grid[1]
        side_in, side_out = _adaln_specs(side_tn, comp0, lambda i, j: jnp.minimum(i * grid[1] + j, n_side - 1))
        in_specs += side_in
        args += [c16, w_ada, b_ada.reshape(1, -1)]
        out_specs.append(side_out)
        out_shape.append(jax.ShapeDtypeStruct((n_comp, C_ROWS, D_MODEL), F32))
    return pl.pallas_call(
        functools.partial(_down_kernel, res_kind=res[0], n_res=len(res_args), tile_rows=tm, gate_scale=0.5,
                          with_adaln=adaln is not None),
        grid=grid,
        in_specs=in_specs,
        out_specs=out_specs,
        out_shape=out_shape,
        compiler_params=_params(2, 60),
        name="ffn_down" if adaln is None else "ffn_down_adaln",
    )(*args)


def _out_proj_kernel(a1_ref, a2_ref, b_ref, *refs, res_kind):
    x_refs, gate_ref, z_ref = refs[:-2], refs[-2], refs[-1]
    k1 = a1_ref.shape[1]
    acc = _dot(a1_ref[...], b_ref[:k1, :])
    acc += _dot(a2_ref[...], b_ref[k1:, :])
    _gated_residual(res_kind, x_refs, TM, acc, gate_ref, 1.0, z_ref)


def _out_proj(a1, a2, b, res, gate):
    tm, tn = TM, 512
    m, k1 = a1.shape
    k2 = a2.shape[1]
    n = b.shape[1]
    gate_arr, gate_spec = _seg_operand(gate, tm, tn, lambda i, j: (i, 0, j))
    res_args, res_specs = _residual_operands(res, tm, tn)
    return pl.pallas_call(
        functools.partial(_out_proj_kernel, res_kind=res[0]),
        grid=(m // tm, n // tn),
        in_specs=[pl.BlockSpec((tm, k1), lambda i, j: (i, 0)),
                  pl.BlockSpec((tm, k2), lambda i, j: (i, 0)),
                  pl.BlockSpec((k1 + k2, tn), lambda i, j: (0, j))] + res_specs + [gate_spec],
        out_specs=pl.BlockSpec((tm, tn), lambda i, j: (i, j)),
        out_shape=jax.ShapeDtypeStruct((m, n), F32),
        compiler_params=_params(2, 56),
        name="out_proj",
    )(a1, a2, b, *res_args, gate_arr)


def _rope128(blk, cc, ss):
    return blk * cc + pltpu.roll(blk, QK_ROPE, 1) * ss


def _q_kernel(cq_ref, g_ref, w_ref, cc_ref, ss_ref, o_ref):
    x = cq_ref[...]
    r = lax.rsqrt(jnp.mean(x * x, axis=-1, keepdims=True) + RMS_EPS)
    a = (x * r * g_ref[...]).astype(BF16)
    cc = cc_ref[...]
    ss = ss_ref[...]
    for h in range(HEADS):
        acc = _dot(a, w_ref[:, h * HEAD_PAD:(h + 1) * HEAD_PAD])
        o_ref[h, :, :LANES] = (acc[:, :LANES] * Q_SCALE).astype(o_ref.dtype)
        o_ref[h, :, LANES:] = (_rope128(acc[:, LANES:], cc, ss) * Q_SCALE).astype(o_ref.dtype)


def _q_proj(proj, q_norm_g, wuq, cc, ss):
    tm = TM // 2
    cq_block = (2 * GMLP_WIDTH) // Q_RANK
    return pl.pallas_call(
        _q_kernel,
        grid=(M_ROWS // tm,),
        in_specs=[pl.BlockSpec((tm, Q_RANK), lambda i: (i, cq_block)),
                  pl.BlockSpec((1, Q_RANK), lambda i: (0, 0)),
                  pl.BlockSpec((Q_RANK, HEADS * HEAD_PAD), lambda i: (0, 0)),
                  pl.BlockSpec((tm, LANES), lambda i: (i, 0)),
                  pl.BlockSpec((tm, LANES), lambda i: (i, 0))],
        out_specs=pl.BlockSpec((HEADS, tm, HEAD_PAD), lambda i: (0, i, 0)),
        out_shape=jax.ShapeDtypeStruct((HEADS, M_ROWS, HEAD_PAD), BF16),
        compiler_params=_params(1, 40),
        name="q_proj",
    )(proj, q_norm_g.reshape(1, Q_RANK), wuq, cc, ss)


def _latkr_kernel(ckv_ref, h_ref, wkr_ref, g_ref, cc_ref, ss_ref, lat_ref, kr_out_ref, kr128_ref):
    x = ckv_ref[...]
    r = lax.rsqrt(jnp.mean(x * x, axis=-1, keepdims=True) + RMS_EPS)
    lat_ref[...] = x * r * g_ref[...]
    blk = lax.dot_general(h_ref[...], wkr_ref[...], (((1,), (1,)), ((), ())), preferred_element_type=F32)
    rot = _rope128(blk, cc_ref[...], ss_ref[...])
    kr_out_ref[...] = rot[:, :QK_ROPE]
    kr128_ref[...] = rot


def _lat_krope(proj, h, wkr, kv_norm_g, cc, ss, row0, rows):
    tr = 512
    off = row0 // tr
    ckv_block = (2 * GMLP_WIDTH + Q_RANK) // KV_RANK
    return pl.pallas_call(
        _latkr_kernel,
        grid=(rows // tr,),
        in_specs=[pl.BlockSpec((tr, KV_RANK), lambda i: (i + off, ckv_block)),
                  pl.BlockSpec((tr, D_MODEL), lambda i: (i + off, 0)),
                  pl.BlockSpec((LANES, D_MODEL), lambda i: (0, 0)),
                  pl.BlockSpec((1, KV_RANK), lambda i: (0, 0)),
                  pl.BlockSpec((tr, LANES), lambda i: (i + off, 0)),
                  pl.BlockSpec((tr, LANES), lambda i: (i + off, 0))],
        out_specs=[pl.BlockSpec((tr, KV_RANK), lambda i: (i, 0)),
                   pl.BlockSpec((tr, QK_ROPE), lambda i: (i, 0)),
                   pl.BlockSpec((tr, LANES), lambda i: (i, 0))],
        out_shape=[jax.ShapeDtypeStruct((rows, KV_RANK), F32),
                   jax.ShapeDtypeStruct((rows, QK_ROPE), F32),
                   jax.ShapeDtypeStruct((rows, LANES), F32)],
        compiler_params=_params(1, 32),
        name="lat_krope",
    )(proj, h, wkr, kv_norm_g.reshape(1, KV_RANK), cc, ss)


def _kv_t_kernel(lat_ref, kr_ref, w_ref, wvt_ref, k_ref, vt_ref):
    a = lat_ref[...].astype(BF16)
    kr = kr_ref[...].astype(BF16)
    for h in range(HEADS):
        kn = _dot(a, w_ref[:, h * HEAD_PAD:h * HEAD_PAD + QK_NOPE])
        k_ref[h, :, :LANES] = kn.astype(BF16)
        k_ref[h, :, LANES:] = kr
        vt = lax.dot_general(wvt_ref[h], a, (((1,), (1,)), ((), ())), preferred_element_type=F32)
        vt_ref[h, 0] = vt.astype(BF16)


def _kv_proj_t(lat, kr128, wukv, wvt, tr):
    rows = lat.shape[0]
    return pl.pallas_call(
        _kv_t_kernel,
        grid=(rows // tr,),
        in_specs=[pl.BlockSpec((tr, KV_RANK), lambda i: (i, 0)),
                  pl.BlockSpec((tr, LANES), lambda i: (i, 0)),
                  pl.BlockSpec((KV_RANK, HEADS * HEAD_PAD), lambda i: (0, 0)),
                  pl.BlockSpec((HEADS, HEAD_V, KV_RANK), lambda i: (0, 0, 0))],
        out_specs=[pl.BlockSpec((HEADS, tr, HEAD_PAD), lambda i: (0, i, 0)),
                   pl.BlockSpec((HEADS, 1, HEAD_V, tr), lambda i: (0, i, 0, 0))],
        out_shape=[jax.ShapeDtypeStruct((HEADS, rows, HEAD_PAD), BF16),
                   jax.ShapeDtypeStruct((HEADS, rows // tr, HEAD_V, tr), BF16)],
        compiler_params=_params(1, 48),
        name="kv_proj_t",
    )(lat, kr128, wukv, wvt)


ATTN_TILE = 1024


def _softmax_pv_t(s, s_max, vt, carry):
    m, l, acc = carry
    m_new = jnp.maximum(m, s_max)
    p = jnp.exp2(s - m_new)
    a = jnp.exp2(m - m_new)
    l = a * l + jnp.sum(p, axis=0, keepdims=True)
    acc = a * acc + jnp.dot(vt, p.astype(BF16), preferred_element_type=F32)
    return m_new, l, acc


def _chunk_mask_t(q_rel0, nk, nq):
    kc = lax.broadcasted_iota(jnp.int32, (nk, nq), 0) >> CHUNK_SHIFT
    qc = (q_rel0 + lax.broadcasted_iota(jnp.int32, (nk, nq), 1)) >> CHUNK_SHIFT
    return qc >= kc


def _attn_prompt_kernel(q_ref, k_ref, vt_ref, o_init_ref, o_ref, s_even, s_odd, *, rb, nh):
    del o_init_ref
    qi = pl.program_id(1)
    n_rb = ATTN_TILE // rb
    heads = range(nh)

    def scores(j, s_ref):
        k0 = pl.multiple_of(j * ATTN_TILE, ATTN_TILE)
        maxima = []
        for hh in heads:
            s = lax.dot_general(k_ref[hh, pl.ds(k0, ATTN_TILE), :], q_ref[hh], (((1,), (1,)), ((), ())),
                                preferred_element_type=F32)
            s_ref[hh] = s
            maxima.append(jnp.max(s, axis=0, keepdims=True))
        return tuple(maxima)

    def make_step(s_cur, s_next):
        def step(j, carry):
            stats, s_max = carry
            next_max = scores(j + 1, s_next)
            stats = tuple(
                tuple(_softmax_pv_t(s_cur[hh, :, r * rb:(r + 1) * rb], s_max[hh][:, r * rb:(r + 1) * rb],
                                    vt_ref[hh, j], stats[hh][r]) for r in range(n_rb))
                for hh in heads)
            return stats, next_max
        return step

    step_even, step_odd = make_step(s_even, s_odd), make_step(s_odd, s_even)

    def finish(s_ref, stats):
        mask = _chunk_mask_t(0, rb, rb)
        for r in range(n_rb):
            nk = (r + 1) * rb
            cols = slice(r * rb, nk)
            for hh in heads:
                s_r = jnp.where(mask, s_ref[hh, r * rb:nk, cols], NEG)
                if r > 0:
                    s_r = jnp.concatenate([s_ref[hh, :r * rb, cols], s_r], axis=0)
                _, l, acc = _softmax_pv_t(s_r, jnp.max(s_r, axis=0, keepdims=True), vt_ref[hh, qi, :, :nk],
                                          stats[hh][r])
                o_ref[r * rb:(r + 1) * rb, hh * HEAD_V:(hh + 1) * HEAD_V] = (acc / l).T.astype(o_ref.dtype)

    init = tuple(tuple((jnp.full((1, rb), NEG, F32), jnp.zeros((1, rb), F32), jnp.zeros((HEAD_V, rb), F32))
                       for _ in range(n_rb)) for _ in heads)
    carry = (init, scores(0, s_even))
    carry = lax.fori_loop(0, qi >> 1, lambda p, c: step_odd(2 * p + 1, step_even(2 * p, c)), carry)
    odd = (qi & 1) == 1
    stats, _ = lax.cond(odd, lambda: step_even(qi - 1, carry), lambda: carry)
    lax.cond(odd, lambda: finish(s_odd, stats), lambda: finish(s_even, stats))


def _attn_prompt(q, k, vt, o_init):
    nh = 2
    return pl.pallas_call(
        functools.partial(_attn_prompt_kernel, rb=256, nh=nh),
        grid=(HEADS // nh, SEQ // ATTN_TILE),
        in_specs=[pl.BlockSpec((nh, ATTN_TILE, HEAD_PAD), lambda h, i: (h, i, 0)),
                  pl.BlockSpec((nh, SEQ, HEAD_PAD), lambda h, i: (h, 0, 0)),
                  pl.BlockSpec((nh, SEQ // ATTN_TILE, HEAD_V, ATTN_TILE), lambda h, i: (h, 0, 0, 0)),
                  pl.BlockSpec(memory_space=pl.ANY)],
        out_specs=pl.BlockSpec((ATTN_TILE, nh * HEAD_V), lambda h, i: (i, h)),
        out_shape=jax.ShapeDtypeStruct((M_ROWS, HEADS * HEAD_V), BF16),
        scratch_shapes=[pltpu.VMEM((nh, ATTN_TILE, ATTN_TILE), F32), pltpu.VMEM((nh, ATTN_TILE, ATTN_TILE), F32)],
        input_output_aliases={3: 0},
        compiler_params=_params(2, 56),
        name="attn_prompt",
    )(q, k, vt, o_init)


def _attn_sample_kernel(q_ref, cache_ref, new_ref, kr_ref, w_ref, o_prev_ref, o_ref, qa_scr):
    del o_prev_ref
    nt = (((1,), (1,)), ((), ()))
    pad = jnp.zeros((KV_PAD - PAST_LEN - DEC_SEQ, KV_RANK), F32)
    lat = jnp.concatenate([cache_ref[0], new_ref[...], pad], axis=0).astype(BF16)
    keys = jnp.concatenate([lat, kr_ref[0].astype(BF16)], axis=1)
    for h in range(HEADS):
        rows = slice(h * DEC_SEQ, (h + 1) * DEC_SEQ)
        w_uk = w_ref[:, h * HEAD_PAD:h * HEAD_PAD + QK_NOPE]
        q_lat = lax.dot_general(q_ref[h, :, :LANES], w_uk, nt, preferred_element_type=F32)
        qa_scr[rows, :KV_RANK] = q_lat.astype(BF16)
        qa_scr[rows, KV_RANK:] = q_ref[h, :, LANES:]
    s = lax.dot_general(qa_scr[...], keys, nt, preferred_element_type=F32)
    shape = (HEADS * DEC_SEQ, KV_PAD)
    q_pos = PAST_LEN + (lax.broadcasted_iota(jnp.int32, shape, 0) & (DEC_SEQ - 1))
    k_pos = lax.broadcasted_iota(jnp.int32, shape, 1)
    visible = ((q_pos >> CHUNK_SHIFT) >= (k_pos >> CHUNK_SHIFT)) & (k_pos < PAST_LEN + DEC_SEQ)
    s = jnp.where(visible, s, NEG)
    m = jnp.max(s, axis=-1, keepdims=True)
    p = jnp.exp2(s - m)
    l = jnp.sum(p, axis=-1, keepdims=True)
    o_lat = (jnp.dot(p.astype(BF16), lat, preferred_element_type=F32) / l).astype(BF16)
    for h in range(HEADS):
        w_uv = w_ref[:, h * HEAD_PAD + QK_NOPE:(h + 1) * HEAD_PAD]
        o_h = _dot(o_lat[h * DEC_SEQ:(h + 1) * DEC_SEQ, :], w_uv)
        o_ref[:, h * HEAD_V:(h + 1) * HEAD_V] = o_h.astype(o_ref.dtype)


def _attn_sample(q, cache_lat, lat_new, kr_all, wukv, o_prev):
    q_off = SEQ // DEC_SEQ
    return pl.pallas_call(
        _attn_sample_kernel,
        grid=(DEC_BATCH,),
        in_specs=[pl.BlockSpec((HEADS, DEC_SEQ, HEAD_PAD), lambda b: (0, q_off + b, 0)),
                  pl.BlockSpec((1, PAST_LEN, KV_RANK), lambda b: (b, 0, 0)),
                  pl.BlockSpec((DEC_SEQ, KV_RANK), lambda b: (b, 0)),
                  pl.BlockSpec((1, KV_PAD, LANES), lambda b: (b, 0, 0)),
                  pl.BlockSpec((KV_RANK, HEADS * HEAD_PAD), lambda b: (0, 0)),
                  pl.BlockSpec(memory_space=pl.ANY)],
        out_specs=pl.BlockSpec((DEC_SEQ, HEADS * HEAD_V), lambda b: (q_off + b, 0)),
        out_shape=jax.ShapeDtypeStruct((M_ROWS, HEADS * HEAD_V), BF16),
        scratch_shapes=[pltpu.VMEM((HEADS * DEC_SEQ, KV_RANK + LANES), BF16)],
        input_output_aliases={5: 0},
        compiler_params=_params(1, 48),
        name="attn_sample",
    )(q, cache_lat, lat_new, kr_all, wukv, o_prev)


def _gmlp_kernel(u_ref, v_ref, g_ref, b_ref, w_ref, bs_ref, *rest, chunk, emit_vn, n_own):
    if emit_vn:
        _, a_ref, vn_ref = rest
        _gmlp_tile(u_ref, v_ref, g_ref, b_ref, w_ref, bs_ref, a_ref, vn_ref, chunk)
    else:
        (a_ref,) = rest
        i = pl.program_id(0)
        pl.when(i < n_own)(lambda: _gmlp_tile(u_ref, v_ref, g_ref, b_ref, w_ref, bs_ref, a_ref, None, chunk))

        @pl.when(i >= n_own)
        def _():
            a_ref[...] = jnp.zeros(a_ref.shape, a_ref.dtype)


def _gmlp_tile(u_ref, v_ref, g_ref, b_ref, w_ref, bs_ref, a_ref, vn_ref, chunk):
    emit_vn = vn_ref is not None
    v = v_ref[...]
    mu = jnp.mean(v, axis=-1, keepdims=True)
    vc = v - mu
    var = jnp.mean(vc * vc, axis=-1, keepdims=True)
    vn = vc * lax.rsqrt(var + LN_EPS) * g_ref[...] + b_ref[...]
    if emit_vn:
        vn_ref[...] = vn
    vnb = vn.astype(BF16)
    n_chunks = v.shape[0] // chunk
    causal = (lax.broadcasted_iota(jnp.int32, (chunk, GMLP_CHUNK), 0)
              >= lax.broadcasted_iota(jnp.int32, (chunk, GMLP_CHUNK), 1))
    for g in range(GMLP_GROUPS):
        cols = slice(g * GROUP_DIM, (g + 1) * GROUP_DIM)
        w = jnp.where(causal, w_ref[g, :chunk, :], 0.0).astype(BF16)
        rhs = jnp.concatenate([vnb[c * chunk:(c + 1) * chunk, cols] for c in range(n_chunks)], axis=1)
        if chunk < GMLP_CHUNK:
            rhs = jnp.concatenate([rhs, jnp.zeros((GMLP_CHUNK - chunk, rhs.shape[1]), BF16)], axis=0)
        mixed = jnp.dot(w, rhs, preferred_element_type=F32) + bs_ref[g, :chunk, :]
        for c in range(n_chunks):
            rows = slice(c * chunk, (c + 1) * chunk)
            gate = mixed[:, c * GROUP_DIM:(c + 1) * GROUP_DIM]
            a_ref[rows, cols] = (u_ref[rows, cols] * gate).astype(a_ref.dtype)


def _gmlp(proj, ln_g, ln_b, w_s, b_s3, row0, rows, tr, chunk, a_prev=None):
    emit_vn = a_prev is not None
    off = row0 // tr
    n_own = rows // tr
    n_steps = n_own if emit_vn else M_ROWS // tr
    vec = pl.BlockSpec((1, GMLP_WIDTH), lambda i: (0, 0))
    in_specs = [pl.BlockSpec((tr, GMLP_WIDTH), lambda i: (i + off, 0)),
                pl.BlockSpec((tr, GMLP_WIDTH), lambda i: (i + off, 1)),
                vec, vec,
                pl.BlockSpec((GMLP_GROUPS, GMLP_CHUNK, GMLP_CHUNK), lambda i: (0, 0, 0)),
                pl.BlockSpec((GMLP_GROUPS, GMLP_CHUNK, 1), lambda i: (0, 0, 0))]
    args = [proj, proj, ln_g.reshape(1, GMLP_WIDTH), ln_b.reshape(1, GMLP_WIDTH), w_s, b_s3]
    out_specs = [pl.BlockSpec((tr, GMLP_WIDTH), lambda i: (i + off, 0))]
    out_shape = [jax.ShapeDtypeStruct((M_ROWS, GMLP_WIDTH), BF16)]
    aliases = {}
    if emit_vn:
        in_specs.append(pl.BlockSpec(memory_space=pl.ANY))
        args.append(a_prev)
        aliases = {len(args) - 1: 0}
        out_specs.append(pl.BlockSpec((tr, GMLP_WIDTH), lambda i: (i, 0)))
        out_shape.append(jax.ShapeDtypeStruct((rows, GMLP_WIDTH), F32))
    return pl.pallas_call(
        functools.partial(_gmlp_kernel, chunk=chunk, emit_vn=emit_vn, n_own=n_own),
        grid=(n_steps,),
        in_specs=in_specs,
        out_specs=out_specs,
        out_shape=out_shape,
        input_output_aliases=aliases,
        compiler_params=_params(1, 40),
        name="gmlp_gate_vn" if emit_vn else "gmlp_gate",
    )(*args)


def _swap_halves(w):
    half = w.shape[-1] // 2
    return jnp.concatenate([w[..., half:], w[..., :half]], axis=-1)


def _rope_tables():
    pos = jnp.concatenate([jnp.arange(SEQ, dtype=jnp.int32),
                           jnp.tile(PAST_LEN + jnp.arange(DEC_SEQ, dtype=jnp.int32), DEC_BATCH)])
    inv = 1.0 / (ROPE_THETA ** (jnp.arange(0, QK_ROPE, 2, dtype=F32) / QK_ROPE))
    ang = pos.astype(F32)[:, None] * inv[None, :]
    cos, sin = jnp.cos(ang), jnp.sin(ang)
    zero = jnp.zeros((M_ROWS, LANES - QK_ROPE), F32)
    return (jnp.concatenate([cos, cos, zero], axis=1), jnp.concatenate([-sin, sin, zero], axis=1))


def kernel(x_prompt, x_sample, cache_mla_latent, cache_mla_krope, c_prompt, c_sample, w_ada, b_ada, ffn1_w_gate_up, ffn1_w_down, ln1_g, ln1_b, w_in, gmlp_ln_g, gmlp_ln_b, gmlp_w_s, gmlp_b_s, mla_q_norm_g, mla_w_uq, mla_kv_norm_g, mla_w_ukv, w_out, ln2_g, ln2_b, ffn2_w_gate_up, ffn2_w_down, ln3_g, ln3_b):
    wukv = mla_w_ukv.astype(BF16)
    wvt = mla_w_ukv.reshape(KV_RANK, HEADS, HEAD_PAD)[:, :, QK_NOPE:].transpose(1, 2, 0).astype(BF16)
    w_in_t = w_in.T
    w_kr_t = w_in_t[IN_COLS:]
    wkr = jnp.concatenate([w_kr_t, w_kr_t[QK_ROPE // 2:], w_kr_t[:QK_ROPE // 2]], axis=0)
    uq = mla_w_uq.reshape(Q_RANK, HEADS, QK_NOPE + QK_ROPE)
    wuq = jnp.concatenate([uq, _swap_halves(uq[..., QK_NOPE:])], axis=-1).reshape(Q_RANK, HEADS * HEAD_PAD).astype(BF16)
    b_s3 = gmlp_b_s[:, :, None]
    cc, ss = _rope_tables()

    c16 = jnp.concatenate([c_prompt, c_sample, jnp.zeros((C_ROWS - 1 - DEC_BATCH, D_MODEL), F32)], axis=0)
    n_first = 3
    first = _segment_table(_modulation(c16, w_ada, b_ada, 0, n_first))
    SH1, SC1, G1 = ((first, i) for i in range(n_first))

    x0 = (x_prompt.reshape(SEQ, D_MODEL), x_sample.reshape(S_ROWS, D_MODEL))

    h1 = _modulate(x0, SC1, SH1)
    z1, mod_rest = _ffn_down(*_swiglu_up(h1, ffn1_w_gate_up, ffn1_w_down), _raw_residual(*x0), G1,
                             adaln=(c16, w_ada, b_ada, n_first, N_MOD - n_first))
    rest = _segment_table(mod_rest)
    SH2, SC2, G2, SH3, SC3, G3 = ((rest, i) for i in range(N_MOD - n_first))
    h2, mu1, rstd1 = _layernorm(z1, ln1_g, ln1_b, SC2, SH2)

    proj = _matmul_nt(h2, w_in_t, TM, IN_TILE, "in_proj", n_cols=IN_COLS)
    (a_mix,) = _gmlp(proj, gmlp_ln_g, gmlp_ln_b, gmlp_w_s, b_s3, 0, SEQ, 512, GMLP_CHUNK)
    a_mix, vn_s = _gmlp(proj, gmlp_ln_g, gmlp_ln_b, gmlp_w_s, b_s3, SEQ, S_ROWS, DEC_SEQ, DEC_SEQ, a_prev=a_mix)
    q = _q_proj(proj, mla_q_norm_g, wuq, cc, ss)
    lat_p, kr_p, kr128_p = _lat_krope(proj, h2, wkr, mla_kv_norm_g, cc, ss, 0, SEQ)
    lat_s, kr_s, kr128_s = _lat_krope(proj, h2, wkr, mla_kv_norm_g, cc, ss, SEQ, S_ROWS)

    k_p, vt_p = _kv_proj_t(lat_p, kr128_p, wukv, wvt, ATTN_TILE)
    o_mix = _attn_prompt(q, k_p, vt_p, jnp.zeros((M_ROWS, HEADS * HEAD_V), BF16))

    pad_rows = KV_PAD - PAST_LEN - DEC_SEQ
    cache_kr128 = jnp.pad(cache_mla_krope, ((0, 0), (0, 0), (0, LANES - QK_ROPE)))
    kr_all = jnp.concatenate([cache_kr128, kr128_s.reshape(DEC_BATCH, DEC_SEQ, LANES),
                              jnp.zeros((DEC_BATCH, pad_rows, LANES), F32)], axis=1)
    o_mix = _attn_sample(q, cache_mla_latent, lat_s, kr_all, wukv, o_mix)

    z2 = _out_proj(a_mix, o_mix, w_out, _ln_residual(z1, mu1, rstd1, ln1_g, ln1_b), G2)
    h3, mu2, rstd2 = _layernorm(z2, ln2_g, ln2_b, SC3, SH3)

    (z3,) = _ffn_down(*_swiglu_up(h3, ffn2_w_gate_up, ffn2_w_down), _ln_residual(z2, mu2, rstd2, ln2_g, ln2_b), G3)
    (y_p,) = _layernorm(z3, ln3_g, ln3_b, row0=0, rows=SEQ)
    (y_s,) = _layernorm(z3, ln3_g, ln3_b, row0=SEQ, rows=S_ROWS)

    return (y_p.reshape(1, SEQ, D_MODEL),
            y_s.reshape(DEC_BATCH, DEC_SEQ, D_MODEL),
            lat_p.reshape(1, SEQ, KV_RANK),
            kr_p.reshape(1, SEQ, QK_ROPE),
            lat_s.reshape(DEC_BATCH, DEC_SEQ, KV_RANK),
            kr_s.reshape(DEC_BATCH, DEC_SEQ, QK_ROPE),
            vn_s.reshape(DEC_BATCH, DEC_SEQ, GMLP_WIDTH))
```

```python
import functools

import jax
import jax.numpy as jnp
from jax import lax
from jax.experimental import pallas as pl
from jax.experimental.pallas import tpu as pltpu

F32 = jnp.float32
BF16 = jnp.bfloat16

D_MODEL = 4096
SEQ = 8192
DEC_BATCH = 8
DEC_SEQ = 64
PAST_LEN = 2048
CHUNK = 64
CHUNK_SHIFT = CHUNK.bit_length() - 1
GMLP_CHUNK = 128
GROUP_DIM = 128
GMLP_WIDTH = D_MODEL // 2
GMLP_GROUPS = GMLP_WIDTH // GROUP_DIM
HEAD_V = 128
QK_NOPE = 128
QK_ROPE = 64
HEADS = (D_MODEL - GMLP_WIDTH) // HEAD_V
KV_RANK = 512
Q_RANK = D_MODEL // 4
D_FF = 256 * ((8 * D_MODEL // 3 + 255) // 256)
N_MOD = 9
ROPE_THETA = 10000.0
LN_EPS = 1e-5
RMS_EPS = 1e-6
ALPHA = 2.0 ** 0.25
Q_SCALE = (QK_NOPE + QK_ROPE) ** -0.5 * 1.4426950408889634
NEG = -1e30

LANES = 128
HEAD_PAD = 2 * LANES
S_ROWS = DEC_BATCH * DEC_SEQ
M_ROWS = SEQ + S_ROWS
SEG = DEC_SEQ
TM = M_ROWS // 8
TR = 8 * SEG
IN_COLS = 2 * GMLP_WIDTH + Q_RANK + KV_RANK
IN_TILE = 512
KV_PAD = -(-(PAST_LEN + DEC_SEQ) // LANES) * LANES
MIB = 2 ** 20


def _params(n_axes, vmem_mib):
    return pltpu.CompilerParams(dimension_semantics=("arbitrary",) * n_axes,
                                vmem_limit_bytes=vmem_mib * MIB)


C_ROWS = 16


def _adaln_tile(c_ref, w_ref, b_ref):
    c = c_ref[...]
    s = (c * jax.nn.sigmoid(c)).astype(BF16)
    return lax.dot_general(s, w_ref[...], (((1,), (0,)), ((), ())), preferred_element_type=F32) + b_ref[...]


def _adaln_specs(tn, comp0, block_of_step):
    per_comp = D_MODEL // tn

    def col(*idx):
        return comp0 * per_comp + block_of_step(*idx)

    in_specs = [pl.BlockSpec((C_ROWS, D_MODEL), lambda *idx: (0, 0)),
                pl.BlockSpec((D_MODEL, tn), lambda *idx: (0, col(*idx))),
                pl.BlockSpec((1, tn), lambda *idx: (0, col(*idx)))]
    out_spec = pl.BlockSpec((1, C_ROWS, tn),
                            lambda *idx: (block_of_step(*idx) // per_comp, 0, block_of_step(*idx) % per_comp))
    return in_specs, out_spec


def _mod_kernel(c_ref, w_ref, b_ref, o_ref):
    o_ref[0] = _adaln_tile(c_ref, w_ref, b_ref)


def _modulation(c16, w_ada, b_ada, comp0, n_comp):
    tn = 512
    in_specs, out_spec = _adaln_specs(tn, comp0, lambda j: j)
    return pl.pallas_call(
        _mod_kernel,
        grid=(n_comp * D_MODEL // tn,),
        in_specs=in_specs,
        out_specs=out_spec,
        out_shape=jax.ShapeDtypeStruct((n_comp, C_ROWS, D_MODEL), F32),
        compiler_params=_params(1, 40),
        name="adaln_mod",
    )(c16, w_ada, b_ada.reshape(1, -1))


def _segment_table(mod):
    n = mod.shape[0]
    return jnp.concatenate([jnp.broadcast_to(mod[:, 0:1], (n, SEQ // SEG, D_MODEL)), mod[:, 1:1 + DEC_BATCH]], axis=1)


def _seg_operand(entry, tile_rows, width, index_map):
    table, comp = entry
    view = table.reshape(table.shape[0], M_ROWS // tile_rows, tile_rows // SEG, D_MODEL)
    spec = pl.BlockSpec((1, 1, tile_rows // SEG, width), lambda *idx: (comp,) + tuple(index_map(*idx)))
    return view, spec


def _split_x_specs(tile_rows, width, col):
    n_prompt = SEQ // tile_rows
    return [pl.BlockSpec((tile_rows, width), lambda *idx: (jnp.minimum(idx[0], n_prompt - 1), col(*idx))),
            pl.BlockSpec((tile_rows, width), lambda *idx: (jnp.maximum(idx[0] - n_prompt, 0), col(*idx)))]


def _on_owner(x_refs, tile_rows, body):
    if len(x_refs) == 1:
        body(x_refs[0])
    else:
        i = pl.program_id(0)
        n_prompt = SEQ // tile_rows
        pl.when(i < n_prompt)(lambda: body(x_refs[0]))
        pl.when(i >= n_prompt)(lambda: body(x_refs[1]))


def _modulate_kernel(xp_ref, xs_ref, sc_ref, sh_ref, h_ref):
    def body(x_ref):
        for s in range(TR // SEG):
            rows = slice(s * SEG, (s + 1) * SEG)
            h = x_ref[rows, :] * (1.0 + sc_ref[0, 0, s:s + 1, :]) + sh_ref[0, 0, s:s + 1, :]
            h_ref[rows, :] = h.astype(h_ref.dtype)

    _on_owner((xp_ref, xs_ref), TR, body)


def _modulate(xs, sc, sh):
    sc_arr, sc_spec = _seg_operand(sc, TR, D_MODEL, lambda i: (i, 0, 0))
    sh_arr, sh_spec = _seg_operand(sh, TR, D_MODEL, lambda i: (i, 0, 0))
    return pl.pallas_call(
        _modulate_kernel,
        grid=(M_ROWS // TR,),
        in_specs=_split_x_specs(TR, D_MODEL, lambda i: 0) + [sc_spec, sh_spec],
        out_specs=pl.BlockSpec((TR, D_MODEL), lambda i: (i, 0)),
        out_shape=jax.ShapeDtypeStruct((M_ROWS, D_MODEL), BF16),
        compiler_params=_params(1, 48),
        name="modulate",
    )(*xs, sc_arr, sh_arr)


def _raw_residual(*xs):
    return ("raw",) + xs


def _ln_residual(z, mu, rstd, ln_g, ln_b):
    return ("ln", z, mu, rstd, ln_g.reshape(1, -1), ln_b.reshape(1, -1))


def _residual_operands(res, tm, tn):
    tile = pl.BlockSpec((tm, tn), lambda i, j: (i, j))
    if res[0] == "raw":
        xs = list(res[1:])
        return xs, (_split_x_specs(tm, tn, lambda i, j: j) if len(xs) == 2 else [tile])
    stat = pl.BlockSpec((tm, LANES), lambda i, j: (i, 0))
    vec = pl.BlockSpec((1, tn), lambda i, j: (0, j))
    return list(res[1:]), [tile, stat, stat, vec, vec]


def _residual_rows(kind, refs, tile_rows, rows):
    if kind == "ln":
        z_ref, mu_ref, rstd_ref, g_ref, b_ref = refs
        return (z_ref[rows, :] - mu_ref[rows, :1]) * rstd_ref[rows, :1] * g_ref[...] + b_ref[...]
    x = refs[0][rows, :]
    if len(refs) == 2:
        x = jnp.where(pl.program_id(0) < SEQ // tile_rows, x, refs[1][rows, :])
    return x


def _gated_residual(kind, x_refs, tile_rows, acc, gate_ref, gate_scale, z_ref):
    for s in range(acc.shape[0] // SEG):
        rows = slice(s * SEG, (s + 1) * SEG)
        x = _residual_rows(kind, x_refs, tile_rows, rows)
        z_ref[rows, :] = ALPHA * x + (gate_scale * gate_ref[0, 0, s:s + 1, :]) * acc[rows, :]


LN_ROWS = 8


def _ln_kernel(z_ref, g_ref, b_ref, *rest, with_h):
    if with_h:
        sc_ref, sh_ref, h_ref, mu_ref, rstd_ref = rest
    else:
        (xo_ref,) = rest
    grp = (LN_ROWS, D_MODEL)
    g8 = jnp.broadcast_to(g_ref[...], grp)
    b8 = jnp.broadcast_to(b_ref[...], grp)
    for s in range(TR // SEG):
        if with_h:
            sc8 = jnp.broadcast_to(1.0 + sc_ref[0, 0, s:s + 1, :], grp)
            sh8 = jnp.broadcast_to(sh_ref[0, 0, s:s + 1, :], grp)
        for r0 in range(s * SEG, (s + 1) * SEG, LN_ROWS):
            rows = slice(r0, r0 + LN_ROWS)
            z = z_ref[rows, :]
            mu = jnp.mean(z, axis=-1, keepdims=True)
            zc = z - mu
            var = jnp.mean(zc * zc, axis=-1, keepdims=True)
            rstd = lax.rsqrt(var + LN_EPS)
            xn = zc * rstd * g8 + b8
            if with_h:
                h_ref[rows, :] = (xn * sc8 + sh8).astype(h_ref.dtype)
                mu_ref[rows, :] = jnp.broadcast_to(mu, (LN_ROWS, LANES))
                rstd_ref[rows, :] = jnp.broadcast_to(rstd, (LN_ROWS, LANES))
            else:
                xo_ref[rows, :] = xn


def _layernorm(z, ln_g, ln_b, sc=None, sh=None, *, row0=0, rows=M_ROWS):
    with_h = sc is not None
    off = row0 // TR
    vec = pl.BlockSpec((1, D_MODEL), lambda i: (0, 0))
    in_specs = [pl.BlockSpec((TR, D_MODEL), lambda i: (i + off, 0)), vec, vec]
    args = [z, ln_g.reshape(1, D_MODEL), ln_b.reshape(1, D_MODEL)]
    row_out = pl.BlockSpec((TR, D_MODEL), lambda i: (i, 0))
    if with_h:
        for entry in (sc, sh):
            arr, spec = _seg_operand(entry, TR, D_MODEL, lambda i: (i + off, 0, 0))
            in_specs.append(spec)
            args.append(arr)
        stat = pl.BlockSpec((TR, LANES), lambda i: (i, 0))
        out_specs = [row_out, stat, stat]
        out_shape = [jax.ShapeDtypeStruct((rows, D_MODEL), BF16),
                     jax.ShapeDtypeStruct((rows, LANES), F32), jax.ShapeDtypeStruct((rows, LANES), F32)]
    else:
        out_specs = [row_out]
        out_shape = [jax.ShapeDtypeStruct((rows, D_MODEL), F32)]
    return pl.pallas_call(
        functools.partial(_ln_kernel, with_h=with_h),
        grid=(rows // TR,),
        in_specs=in_specs,
        out_specs=out_specs,
        out_shape=out_shape,
        compiler_params=_params(1, 56),
        name="ln_mod" if with_h else "ln",
    )(*args)


def _dot(a, b):
    return lax.dot_general(a, b, (((1,), (0,)), ((), ())), preferred_element_type=F32)


def _gateup_kernel(h_ref, wg_ref, wu_ref, wd_ref, o_ref, wd_out_ref):
    h = h_ref[...]
    g = _dot(h, wg_ref[...])
    u = _dot(h, wu_ref[...])
    o_ref[...] = (g * jax.nn.sigmoid(g) * u).astype(o_ref.dtype)

    @pl.when(pl.program_id(0) == 0)
    def _():
        wd_out_ref[...] = wd_ref[...].astype(wd_out_ref.dtype)


def _swiglu_up(h, wgu, w_down):
    tm = 2 * TM
    tn = 256
    nj = D_FF // tn
    slab = pl.BlockSpec((tn, D_MODEL), lambda i, j: (jnp.where(i == 0, j, nj - 1), 0))
    return pl.pallas_call(
        _gateup_kernel,
        grid=(M_ROWS // tm, nj),
        in_specs=[pl.BlockSpec((tm, D_MODEL), lambda i, j: (i, 0), pipeline_mode=pl.Buffered(1)),
                  pl.BlockSpec((D_MODEL, tn), lambda i, j: (0, j)),
                  pl.BlockSpec((D_MODEL, tn), lambda i, j: (0, j + nj)),
                  slab],
        out_specs=[pl.BlockSpec((tm, tn), lambda i, j: (i, j)), slab],
        out_shape=[jax.ShapeDtypeStruct((M_ROWS, D_FF), BF16),
                   jax.ShapeDtypeStruct((D_FF, D_MODEL), BF16)],
        compiler_params=_params(2, 56),
        name="swiglu_up",
    )(h, wgu, wgu, w_down)


def _mm_nt_kernel(a_ref, bt_ref, o_ref):
    o_ref[...] = lax.dot_general(a_ref[...], bt_ref[...], (((1,), (1,)), ((), ())), preferred_element_type=F32)


def _matmul_nt(a, bt, tm, tn, name, n_cols):
    m, k = a.shape
    return pl.pallas_call(
        _mm_nt_kernel,
        grid=(m // tm, n_cols // tn),
        in_specs=[pl.BlockSpec((tm, k), lambda i, j: (i, 0)),
                  pl.BlockSpec((tn, k), lambda i, j: (j, 0))],
        out_specs=pl.BlockSpec((tm, tn), lambda i, j: (i, j)),
        out_shape=jax.ShapeDtypeStruct((m, n_cols), F32),
        compiler_params=_params(2, 48),
        name=name,
    )(a, bt)


def _down_kernel(*refs, res_kind, n_res, tile_rows, gate_scale, with_adaln):
    a_ref, b_ref = refs[:2]
    x_refs = refs[2:2 + n_res]
    gate_ref = refs[2 + n_res]
    if with_adaln:
        c_ref, w_ref, bias_ref, z_ref, mod_ref = refs[3 + n_res:]
    else:
        (z_ref,) = refs[3 + n_res:]
    acc = _dot(a_ref[...], b_ref[...])
    _gated_residual(res_kind, x_refs, tile_rows, acc, gate_ref, gate_scale, z_ref)
    if with_adaln:
        mod_ref[0] = _adaln_tile(c_ref, w_ref, bias_ref)


def _ffn_down(a, w_down_bf16, res, gate, adaln=None):
    tm, tn = TR, 512
    m, k = a.shape
    n = w_down_bf16.shape[1]
    grid = (m // tm, n // tn)
    gate_arr, gate_spec = _seg_operand(gate, tm, tn, lambda i, j: (i, 0, j))
    res_args, res_specs = _residual_operands(res, tm, tn)
    in_specs = [pl.BlockSpec((tm, k), lambda i, j: (i, 0)),
                pl.BlockSpec((k, tn), lambda i, j: (0, j))] + res_specs + [gate_spec]
    args = [a, w_down_bf16, *res_args, gate_arr]
    out_specs = [pl.BlockSpec((tm, tn), lambda i, j: (i, j))]
    out_shape = [jax.ShapeDtypeStruct((m, n), F32)]
    if adaln is not None:
        c16, w_ada, b_ada, comp0, n_comp = adaln
        side_tn = 256
        n_side = n_comp * D_MODEL // side_tn
        assert n_side <= grid[0] * grid[1]
        side_in, side_out = _adaln_specs(side_tn, comp0, lambda i, j: jnp.minimum(i * grid[1] + j, n_side - 1))
        in_specs += side_in
        args += [c16, w_ada, b_ada.reshape(1, -1)]
        out_specs.append(side_out)
        out_shape.append(jax.ShapeDtypeStruct((n_comp, C_ROWS, D_MODEL), F32))
    return pl.pallas_call(
        functools.partial(_down_kernel, res_kind=res[0], n_res=len(res_args), tile_rows=tm, gate_scale=0.5,
                          with_adaln=adaln is not None),
        grid=grid,
        in_specs=in_specs,
        out_specs=out_specs,
        out_shape=out_shape,
        compiler_params=_params(2, 60),
        name="ffn_down" if adaln is None else "ffn_down_adaln",
    )(*args)


def _out_proj_kernel(a1_ref, a2_ref, b_ref, *refs, res_kind):
    x_refs, gate_ref, z_ref = refs[:-2], refs[-2], refs[-1]
    k1 = a1_ref.shape[1]
    acc = _dot(a1_ref[...], b_ref[:k1, :])
    acc += _dot(a2_ref[...], b_ref[k1:, :])
    _gated_residual(res_kind, x_refs, TM, acc, gate_ref, 1.0, z_ref)


def _out_proj(a1, a2, b, res, gate):
    tm, tn = TM, 512
    m, k1 = a1.shape
    k2 = a2.shape[1]
    n = b.shape[1]
    gate_arr, gate_spec = _seg_operand(gate, tm, tn, lambda i, j: (i, 0, j))
    res_args, res_specs = _residual_operands(res, tm, tn)
    return pl.pallas_call(
        functools.partial(_out_proj_kernel, res_kind=res[0]),
        grid=(m // tm, n // tn),
        in_specs=[pl.BlockSpec((tm, k1), lambda i, j: (i, 0)),
                  pl.BlockSpec((tm, k2), lambda i, j: (i, 0)),
                  pl.BlockSpec((k1 + k2, tn), lambda i, j: (0, j))] + res_specs + [gate_spec],
        out_specs=pl.BlockSpec((tm, tn), lambda i, j: (i, j)),
        out_shape=jax.ShapeDtypeStruct((m, n), F32),
        compiler_params=_params(2, 56),
        name="out_proj",
    )(a1, a2, b, *res_args, gate_arr)


def _rope128(blk, cc, ss):
    return blk * cc + pltpu.roll(blk, QK_ROPE, 1) * ss


def _q_kernel(cq_ref, g_ref, w_ref, cc_ref, ss_ref, o_ref):
    x = cq_ref[...]
    r = lax.rsqrt(jnp.mean(x * x, axis=-1, keepdims=True) + RMS_EPS)
    a = (x * r * g_ref[...]).astype(BF16)
    cc = cc_ref[...]
    ss = ss_ref[...]
    for h in range(HEADS):
        acc = _dot(a, w_ref[:, h * HEAD_PAD:(h + 1) * HEAD_PAD])
        o_ref[h, :, :LANES] = (acc[:, :LANES] * Q_SCALE).astype(o_ref.dtype)
        o_ref[h, :, LANES:] = (_rope128(acc[:, LANES:], cc, ss) * Q_SCALE).astype(o_ref.dtype)


def _q_proj(proj, q_norm_g, wuq, cc, ss):
    tm = TM // 2
    cq_block = (2 * GMLP_WIDTH) // Q_RANK
    return pl.pallas_call(
        _q_kernel,
        grid=(M_ROWS // tm,),
        in_specs=[pl.BlockSpec((tm, Q_RANK), lambda i: (i, cq_block)),
                  pl.BlockSpec((1, Q_RANK), lambda i: (0, 0)),
                  pl.BlockSpec((Q_RANK, HEADS * HEAD_PAD), lambda i: (0, 0)),
                  pl.BlockSpec((tm, LANES), lambda i: (i, 0)),
                  pl.BlockSpec((tm, LANES), lambda i: (i, 0))],
        out_specs=pl.BlockSpec((HEADS, tm, HEAD_PAD), lambda i: (0, i, 0)),
        out_shape=jax.ShapeDtypeStruct((HEADS, M_ROWS, HEAD_PAD), BF16),
        compiler_params=_params(1, 40),
        name="q_proj",
    )(proj, q_norm_g.reshape(1, Q_RANK), wuq, cc, ss)


def _latkr_kernel(ckv_ref, h_ref, wkr_ref, g_ref, cc_ref, ss_ref, lat_ref, kr_out_ref, kr128_ref):
    x = ckv_ref[...]
    r = lax.rsqrt(jnp.mean(x * x, axis=-1, keepdims=True) + RMS_EPS)
    lat_ref[...] = x * r * g_ref[...]
    blk = lax.dot_general(h_ref[...], wkr_ref[...], (((1,), (1,)), ((), ())), preferred_element_type=F32)
    rot = _rope128(blk, cc_ref[...], ss_ref[...])
    kr_out_ref[...] = rot[:, :QK_ROPE]
    kr128_ref[...] = rot


def _lat_krope(proj, h, wkr, kv_norm_g, cc, ss, row0, rows):
    tr = 512
    off = row0 // tr
    ckv_block = (2 * GMLP_WIDTH + Q_RANK) // KV_RANK
    return pl.pallas_call(
        _latkr_kernel,
        grid=(rows // tr,),
        in_specs=[pl.BlockSpec((tr, KV_RANK), lambda i: (i + off, ckv_block)),
                  pl.BlockSpec((tr, D_MODEL), lambda i: (i + off, 0)),
                  pl.BlockSpec((LANES, D_MODEL), lambda i: (0, 0)),
                  pl.BlockSpec((1, KV_RANK), lambda i: (0, 0)),
                  pl.BlockSpec((tr, LANES), lambda i: (i + off, 0)),
                  pl.BlockSpec((tr, LANES), lambda i: (i + off, 0))],
        out_specs=[pl.BlockSpec((tr, KV_RANK), lambda i: (i, 0)),
                   pl.BlockSpec((tr, QK_ROPE), lambda i: (i, 0)),
                   pl.BlockSpec((tr, LANES), lambda i: (i, 0))],
        out_shape=[jax.ShapeDtypeStruct((rows, KV_RANK), F32),
                   jax.ShapeDtypeStruct((rows, QK_ROPE), F32),
                   jax.ShapeDtypeStruct((rows, LANES), F32)],
        compiler_params=_params(1, 32),
        name="lat_krope",
    )(proj, h, wkr, kv_norm_g.reshape(1, KV_RANK), cc, ss)


def _kv_t_kernel(lat_ref, kr_ref, w_ref, wvt_ref, k_ref, vt_ref):
    a = lat_ref[...].astype(BF16)
    kr = kr_ref[...].astype(BF16)
    for h in range(HEADS):
        kn = _dot(a, w_ref[:, h * HEAD_PAD:h * HEAD_PAD + QK_NOPE])
        k_ref[h, :, :LANES] = kn.astype(BF16)
        k_ref[h, :, LANES:] = kr
        vt = lax.dot_general(wvt_ref[h], a, (((1,), (1,)), ((), ())), preferred_element_type=F32)
        vt_ref[h, 0] = vt.astype(BF16)


def _kv_proj_t(lat, kr128, wukv, wvt, tr):
    rows = lat.shape[0]
    return pl.pallas_call(
        _kv_t_kernel,
        grid=(rows // tr,),
        in_specs=[pl.BlockSpec((tr, KV_RANK), lambda i: (i, 0)),
                  pl.BlockSpec((tr, LANES), lambda i: (i, 0)),
                  pl.BlockSpec((KV_RANK, HEADS * HEAD_PAD), lambda i: (0, 0)),
                  pl.BlockSpec((HEADS, HEAD_V, KV_RANK), lambda i: (0, 0, 0))],
        out_specs=[pl.BlockSpec((HEADS, tr, HEAD_PAD), lambda i: (0, i, 0)),
                   pl.BlockSpec((HEADS, 1, HEAD_V, tr), lambda i: (0, i, 0, 0))],
        out_shape=[jax.ShapeDtypeStruct((HEADS, rows, HEAD_PAD), BF16),
                   jax.ShapeDtypeStruct((HEADS, rows // tr, HEAD_V, tr), BF16)],
        compiler_params=_params(1, 48),
        name="kv_proj_t",
    )(lat, kr128, wukv, wvt)


ATTN_TILE = 1024


def _softmax_pv_t(s, s_max, vt, carry):
    m, l, acc = carry
    m_new = jnp.maximum(m, s_max)
    p = jnp.exp2(s - m_new)
    a = jnp.exp2(m - m_new)
    l = a * l + jnp.sum(p, axis=0, keepdims=True)
    acc = a * acc + jnp.dot(vt, p.astype(BF16), preferred_element_type=F32)
    return m_new, l, acc


def _chunk_mask_t(q_rel0, nk, nq):
    kc = lax.broadcasted_iota(jnp.int32, (nk, nq), 0) >> CHUNK_SHIFT
    qc = (q_rel0 + lax.broadcasted_iota(jnp.int32, (nk, nq), 1)) >> CHUNK_SHIFT
    return qc >= kc


def _attn_prompt_kernel(q_ref, k_ref, vt_ref, o_init_ref, o_ref, s_even, s_odd, *, rb, nh):
    del o_init_ref
    qi = pl.program_id(1)
    n_rb = ATTN_TILE // rb
    heads = range(nh)

    def scores(j, s_ref):
        k0 = pl.multiple_of(j * ATTN_TILE, ATTN_TILE)
        maxima = []
        for hh in heads:
            s = lax.dot_general(k_ref[hh, pl.ds(k0, ATTN_TILE), :], q_ref[hh], (((1,), (1,)), ((), ())),
                                preferred_element_type=F32)
            s_ref[hh] = s
            maxima.append(jnp.max(s, axis=0, keepdims=True))
        return tuple(maxima)

    def make_step(s_cur, s_next):
        def step(j, carry):
            stats, s_max = carry
            next_max = scores(j + 1, s_next)
            stats = tuple(
                tuple(_softmax_pv_t(s_cur[hh, :, r * rb:(r + 1) * rb], s_max[hh][:, r * rb:(r + 1) * rb],
                                    vt_ref[hh, j], stats[hh][r]) for r in range(n_rb))
                for hh in heads)
            return stats, next_max
        return step

    step_even, step_odd = make_step(s_even, s_odd), make_step(s_odd, s_even)

    def finish(s_ref, stats):
        mask = _chunk_mask_t(0, rb, rb)
        for r in range(n_rb):
            nk = (r + 1) * rb
            cols = slice(r * rb, nk)
            for hh in heads:
                s_r = jnp.where(mask, s_ref[hh, r * rb:nk, cols], NEG)
                if r > 0:
                    s_r = jnp.concatenate([s_ref[hh, :r * rb, cols], s_r], axis=0)
                _, l, acc = _softmax_pv_t(s_r, jnp.max(s_r, axis=0, keepdims=True), vt_ref[hh, qi, :, :nk],
                                          stats[hh][r])
                o_ref[r * rb:(r + 1) * rb, hh * HEAD_V:(hh + 1) * HEAD_V] = (acc / l).T.astype(o_ref.dtype)

    init = tuple(tuple((jnp.full((1, rb), NEG, F32), jnp.zeros((1, rb), F32), jnp.zeros((HEAD_V, rb), F32))
                       for _ in range(n_rb)) for _ in heads)
    carry = (init, scores(0, s_even))
    carry = lax.fori_loop(0, qi >> 1, lambda p, c: step_odd(2 * p + 1, step_even(2 * p, c)), carry)
    odd = (qi & 1) == 1
    stats, _ = lax.cond(odd, lambda: step_even(qi - 1, carry), lambda: carry)
    lax.cond(odd, lambda: finish(s_odd, stats), lambda: finish(s_even, stats))


def _attn_prompt(q, k, vt, o_init):
    nh = 2
    return pl.pallas_call(
        functools.partial(_attn_prompt_kernel, rb=256, nh=nh),
        grid=(HEADS // nh, SEQ // ATTN_TILE),
        in_specs=[pl.BlockSpec((nh, ATTN_TILE, HEAD_PAD), lambda h, i: (h, i, 0)),
                  pl.BlockSpec((nh, SEQ, HEAD_PAD), lambda h, i: (h, 0, 0)),
                  pl.BlockSpec((nh, SEQ // ATTN_TILE, HEAD_V, ATTN_TILE), lambda h, i: (h, 0, 0, 0)),
                  pl.BlockSpec(memory_space=pl.ANY)],
        out_specs=pl.BlockSpec((ATTN_TILE, nh * HEAD_V), lambda h, i: (i, h)),
        out_shape=jax.ShapeDtypeStruct((M_ROWS, HEADS * HEAD_V), BF16),
        scratch_shapes=[pltpu.VMEM((nh, ATTN_TILE, ATTN_TILE), F32), pltpu.VMEM((nh, ATTN_TILE, ATTN_TILE), F32)],
        input_output_aliases={3: 0},
        compiler_params=_params(2, 56),
        name="attn_prompt",
    )(q, k, vt, o_init)


def _attn_sample_kernel(q_ref, cache_ref, new_ref, kr_ref, w_ref, o_prev_ref, o_ref, qa_scr):
    del o_prev_ref
    nt = (((1,), (1,)), ((), ()))
    pad = jnp.zeros((KV_PAD - PAST_LEN - DEC_SEQ, KV_RANK), F32)
    lat = jnp.concatenate([cache_ref[0], new_ref[...], pad], axis=0).astype(BF16)
    keys = jnp.concatenate([lat, kr_ref[0].astype(BF16)], axis=1)
    for h in range(HEADS):
        rows = slice(h * DEC_SEQ, (h + 1) * DEC_SEQ)
        w_uk = w_ref[:, h * HEAD_PAD:h * HEAD_PAD + QK_NOPE]
        q_lat = lax.dot_general(q_ref[h, :, :LANES], w_uk, nt, preferred_element_type=F32)
        qa_scr[rows, :KV_RANK] = q_lat.astype(BF16)
        qa_scr[rows, KV_RANK:] = q_ref[h, :, LANES:]
    s = lax.dot_general(qa_scr[...], keys, nt, preferred_element_type=F32)
    shape = (HEADS * DEC_SEQ, KV_PAD)
    q_pos = PAST_LEN + (lax.broadcasted_iota(jnp.int32, shape, 0) & (DEC_SEQ - 1))
    k_pos = lax.broadcasted_iota(jnp.int32, shape, 1)
    visible = ((q_pos >> CHUNK_SHIFT) >= (k_pos >> CHUNK_SHIFT)) & (k_pos < PAST_LEN + DEC_SEQ)
    s = jnp.where(visible, s, NEG)
    m = jnp.max(s, axis=-1, keepdims=True)
    p = jnp.exp2(s - m)
    l = jnp.sum(p, axis=-1, keepdims=True)
    o_lat = (jnp.dot(p.astype(BF16), lat, preferred_element_type=F32) / l).astype(BF16)
    for h in range(HEADS):
        w_uv = w_ref[:, h * HEAD_PAD + QK_NOPE:(h + 1) * HEAD_PAD]
        o_h = _dot(o_lat[h * DEC_SEQ:(h + 1) * DEC_SEQ, :], w_uv)
        o_ref[:, h * HEAD_V:(h + 1) * HEAD_V] = o_h.astype(o_ref.dtype)


def _attn_sample(q, cache_lat, lat_new, kr_all, wukv, o_prev):
    q_off = SEQ // DEC_SEQ
    return pl.pallas_call(
        _attn_sample_kernel,
        grid=(DEC_BATCH,),
        in_specs=[pl.BlockSpec((HEADS, DEC_SEQ, HEAD_PAD), lambda b: (0, q_off + b, 0)),
                  pl.BlockSpec((1, PAST_LEN, KV_RANK), lambda b: (b, 0, 0)),
                  pl.BlockSpec((DEC_SEQ, KV_RANK), lambda b: (b, 0)),
                  pl.BlockSpec((1, KV_PAD, LANES), lambda b: (b, 0, 0)),
                  pl.BlockSpec((KV_RANK, HEADS * HEAD_PAD), lambda b: (0, 0)),
                  pl.BlockSpec(memory_space=pl.ANY)],
        out_specs=pl.BlockSpec((DEC_SEQ, HEADS * HEAD_V), lambda b: (q_off + b, 0)),
        out_shape=jax.ShapeDtypeStruct((M_ROWS, HEADS * HEAD_V), BF16),
        scratch_shapes=[pltpu.VMEM((HEADS * DEC_SEQ, KV_RANK + LANES), BF16)],
        input_output_aliases={5: 0},
        compiler_params=_params(1, 48),
        name="attn_sample",
    )(q, cache_lat, lat_new, kr_all, wukv, o_prev)


def _gmlp_kernel(u_ref, v_ref, g_ref, b_ref, w_ref, bs_ref, *rest, chunk, emit_vn, n_own):
    if emit_vn:
        _, a_ref, vn_ref = rest
        _gmlp_tile(u_ref, v_ref, g_ref, b_ref, w_ref, bs_ref, a_ref, vn_ref, chunk)
    else:
        (a_ref,) = rest
        i = pl.program_id(0)
        pl.when(i < n_own)(lambda: _gmlp_tile(u_ref, v_ref, g_ref, b_ref, w_ref, bs_ref, a_ref, None, chunk))

        @pl.when(i >= n_own)
        def _():
            a_ref[...] = jnp.zeros(a_ref.shape, a_ref.dtype)


def _gmlp_tile(u_ref, v_ref, g_ref, b_ref, w_ref, bs_ref, a_ref, vn_ref, chunk):
    emit_vn = vn_ref is not None
    v = v_ref[...]
    mu = jnp.mean(v, axis=-1, keepdims=True)
    vc = v - mu
    var = jnp.mean(vc * vc, axis=-1, keepdims=True)
    vn = vc * lax.rsqrt(var + LN_EPS) * g_ref[...] + b_ref[...]
    if emit_vn:
        vn_ref[...] = vn
    vnb = vn.astype(BF16)
    n_chunks = v.shape[0] // chunk
    causal = (lax.broadcasted_iota(jnp.int32, (chunk, GMLP_CHUNK), 0)
              >= lax.broadcasted_iota(jnp.int32, (chunk, GMLP_CHUNK), 1))
    for g in range(GMLP_GROUPS):
        cols = slice(g * GROUP_DIM, (g + 1) * GROUP_DIM)
        w = jnp.where(causal, w_ref[g, :chunk, :], 0.0).astype(BF16)
        rhs = jnp.concatenate([vnb[c * chunk:(c + 1) * chunk, cols] for c in range(n_chunks)], axis=1)
        if chunk < GMLP_CHUNK:
            rhs = jnp.concatenate([rhs, jnp.zeros((GMLP_CHUNK - chunk, rhs.shape[1]), BF16)], axis=0)
        mixed = jnp.dot(w, rhs, preferred_element_type=F32) + bs_ref[g, :chunk, :]
        for c in range(n_chunks):
            rows = slice(c * chunk, (c + 1) * chunk)
            gate = mixed[:, c * GROUP_DIM:(c + 1) * GROUP_DIM]
            a_ref[rows, cols] = (u_ref[rows, cols] * gate).astype(a_ref.dtype)


def _gmlp(proj, ln_g, ln_b, w_s, b_s3, row0, rows, tr, chunk, a_prev=None):
    emit_vn = a_prev is not None
    off = row0 // tr
    n_own = rows // tr
    n_steps = n_own if emit_vn else M_ROWS // tr
    vec = pl.BlockSpec((1, GMLP_WIDTH), lambda i: (0, 0))
    in_specs = [pl.BlockSpec((tr, GMLP_WIDTH), lambda i: (i + off, 0)),
                pl.BlockSpec((tr, GMLP_WIDTH), lambda i: (i + off, 1)),
                vec, vec,
                pl.BlockSpec((GMLP_GROUPS, GMLP_CHUNK, GMLP_CHUNK), lambda i: (0, 0, 0)),
                pl.BlockSpec((GMLP_GROUPS, GMLP_CHUNK, 1), lambda i: (0, 0, 0))]
    args = [proj, proj, ln_g.reshape(1, GMLP_WIDTH), ln_b.reshape(1, GMLP_WIDTH), w_s, b_s3]
    out_specs = [pl.BlockSpec((tr, GMLP_WIDTH), lambda i: (i + off, 0))]
    out_shape = [jax.ShapeDtypeStruct((M_ROWS, GMLP_WIDTH), BF16)]
    aliases = {}
    if emit_vn:
        in_specs.append(pl.BlockSpec(memory_space=pl.ANY))
        args.append(a_prev)
        aliases = {len(args) - 1: 0}
        out_specs.append(pl.BlockSpec((tr, GMLP_WIDTH), lambda i: (i, 0)))
        out_shape.append(jax.ShapeDtypeStruct((rows, GMLP_WIDTH), F32))
    return pl.pallas_call(
        functools.partial(_gmlp_kernel, chunk=chunk, emit_vn=emit_vn, n_own=n_own),
        grid=(n_steps,),
        in_specs=in_specs,
        out_specs=out_specs,
        out_shape=out_shape,
        input_output_aliases=aliases,
        compiler_params=_params(1, 40),
        name="gmlp_gate_vn" if emit_vn else "gmlp_gate",
    )(*args)


def _swap_halves(w):
    half = w.shape[-1] // 2
    return jnp.concatenate([w[..., half:], w[..., :half]], axis=-1)


def _rope_tables():
    pos = jnp.concatenate([jnp.arange(SEQ, dtype=jnp.int32),
                           jnp.tile(PAST_LEN + jnp.arange(DEC_SEQ, dtype=jnp.int32), DEC_BATCH)])
    inv = 1.0 / (ROPE_THETA ** (jnp.arange(0, QK_ROPE, 2, dtype=F32) / QK_ROPE))
    ang = pos.astype(F32)[:, None] * inv[None, :]
    cos, sin = jnp.cos(ang), jnp.sin(ang)
    zero = jnp.zeros((M_ROWS, LANES - QK_ROPE), F32)
    return (jnp.concatenate([cos, cos, zero], axis=1), jnp.concatenate([-sin, sin, zero], axis=1))


def kernel(x_prompt, x_sample, cache_mla_latent, cache_mla_krope, c_prompt, c_sample, w_ada, b_ada, ffn1_w_gate_up, ffn1_w_down, ln1_g, ln1_b, w_in, gmlp_ln_g, gmlp_ln_b, gmlp_w_s, gmlp_b_s, mla_q_norm_g, mla_w_uq, mla_kv_norm_g, mla_w_ukv, w_out, ln2_g, ln2_b, ffn2_w_gate_up, ffn2_w_down, ln3_g, ln3_b):
    wukv = mla_w_ukv.astype(BF16)
    wvt = mla_w_ukv.reshape(KV_RANK, HEADS, HEAD_PAD)[:, :, QK_NOPE:].transpose(1, 2, 0).astype(BF16)
    w_in_t = w_in.T
    w_kr_t = w_in_t[IN_COLS:]
    wkr = jnp.concatenate([w_kr_t, w_kr_t[QK_ROPE // 2:], w_kr_t[:QK_ROPE // 2]], axis=0)
    uq = mla_w_uq.reshape(Q_RANK, HEADS, QK_NOPE + QK_ROPE)
    wuq = jnp.concatenate([uq, _swap_halves(uq[..., QK_NOPE:])], axis=-1).reshape(Q_RANK, HEADS * HEAD_PAD).astype(BF16)
    b_s3 = gmlp_b_s[:, :, None]
    cc, ss = _rope_tables()

    c16 = jnp.concatenate([c_prompt, c_sample, jnp.zeros((C_ROWS - 1 - DEC_BATCH, D_MODEL), F32)], axis=0)
    n_first = 3
    first = _segment_table(_modulation(c16, w_ada, b_ada, 0, n_first))
    SH1, SC1, G1 = ((first, i) for i in range(n_first))

    x0 = (x_prompt.reshape(SEQ, D_MODEL), x_sample.reshape(S_ROWS, D_MODEL))

    h1 = _modulate(x0, SC1, SH1)
    z1, mod_rest = _ffn_down(*_swiglu_up(h1, ffn1_w_gate_up, ffn1_w_down), _raw_residual(*x0), G1,
                             adaln=(c16, w_ada, b_ada, n_first, N_MOD - n_first))
    rest = _segment_table(mod_rest)
    SH2, SC2, G2, SH3, SC3, G3 = ((rest, i) for i in range(N_MOD - n_first))
    h2, mu1, rstd1 = _layernorm(z1, ln1_g, ln1_b, SC2, SH2)

    proj = _matmul_nt(h2, w_in_t, TM, IN_TILE, "in_proj", n_cols=IN_COLS)
    (a_mix,) = _gmlp(proj, gmlp_ln_g, gmlp_ln_b, gmlp_w_s, b_s3, 0, SEQ, 512, GMLP_CHUNK)
    a_mix, vn_s = _gmlp(proj, gmlp_ln_g, gmlp_ln_b, gmlp_w_s, b_s3, SEQ, S_ROWS, DEC_SEQ, DEC_SEQ, a_prev=a_mix)
    q = _q_proj(proj, mla_q_norm_g, wuq, cc, ss)
    lat_p, kr_p, kr128_p = _lat_krope(proj, h2, wkr, mla_kv_norm_g, cc, ss, 0, SEQ)
    lat_s, kr_s, kr128_s = _lat_krope(proj, h2, wkr, mla_kv_norm_g, cc, ss, SEQ, S_ROWS)

    k_p, vt_p = _kv_proj_t(lat_p, kr128_p, wukv, wvt, ATTN_TILE)
    o_mix = _attn_prompt(q, k_p, vt_p, jnp.zeros((M_ROWS, HEADS * HEAD_V), BF16))

    pad_rows = KV_PAD - PAST_LEN - DEC_SEQ
    cache_kr128 = jnp.pad(cache_mla_krope, ((0, 0), (0, 0), (0, LANES - QK_ROPE)))
    kr_all = jnp.concatenate([cache_kr128, kr128_s.reshape(DEC_BATCH, DEC_SEQ, LANES),
                              jnp.zeros((DEC_BATCH, pad_rows, LANES), F32)], axis=1)
    o_mix = _attn_sample(q, cache_mla_latent, lat_s, kr_all, wukv, o_mix)

    z2 = _out_proj(a_mix, o_mix, w_out, _ln_residual(z1, mu1, rstd1, ln1_g, ln1_b), G2)
    h3, mu2, rstd2 = _layernorm(z2, ln2_g, ln2_b, SC3, SH3)

    (z3,) = _ffn_down(*_swiglu_up(h3, ffn2_w_gate_up, ffn2_w_down), _ln_residual(z2, mu2, rstd2, ln2_g, ln2_b), G3)
    (y_p,) = _layernorm(z3, ln3_g, ln3_b, row0=0, rows=SEQ)
    (y_s,) = _layernorm(z3, ln3_g, ln3_b, row0=SEQ, rows=S_ROWS)

    return (y_p.reshape(1, SEQ, D_MODEL),
            y_s.reshape(DEC_BATCH, DEC_SEQ, D_MODEL),
            lat_p.reshape(1, SEQ, KV_RANK),
            kr_p.reshape(1, SEQ, QK_ROPE),
            lat_s.reshape(DEC_BATCH, DEC_SEQ, KV_RANK),
            kr_s.reshape(DEC_BATCH, DEC_SEQ, QK_ROPE),
            vn_s.reshape(DEC_BATCH, DEC_SEQ, GMLP_WIDTH))
```
